```python
import jax, jax.numpy as jnp
from jax import lax
import numpy as np

D_MODEL = 2048
BATCH = 4
SEQ = 2048
DEPTH = 1
DEC_BATCH = 16
DEC_SEQ = 16
PAST_LEN = 4096

CHUNK = 64
D_RNN = D_MODEL // 2
RG_BLOCKS = 8
RG_BLOCK_W = D_RNN // RG_BLOCKS
CONV_W = 4
RG_C = 8.0
FOX_HEADS = 8
FOX_HEAD_DIM = 128
D_FOX = FOX_HEADS * FOX_HEAD_DIM
D_MIX = D_RNN + D_FOX
D_IN = 2 * D_RNN + 3 * D_FOX + FOX_HEADS
SPLITS = [D_RNN, 2 * D_RNN, 2 * D_RNN + D_FOX, 2 * D_RNN + 2 * D_FOX, 2 * D_RNN + 3 * D_FOX]
Q_BLOCK = 128
N_EXPERTS = 32
TOP_K = 4
D_FF = D_MODEL
SWIGLU_LIMIT = 7.0
SWIGLU_ALPHA = 1.702
MOE_BLOCK = 128
D_PLE = 256
EPS = 1e-6

kernel_name = 'hymba_rglru_fox_moe_stream_step'


def _rmsnorm(x, g):
    xf = x.astype(jnp.float32)
    y = xf * lax.rsqrt(jnp.mean(xf * xf, axis=-1, keepdims=True) + EPS)
    return (y * g.astype(jnp.float32)).astype(x.dtype)


def _causal_conv(x, past, w, b):
    T = x.shape[1]
    xp = jnp.concatenate([past.astype(x.dtype), x], axis=1)
    out = xp[:, 0:T] * w[0]
    for j in range(1, CONV_W):
        out = out + xp[:, j:j + T] * w[j]
    return out + b, xp[:, -(CONV_W - 1):]


def _lin_combine(left, right):
    a_l, b_l = left
    a_r, b_r = right
    return a_l * a_r, a_r * b_l + b_r


def _rglru(xc, h0, w_a, b_a, w_i, b_i, lam):
    B, T, _ = xc.shape
    xf = xc.astype(jnp.float32)
    xb = xf.reshape(B, T, RG_BLOCKS, RG_BLOCK_W)
    r = jax.nn.sigmoid(jnp.einsum('btnc,ncd->btnd', xb, w_a.astype(jnp.float32)).reshape(B, T, D_RNN) + b_a.astype(jnp.float32))
    i = jax.nn.sigmoid(jnp.einsum('btnc,ncd->btnd', xb, w_i.astype(jnp.float32)).reshape(B, T, D_RNN) + b_i.astype(jnp.float32))
    log_a = RG_C * r * jax.nn.log_sigmoid(lam.astype(jnp.float32))
    a = jnp.exp(log_a)
    bx = jnp.sqrt(-jnp.expm1(2.0 * log_a)) * (i * xf)
    bx = bx.at[:, 0].add(a[:, 0] * h0.astype(jnp.float32))
    _, h = lax.associative_scan(_lin_combine, (a, bx), axis=1)
    return h, h[:, -1]


def _fox_attend(q, k, v, c_q, c_k, q_pos, k_pos):
    s = jnp.einsum('bqhd,bshd->bhqs', q, k, preferred_element_type=jnp.float32) * (FOX_HEAD_DIM ** -0.5)
    bias = jnp.swapaxes(c_q, 1, 2)[..., :, None] - jnp.swapaxes(c_k, 1, 2)[..., None, :]
    mask = k_pos[None, :] <= q_pos[:, None]
    prob = jax.nn.softmax(jnp.where(mask, s + bias, -jnp.inf), axis=-1)
    return jnp.einsum('bhqs,bshd->bqhd', prob.astype(v.dtype), v)


def _fox_prompt(q, k, v, logf):
    B, T, H, Dh = q.shape
    nb = T // Q_BLOCK
    c = jnp.cumsum(logf, axis=1)
    pos = jnp.arange(T)
    qb = jnp.moveaxis(q.reshape(B, nb, Q_BLOCK, H, Dh), 1, 0)
    cb = jnp.moveaxis(c.reshape(B, nb, Q_BLOCK, H), 1, 0)
    pb = pos.reshape(nb, Q_BLOCK)
    out = lax.map(lambda a: _fox_attend(a[0], k, v, a[1], c, a[2], pos), (qb, cb, pb))
    return jnp.moveaxis(out, 0, 1).reshape(B, T, H, Dh)


def _fox_sample(q, k, v, logf, pk, pv, plf):
    P = pk.shape[1]
    T = q.shape[1]
    k_all = jnp.concatenate([pk.astype(k.dtype), k], axis=1)
    v_all = jnp.concatenate([pv.astype(v.dtype), v], axis=1)
    c_all = jnp.cumsum(jnp.concatenate([plf.astype(jnp.float32), logf], axis=1), axis=1)
    q_pos = P + jnp.arange(T)
    k_pos = jnp.arange(P + T)
    return _fox_attend(q, k_all, v_all, c_all[:, P:], c_all, q_pos, k_pos)


def _moe(xn, router_w, router_b, w_gu, b_gu, w_down, b_down):
    B, T, D = xn.shape
    n_tok = B * T
    xt = xn.reshape(n_tok, D)
    logits = jnp.dot(xt, router_w, preferred_element_type=jnp.float32) + router_b.astype(jnp.float32)
    top_logit, top_e = lax.top_k(logits, TOP_K)
    gate = jax.nn.softmax(top_logit, axis=-1)
    n_rows = n_tok * TOP_K
    flat_e = top_e.reshape(n_rows)
    flat_tok = jnp.arange(n_rows, dtype=jnp.int32) // TOP_K
    flat_gate = gate.reshape(n_rows)
    order = jnp.argsort(flat_e)
    se = flat_e[order]
    counts = jnp.bincount(flat_e, length=N_EXPERTS)
    padded = (counts + MOE_BLOCK - 1) // MOE_BLOCK * MOE_BLOCK
    pad_end = jnp.cumsum(padded)
    pad_start = pad_end - padded
    start = jnp.cumsum(counts) - counts
    dest = pad_start[se] + jnp.arange(n_rows, dtype=jnp.int32) - start[se]
    n_blocks = (n_rows + N_EXPERTS * (MOE_BLOCK - 1) + MOE_BLOCK - 1) // MOE_BLOCK
    n_pad = n_blocks * MOE_BLOCK
    row_tok = jnp.zeros((n_pad,), jnp.int32).at[dest].set(flat_tok[order])
    row_gate = jnp.zeros((n_pad,), jnp.float32).at[dest].set(flat_gate[order])
    block_e = jnp.minimum(jnp.searchsorted(pad_end, jnp.arange(n_blocks, dtype=jnp.int32) * MOE_BLOCK, side='right'), N_EXPERTS - 1)
    xb = xt[row_tok].reshape(n_blocks, MOE_BLOCK, D)

    def expert_block(args):
        xblk, e = args
        gu = jnp.dot(xblk, w_gu[e]) + b_gu[e]
        g = jnp.minimum(gu[:, :D_FF], SWIGLU_LIMIT)
        u = jnp.clip(gu[:, D_FF:], -SWIGLU_LIMIT, SWIGLU_LIMIT)
        hdn = (u + 1.0) * (g * jax.nn.sigmoid(SWIGLU_ALPHA * g))
        return jnp.dot(hdn, w_down[e]) + b_down[e]

    yb = lax.map(expert_block, (xb, block_e))
    y = jax.ops.segment_sum(yb.reshape(n_pad, D).astype(jnp.float32) * row_gate[:, None], row_tok, num_segments=n_tok)
    return y.astype(xn.dtype).reshape(B, T, D)


def _layer(x, p, conv_past, h0, kv_past, lw):
    (g_mix, w_in, b_fgate, conv_w, conv_b, rg_w_a, rg_b_a, rg_w_i, rg_b_i, rg_lambda,
     g_out_rnn, g_out_fox, w_out, g_moe, router_w, router_b, w_gu, b_gu, w_down, b_down,
     w_ple_proj, g_ple_proj, g_ple_gate, w_ple_gate) = lw
    B, T, _ = x.shape
    xn = _rmsnorm(x, g_mix)
    proj = jnp.dot(xn, w_in)
    xr, yr, q, k, v, fl = jnp.split(proj, SPLITS, axis=-1)
    xc, conv_new = _causal_conv(xr, conv_past, conv_w, conv_b)
    hseq, h_last = _rglru(xc, h0, rg_w_a, rg_b_a, rg_w_i, rg_b_i, rg_lambda)
    o_rnn = (hseq * jax.nn.gelu(yr.astype(jnp.float32))).astype(x.dtype)
    q = q.reshape(B, T, FOX_HEADS, FOX_HEAD_DIM)
    k = k.reshape(B, T, FOX_HEADS, FOX_HEAD_DIM)
    v = v.reshape(B, T, FOX_HEADS, FOX_HEAD_DIM)
    logf = jax.nn.log_sigmoid(fl.astype(jnp.float32) + b_fgate.astype(jnp.float32))
    if kv_past is None:
        att = _fox_prompt(q, k, v, logf)
    else:
        att = _fox_sample(q, k, v, logf, kv_past[0], kv_past[1], kv_past[2])
    o_fox = att.reshape(B, T, D_FOX)
    mixed = jnp.concatenate([_rmsnorm(o_rnn, g_out_rnn), _rmsnorm(o_fox, g_out_fox)], axis=-1)
    h = x + jnp.dot(mixed, w_out)
    h = h + _moe(_rmsnorm(h, g_moe), router_w, router_b, w_gu, b_gu, w_down, b_down)
    e = _rmsnorm(jnp.dot(p.astype(h.dtype), w_ple_proj), g_ple_proj)
    gate = jax.nn.sigmoid(jnp.dot(_rmsnorm(h, g_ple_gate), w_ple_gate, preferred_element_type=jnp.float32))
    h = h + (gate * e.astype(jnp.float32)).astype(h.dtype)
    state = (k, v, logf.astype(x.dtype), conv_new, h_last.astype(h0.dtype))
    return h, state


def setup_inputs(seed: int = 0) -> dict:
    key = jax.random.key(seed)
    ks = iter(jax.random.split(key, 40))
    f32 = jnp.float32

    def nrm(shape, scale=1.0):
        return jax.random.normal(next(ks), shape, f32) * scale

    def gain(shape):
        return 1.0 + 0.05 * jax.random.normal(next(ks), shape, f32)

    cache_logf = jax.nn.log_sigmoid(jax.random.uniform(next(ks), (DEPTH, DEC_BATCH, PAST_LEN, FOX_HEADS), f32, 2.0, 7.0)
                                    + nrm((DEPTH, DEC_BATCH, PAST_LEN, FOX_HEADS)))
    a0 = jax.random.uniform(next(ks), (DEPTH, D_RNN), f32, 0.9, 0.999)
    s0 = a0 ** (1.0 / RG_C)
    rg_lambda = jnp.log(s0) - jnp.log1p(-s0)
    b_fgate = jax.random.uniform(next(ks), (DEPTH, FOX_HEADS), f32, 2.0, 7.0)
    return {
        'x_prompt': nrm((BATCH, SEQ, D_MODEL)),
        'x_sample': nrm((DEC_BATCH, DEC_SEQ, D_MODEL)),
        'cache_k': nrm((DEPTH, DEC_BATCH, PAST_LEN, FOX_HEADS, FOX_HEAD_DIM)),
        'cache_v': nrm((DEPTH, DEC_BATCH, PAST_LEN, FOX_HEADS, FOX_HEAD_DIM)),
        'cache_logf': cache_logf,
        'state_conv': nrm((DEPTH, DEC_BATCH, CONV_W - 1, D_RNN)),
        'state_rglru': nrm((DEPTH, DEC_BATCH, D_RNN), 0.5),
        'p_prompt': nrm((DEPTH, BATCH, SEQ, D_PLE)),
        'p_sample': nrm((DEPTH, DEC_BATCH, DEC_SEQ, D_PLE)),
        'g_mix': gain((DEPTH, D_MODEL)),
        'w_in': nrm((DEPTH, D_MODEL, D_IN), D_MODEL ** -0.5),
        'b_fgate': b_fgate,
        'conv_w': nrm((DEPTH, CONV_W, D_RNN), CONV_W ** -0.5),
        'conv_b': nrm((DEPTH, D_RNN), 0.02),
        'rg_w_a': nrm((DEPTH, RG_BLOCKS, RG_BLOCK_W, RG_BLOCK_W), RG_BLOCK_W ** -0.5),
        'rg_b_a': nrm((DEPTH, D_RNN), 0.02),
        'rg_w_i': nrm((DEPTH, RG_BLOCKS, RG_BLOCK_W, RG_BLOCK_W), RG_BLOCK_W ** -0.5),
        'rg_b_i': nrm((DEPTH, D_RNN), 0.02),
        'rg_lambda': rg_lambda,
        'g_out_rnn': gain((DEPTH, D_RNN)),
        'g_out_fox': gain((DEPTH, D_FOX)),
        'w_out': nrm((DEPTH, D_MIX, D_MODEL), D_MIX ** -0.5),
        'g_moe': gain((DEPTH, D_MODEL)),
        'router_w': nrm((DEPTH, D_MODEL, N_EXPERTS), D_MODEL ** -0.5),
        'router_b': nrm((DEPTH, N_EXPERTS), 0.01),
        'w_gu': nrm((DEPTH, N_EXPERTS, D_MODEL, 2 * D_FF), D_MODEL ** -0.5),
        'b_gu': nrm((DEPTH, N_EXPERTS, 2 * D_FF), 0.02),
        'w_down': nrm((DEPTH, N_EXPERTS, D_FF, D_MODEL), D_FF ** -0.5),
        'b_down': nrm((DEPTH, N_EXPERTS, D_MODEL), 0.02),
        'w_ple_proj': nrm((DEPTH, D_PLE, D_MODEL), D_PLE ** -0.5),
        'g_ple_proj': gain((DEPTH, D_MODEL)),
        'g_ple_gate': gain((DEPTH, D_MODEL)),
        'w_ple_gate': nrm((DEPTH, D_MODEL, D_MODEL), D_MODEL ** -0.5),
        'g_final': gain((D_MODEL,)),
    }


def reference(x_prompt, x_sample, cache_k, cache_v, cache_logf, state_conv, state_rglru, p_prompt, p_sample,
              g_mix, w_in, b_fgate, conv_w, conv_b, rg_w_a, rg_b_a, rg_w_i, rg_b_i, rg_lambda,
              g_out_rnn, g_out_fox, w_out, g_moe, router_w, router_b, w_gu, b_gu, w_down, b_down,
              w_ple_proj, g_ple_proj, g_ple_gate, w_ple_gate, g_final):
    assert x_sample.shape[1] <= CHUNK
    bp = x_prompt.shape[0]
    hp, hs = x_prompt, x_sample
    st_p, st_s = [], []
    for i in range(DEPTH):
        lw = (g_mix[i], w_in[i], b_fgate[i], conv_w[i], conv_b[i], rg_w_a[i], rg_b_a[i], rg_w_i[i], rg_b_i[i],
              rg_lambda[i], g_out_rnn[i], g_out_fox[i], w_out[i], g_moe[i], router_w[i], router_b[i],
              w_gu[i], b_gu[i], w_down[i], b_down[i], w_ple_proj[i], g_ple_proj[i], g_ple_gate[i], w_ple_gate[i])
        conv0 = jnp.zeros((bp, CONV_W - 1, D_RNN), x_prompt.dtype)
        h00 = jnp.zeros((bp, D_RNN), state_rglru.dtype)
        hp, sp = _layer(hp, p_prompt[i], conv0, h00, None, lw)
        hs, ss = _layer(hs, p_sample[i], state_conv[i], state_rglru[i], (cache_k[i], cache_v[i], cache_logf[i]), lw)
        st_p.append(sp)
        st_s.append(ss)
    y_prompt = _rmsnorm(hp, g_final)
    y_sample = _rmsnorm(hs, g_final)
    k_prompt = jnp.stack([s[0] for s in st_p])
    v_prompt = jnp.stack([s[1] for s in st_p])
    logf_prompt = jnp.stack([s[2] for s in st_p])
    conv_prompt = jnp.stack([s[3] for s in st_p])
    rglru_prompt = jnp.stack([s[4] for s in st_p])
    k_sample = jnp.stack([s[0] for s in st_s])
    v_sample = jnp.stack([s[1] for s in st_s])
    logf_sample = jnp.stack([s[2] for s in st_s])
    conv_sample = jnp.stack([s[3] for s in st_s])
    rglru_sample = jnp.stack([s[4] for s in st_s])
    return (y_prompt, y_sample, k_prompt, v_prompt, logf_prompt, conv_prompt, rglru_prompt,
            k_sample, v_sample, logf_sample, conv_sample, rglru_sample)
```

```python
import functools

import jax
import jax.numpy as jnp
from jax import lax
from jax.experimental import pallas as pl
from jax.experimental.pallas import tpu as pltpu

F32 = jnp.float32
BF16 = jnp.bfloat16
I32 = jnp.int32

D_MODEL = 2048
BATCH = 4
SEQ = 2048
DEC_BATCH = 16
DEC_SEQ = 16
PAST_LEN = 4096
D_RNN = 1024
RG_BLOCKS = 8
RG_BLOCK_W = 128
CONV_W = 4
RG_C = 8.0
FOX_HEADS = 8
FOX_HEAD_DIM = 128
D_FOX = 1024
D_MAIN = 2 * D_RNN + 3 * D_FOX
N_EXPERTS = 32
TOP_K = 4
D_FF = 2048
SWIGLU_LIMIT = 7.0
SWIGLU_ALPHA = 1.702
D_PLE = 256
EPS = 1e-6

LANES = 128
N_P = BATCH * SEQ
N_S = DEC_BATCH * DEC_SEQ
N_TOK = N_P + N_S
TM = 256
N_TILES = N_TOK // TM
P_TILES = N_P // TM
N_ROWS = N_TOK * TOP_K
EXP_BLK = 256
N_BLOCKS = (N_ROWS + N_EXPERTS * (EXP_BLK - 1) + EXP_BLK - 1) // EXP_BLK
N_PAD = N_BLOCKS * EXP_BLK
NEG = -1e30
VMEM_LIMIT = 48 * 1024 * 1024


def _cparams(sem):
    return pltpu.CompilerParams(dimension_semantics=sem, vmem_limit_bytes=VMEM_LIMIT)


def _rms(x, g):
    return x * lax.rsqrt(jnp.mean(x * x, axis=-1, keepdims=True) + EPS) * g


def _log_sigmoid(z):
    return jnp.minimum(z, 0.0) - jnp.log1p(jnp.exp(-jnp.abs(z)))


def _sigmoid(z):
    return 1.0 / (1.0 + jnp.exp(-z))


def _expm1(x):
    u = jnp.exp(x)
    degenerate = jnp.logical_or(u == 1.0, u == 0.0)
    val = (u - 1.0) * x / jnp.log(jnp.where(degenerate, 0.5, u))
    return jnp.where(u == 1.0, x, jnp.where(u == 0.0, -1.0, val))


def _gelu_tanh(y):
    return 0.5 * y * (1.0 + jnp.tanh(0.7978845608028654 * (y + 0.044715 * (y * y * y))))


def _dot(a, b):
    return jnp.dot(a, b, preferred_element_type=F32)


def _dot_nt(a, b):
    return lax.dot_general(a, b, (((1,), (1,)), ((), ())), preferred_element_type=F32)


IN_TN = 512
IN_NJ = D_MAIN // IN_TN


def _in_proj_kernel(x_ref, g_ref, w_ref, wf_ref, bf_ref,
                    xy_ref, q_ref, kp_ref, ks_ref, vp_ref, vs_ref, logf_ref, xn_scr):
    i = pl.program_id(0)
    j = pl.program_id(1)

    @pl.when(j == 0)
    def _():
        xb = _rms(x_ref[...], g_ref[...]).astype(BF16)
        xn_scr[...] = xb
        logf_ref[...] = _log_sigmoid(_dot(xb, wf_ref[...]) + bf_ref[...])

    r = _dot(xn_scr[...], w_ref[...])
    is_prompt = i < P_TILES
    for jj in range(IN_NJ):
        if jj < 4:
            @pl.when(j == jj)
            def _(jj=jj):
                xy_ref[:, jj * IN_TN:(jj + 1) * IN_TN] = r
        elif jj < 6:
            @pl.when(j == jj)
            def _(jj=jj):
                q_ref[:, (jj - 4) * IN_TN:(jj - 3) * IN_TN] = r
        else:
            p_ref, s_ref, j0 = (kp_ref, ks_ref, 6) if jj < 8 else (vp_ref, vs_ref, 8)

            @pl.when(jnp.logical_and(j == jj, is_prompt))
            def _(jj=jj, p_ref=p_ref, j0=j0):
                p_ref[:, (jj - j0) * IN_TN:(jj - j0 + 1) * IN_TN] = r

            @pl.when(jnp.logical_and(j == jj, jnp.logical_not(is_prompt)))
            def _(jj=jj, s_ref=s_ref, j0=j0):
                s_ref[:, (jj - j0) * IN_TN:(jj - j0 + 1) * IN_TN] = r


def _in_proj(x_all, g_mix, w_main, w_f, b_f):
    row = lambda i, j: (i, 0)
    prow = lambda i, j: (jnp.minimum(i, P_TILES - 1), 0)
    srow = lambda i, j: (jnp.maximum(i - P_TILES, 0), 0)
    const = lambda i, j: (0, 0)
    return pl.pallas_call(
        _in_proj_kernel,
        grid=(N_TILES, IN_NJ),
        in_specs=[
            pl.BlockSpec((TM, D_MODEL), row),
            pl.BlockSpec((1, D_MODEL), const),
            pl.BlockSpec((D_MODEL, IN_TN), lambda i, j: (0, j)),
            pl.BlockSpec((D_MODEL, LANES), const),
            pl.BlockSpec((1, LANES), const),
        ],
        out_specs=[
            pl.BlockSpec((TM, 2 * D_RNN), row),
            pl.BlockSpec((TM, D_FOX), row),
            pl.BlockSpec((TM, D_FOX), prow),
            pl.BlockSpec((TM, D_FOX), srow),
            pl.BlockSpec((TM, D_FOX), prow),
            pl.BlockSpec((TM, D_FOX), srow),
            pl.BlockSpec((TM, LANES), row),
        ],
        out_shape=[
            jax.ShapeDtypeStruct((N_TOK, 2 * D_RNN), F32),
            jax.ShapeDtypeStruct((N_TOK, D_FOX), F32),
            jax.ShapeDtypeStruct((N_P, D_FOX), F32),
            jax.ShapeDtypeStruct((N_S, D_FOX), F32),
            jax.ShapeDtypeStruct((N_P, D_FOX), F32),
            jax.ShapeDtypeStruct((N_S, D_FOX), F32),
            jax.ShapeDtypeStruct((N_TOK, LANES), F32),
        ],
        scratch_shapes=[pltpu.VMEM((TM, D_MODEL), BF16)],
        compiler_params=_cparams(("arbitrary", "arbitrary")),
        name="in_proj",
    )(x_all, g_mix, w_main, w_f, b_f)


def _prefix_steps(n):
    s = 1
    while s < n:
        yield s
        s *= 2


def _rglru_kernel(x_ref, y_ref, past_ref, h0_ref, cw_ref, cb_ref, wa_ref, ba_ref, wi_ref, bi_ref, lam_ref,
                  o_ref, conv_ref, hlast_ref, xp_scr, h_scr, *, tt, nt):
    t = pl.program_id(1)
    pad = 8

    @pl.when(t == 0)
    def _():
        xp_scr[pad - (CONV_W - 1):pad, :] = past_ref[0]
        h_scr[...] = h0_ref[0]

    xp_scr[pad:pad + tt, :] = x_ref[...]
    xc = cb_ref[...] + xp_scr[pad - 3:pad - 3 + tt, :] * cw_ref[0:1, :]
    for jw in range(1, CONV_W):
        xc = xc + xp_scr[pad - 3 + jw:pad - 3 + jw + tt, :] * cw_ref[jw:jw + 1, :]

    r_parts, i_parts = [], []
    for n in range(RG_BLOCKS):
        xb = xc[:, n * RG_BLOCK_W:(n + 1) * RG_BLOCK_W].astype(BF16)
        r_parts.append(_dot(xb, wa_ref[n]))
        i_parts.append(_dot(xb, wi_ref[n]))
    r = _sigmoid(jnp.concatenate(r_parts, axis=-1) + ba_ref[...])
    ig = _sigmoid(jnp.concatenate(i_parts, axis=-1) + bi_ref[...])
    log_a = RG_C * r * _log_sigmoid(lam_ref[...])
    a = jnp.exp(log_a)
    b = jnp.sqrt(-_expm1(2.0 * log_a)) * (ig * xc)

    rowi = lax.broadcasted_iota(I32, (tt, D_RNN), 0)
    for s in _prefix_steps(tt):
        keep = rowi >= s
        a_sh = jnp.where(keep, pltpu.roll(a, s, 0), 1.0)
        b_sh = jnp.where(keep, pltpu.roll(b, s, 0), 0.0)
        b = a * b_sh + b
        a = a * a_sh
    h = a * h_scr[...] + b
    h_scr[...] = h[tt - 1:tt, :]
    o_ref[...] = h * _gelu_tanh(y_ref[...])

    tail = xp_scr[pad + tt - (CONV_W - 1):pad + tt, :]
    xp_scr[pad - (CONV_W - 1):pad, :] = tail

    @pl.when(t == nt - 1)
    def _():
        conv_ref[0] = tail
        hlast_ref[0] = h[tt - 1:tt, :]


def _rglru(xy, row0, nb, nt, tt, conv_past, h0, cw, cb, wa, ba, wi, bi, lam, name):
    blk0 = row0 // tt
    vec = lambda b, t: (0, 0)
    w3 = lambda b, t: (0, 0, 0)
    kern = functools.partial(_rglru_kernel, tt=tt, nt=nt)
    return pl.pallas_call(
        kern,
        grid=(nb, nt),
        in_specs=[
            pl.BlockSpec((tt, D_RNN), lambda b, t: (blk0 + b * nt + t, 0)),
            pl.BlockSpec((tt, D_RNN), lambda b, t: (blk0 + b * nt + t, 1)),
            pl.BlockSpec((1, CONV_W - 1, D_RNN), lambda b, t: (b, 0, 0)),
            pl.BlockSpec((1, 1, D_RNN), lambda b, t: (b, 0, 0)),
            pl.BlockSpec((CONV_W, D_RNN), vec),
            pl.BlockSpec((1, D_RNN), vec),
            pl.BlockSpec((RG_BLOCKS, RG_BLOCK_W, RG_BLOCK_W), w3),
            pl.BlockSpec((1, D_RNN), vec),
            pl.BlockSpec((RG_BLOCKS, RG_BLOCK_W, RG_BLOCK_W), w3),
            pl.BlockSpec((1, D_RNN), vec),
            pl.BlockSpec((1, D_RNN), vec),
        ],
        out_specs=[
            pl.BlockSpec((tt, D_RNN), lambda b, t: (b * nt + t, 0)),
            pl.BlockSpec((1, CONV_W - 1, D_RNN), lambda b, t: (b, 0, 0)),
            pl.BlockSpec((1, 1, D_RNN), lambda b, t: (b, 0, 0)),
        ],
        out_shape=[
            jax.ShapeDtypeStruct((nb * nt * tt, D_RNN), F32),
            jax.ShapeDtypeStruct((nb, CONV_W - 1, D_RNN), F32),
            jax.ShapeDtypeStruct((nb, 1, D_RNN), F32),
        ],
        scratch_shapes=[pltpu.VMEM((tt + 8, D_RNN), F32), pltpu.VMEM((1, D_RNN), F32)],
        compiler_params=_cparams(("arbitrary", "arbitrary")),
        name=name,
    )(xy, xy, conv_past, h0, cw, cb, wa, ba, wi, bi, lam)


def _cumsum_kernel(x_ref, o_ref, *, n):
    x = x_ref[...]
    rowi = lax.broadcasted_iota(I32, x.shape, 0)
    for s in _prefix_steps(n):
        x = x + jnp.where(rowi >= s, pltpu.roll(x, s, 0), 0.0)
    o_ref[...] = x


def _cumsum_rows(x, name):
    return pl.pallas_call(
        functools.partial(_cumsum_kernel, n=x.shape[0]),
        out_shape=jax.ShapeDtypeStruct(x.shape, F32),
        compiler_params=pltpu.CompilerParams(vmem_limit_bytes=VMEM_LIMIT),
        name=name,
    )(x)


FP_T = 512
FP_NT = SEQ // FP_T
FOX_SCALE = FOX_HEAD_DIM ** -0.5


def _fox_prompt_kernel(q_ref, k_ref, v_ref, cq_ref, ck_ref, o_ref, m_scr, l_scr, acc_scr):
    qi = pl.program_id(1)
    ki = pl.program_id(2)

    @pl.when(ki == 0)
    def _():
        m_scr[...] = jnp.full(m_scr.shape, NEG, F32)
        l_scr[...] = jnp.zeros(l_scr.shape, F32)
        acc_scr[...] = jnp.zeros(acc_scr.shape, F32)

    @pl.when(ki <= qi)
    def _():
        s = _dot_nt(q_ref[...].astype(BF16), k_ref[...].astype(BF16)) * FOX_SCALE
        s = s + (cq_ref[0] - ck_ref[0])
        rowi = lax.broadcasted_iota(I32, (FP_T, FP_T), 0)
        coli = lax.broadcasted_iota(I32, (FP_T, FP_T), 1)
        s = jnp.where(jnp.logical_or(ki < qi, coli <= rowi), s, NEG)
        m_old = m_scr[...]
        m_new = jnp.maximum(m_old, jnp.max(s, axis=-1, keepdims=True))
        alpha = jnp.exp(m_old - m_new)
        p = jnp.exp(s - m_new)
        l_scr[...] = alpha * l_scr[...] + jnp.sum(p, axis=-1, keepdims=True)
        acc_scr[...] = alpha * acc_scr[...] + _dot(p.astype(BF16), v_ref[...].astype(BF16))
        m_scr[...] = m_new

    @pl.when(ki == FP_NT - 1)
    def _():
        o_ref[...] = acc_scr[...] / l_scr[...]


def _fox_prompt(q_all, k_p, v_p, c_col, c_row):
    qmap = lambda bh, qi, ki: ((bh // FOX_HEADS) * FP_NT + qi, bh % FOX_HEADS)
    kmap = lambda bh, qi, ki: ((bh // FOX_HEADS) * FP_NT + jnp.minimum(ki, qi), bh % FOX_HEADS)
    return pl.pallas_call(
        _fox_prompt_kernel,
        grid=(BATCH * FOX_HEADS, FP_NT, FP_NT),
        in_specs=[
            pl.BlockSpec((FP_T, FOX_HEAD_DIM), qmap),
            pl.BlockSpec((FP_T, FOX_HEAD_DIM), kmap),
            pl.BlockSpec((FP_T, FOX_HEAD_DIM), kmap),
            pl.BlockSpec((1, FP_T, 1), lambda bh, qi, ki: (bh, qi, 0)),
            pl.BlockSpec((1, 1, FP_T), lambda bh, qi, ki: (bh, 0, jnp.minimum(ki, qi))),
        ],
        out_specs=pl.BlockSpec((FP_T, FOX_HEAD_DIM), qmap),
        out_shape=jax.ShapeDtypeStruct((N_P, D_FOX), F32),
        scratch_shapes=[pltpu.VMEM((FP_T, 1), F32), pltpu.VMEM((FP_T, 1), F32),
                        pltpu.VMEM((FP_T, FOX_HEAD_DIM), F32)],
        compiler_params=_cparams(("arbitrary", "arbitrary", "arbitrary")),
        name="fox_prompt",
    )(q_all, k_p, v_p, c_col, c_row)


FS_TK = 1024
FS_NT = PAST_LEN // FS_TK
FS_ROWS = DEC_SEQ * FOX_HEADS


def _fox_sample_kernel(q_ref, kc_ref, vc_ref, kn_ref, vn_ref, cq_ref, ckp_ref, ckn_ref, o_ref,
                       qbd_scr, m_scr, l_scr, acc_scr):
    kt = pl.program_id(1)
    rowi = lax.broadcasted_iota(I32, (FS_ROWS, D_FOX), 0)
    coli = lax.broadcasted_iota(I32, (FS_ROWS, D_FOX), 1)
    own_head = jnp.right_shift(coli, 7) == jnp.bitwise_and(rowi, FOX_HEADS - 1)

    @pl.when(kt == 0)
    def _():
        q = q_ref[...]
        qrep = jnp.broadcast_to(q[:, None, :], (DEC_SEQ, FOX_HEADS, D_FOX)).reshape(FS_ROWS, D_FOX)
        qbd_scr[...] = jnp.where(own_head, qrep, 0.0).astype(BF16)
        m_scr[...] = jnp.full(m_scr.shape, NEG, F32)
        l_scr[...] = jnp.zeros(l_scr.shape, F32)
        acc_scr[...] = jnp.zeros(acc_scr.shape, F32)

    def step(k, v, ck, valid):
        n = k.shape[0]
        s = _dot_nt(qbd_scr[...], k.astype(BF16)) * FOX_SCALE
        s = s + (cq_ref[0] - jnp.tile(ck, (DEC_SEQ, 1)))
        if valid is not None:
            s = jnp.where(valid, s, NEG)
        m_old = m_scr[...]
        m_new = jnp.maximum(m_old, jnp.max(s, axis=-1, keepdims=True))
        alpha = jnp.exp(m_old - m_new)
        p = jnp.exp(s - m_new)
        l_scr[...] = alpha * l_scr[...] + jnp.sum(p, axis=-1, keepdims=True)
        acc_scr[...] = alpha * acc_scr[...] + _dot(p.astype(BF16), v.astype(BF16))
        m_scr[...] = m_new

    step(kc_ref[0], vc_ref[0], ckp_ref[0], None)

    @pl.when(kt == FS_NT - 1)
    def _():
        zpad = jnp.zeros((LANES - DEC_SEQ, D_FOX), F32)
        kn = jnp.concatenate([kn_ref[...], zpad], axis=0)
        vn = jnp.concatenate([vn_ref[...], zpad], axis=0)
        r2 = lax.broadcasted_iota(I32, (FS_ROWS, LANES), 0)
        c2 = lax.broadcasted_iota(I32, (FS_ROWS, LANES), 1)
        step(kn, vn, ckn_ref[0], c2 <= jnp.right_shift(r2, 3))
        acc = jnp.where(own_head, acc_scr[...] / l_scr[...], 0.0)
        o_ref[...] = jnp.sum(acc.reshape(DEC_SEQ, FOX_HEADS, D_FOX), axis=1)


def _fox_sample(q_all, cache_k, cache_v, k_s, v_s, cq, ck):
    q_blk0 = N_P // DEC_SEQ
    return pl.pallas_call(
        _fox_sample_kernel,
        grid=(DEC_BATCH, FS_NT),
        in_specs=[
            pl.BlockSpec((DEC_SEQ, D_FOX), lambda b, kt: (q_blk0 + b, 0)),
            pl.BlockSpec((1, FS_TK, D_FOX), lambda b, kt: (b, kt, 0)),
            pl.BlockSpec((1, FS_TK, D_FOX), lambda b, kt: (b, kt, 0)),
            pl.BlockSpec((DEC_SEQ, D_FOX), lambda b, kt: (b, 0)),
            pl.BlockSpec((DEC_SEQ, D_FOX), lambda b, kt: (b, 0)),
            pl.BlockSpec((1, FS_ROWS, 1), lambda b, kt: (b, 0, 0)),
            pl.BlockSpec((1, FOX_HEADS, FS_TK), lambda b, kt: (b, 0, kt)),
            pl.BlockSpec((1, FOX_HEADS, LANES), lambda b, kt: (b, 0, PAST_LEN // LANES)),
        ],
        out_specs=pl.BlockSpec((DEC_SEQ, D_FOX), lambda b, kt: (b, 0)),
        out_shape=jax.ShapeDtypeStruct((N_S, D_FOX), F32),
        scratch_shapes=[pltpu.VMEM((FS_ROWS, D_FOX), BF16), pltpu.VMEM((FS_ROWS, 1), F32),
                        pltpu.VMEM((FS_ROWS, 1), F32), pltpu.VMEM((FS_ROWS, D_FOX), F32)],
        compiler_params=_cparams(("arbitrary", "arbitrary")),
        name="fox_sample",
    )(q_all, cache_k, cache_v, k_s, v_s, cq, ck, ck)


OP_TN = 512
OP_NJ = D_MODEL // OP_TN


def _out_proj_kernel(orp_ref, ors_ref, ofp_ref, ofs_ref, x_ref, g1_ref, g2_ref, w_ref, gm_ref,
                     rwh_ref, rwl_ref, rb_ref, h_ref, xn_ref, lg_ref, mix_scr):
    i = pl.program_id(0)
    j = pl.program_id(1)

    @pl.when(jnp.logical_and(j == 0, i < P_TILES))
    def _():
        mix_scr[:, :D_RNN] = _rms(orp_ref[...], g1_ref[...]).astype(BF16)
        mix_scr[:, D_RNN:] = _rms(ofp_ref[...], g2_ref[...]).astype(BF16)

    @pl.when(jnp.logical_and(j == 0, i >= P_TILES))
    def _():
        mix_scr[:, :D_RNN] = _rms(ors_ref[...], g1_ref[...]).astype(BF16)
        mix_scr[:, D_RNN:] = _rms(ofs_ref[...], g2_ref[...]).astype(BF16)

    r = _dot(mix_scr[...], w_ref[...])
    for jj in range(OP_NJ):
        @pl.when(j == jj)
        def _(jj=jj):
            h_ref[:, jj * OP_TN:(jj + 1) * OP_TN] = x_ref[:, jj * OP_TN:(jj + 1) * OP_TN] + r

    @pl.when(j == OP_NJ - 1)
    def _():
        xn = _rms(h_ref[...], gm_ref[...])
        xn_ref[...] = xn
        xh = xn.astype(BF16)
        xl = (xn - xh.astype(F32)).astype(BF16)
        lg_ref[...] = (_dot(xh, rwh_ref[...]) + _dot(xl, rwh_ref[...]) + _dot(xh, rwl_ref[...])) + rb_ref[...]


def _out_proj(o_rnn_p, o_rnn_s, o_fox_p, o_fox_s, x_all, g1, g2, w_out, g_moe, rw_hi, rw_lo, rb):
    row = lambda i, j: (i, 0)
    prow = lambda i, j: (jnp.minimum(i, P_TILES - 1), 0)
    srow = lambda i, j: (jnp.maximum(i - P_TILES, 0), 0)
    const = lambda i, j: (0, 0)
    return pl.pallas_call(
        _out_proj_kernel,
        grid=(N_TILES, OP_NJ),
        in_specs=[
            pl.BlockSpec((TM, D_RNN), prow),
            pl.BlockSpec((TM, D_RNN), srow),
            pl.BlockSpec((TM, D_FOX), prow),
            pl.BlockSpec((TM, D_FOX), srow),
            pl.BlockSpec((TM, D_MODEL), row),
            pl.BlockSpec((1, D_RNN), const),
            pl.BlockSpec((1, D_FOX), const),
            pl.BlockSpec((D_MODEL, OP_TN), lambda i, j: (0, j)),
            pl.BlockSpec((1, D_MODEL), const),
            pl.BlockSpec((D_MODEL, LANES), const),
            pl.BlockSpec((D_MODEL, LANES), const),
            pl.BlockSpec((1, LANES), const),
        ],
        out_specs=[
            pl.BlockSpec((TM, D_MODEL), row),
            pl.BlockSpec((TM, D_MODEL), row),
            pl.BlockSpec((TM, LANES), row),
        ],
        out_shape=[
            jax.ShapeDtypeStruct((N_TOK, D_MODEL), F32),
            jax.ShapeDtypeStruct((N_TOK, D_MODEL), F32),
            jax.ShapeDtypeStruct((N_TOK, LANES), F32),
        ],
        scratch_shapes=[pltpu.VMEM((TM, D_MODEL), BF16)],
        compiler_params=_cparams(("arbitrary", "arbitrary")),
        name="out_proj",
    )(o_rnn_p, o_rnn_s, o_fox_p, o_fox_s, x_all, g1, g2, w_out, g_moe, rw_hi, rw_lo, rb)


def _route_kernel(lg_ref, ei_ref, gate_ref, cnt_ref, carry_scr):
    t = pl.program_id(0)

    @pl.when(t == 0)
    def _():
        carry_scr[...] = jnp.zeros(carry_scr.shape, F32)

    lane = lax.broadcasted_iota(I32, (TM, LANES), 1)
    lane_f = lane.astype(F32)
    work = lg_ref[...]
    tops, idxs, hots = [], [], []
    for _ in range(TOP_K):
        m = jnp.max(work, axis=-1, keepdims=True)
        idx_f = jnp.min(jnp.where(work == m, lane_f, float(LANES)), axis=-1, keepdims=True)
        hot = lane_f == idx_f
        work = jnp.where(hot, -jnp.inf, work)
        tops.append(m)
        idxs.append(idx_f.astype(I32))
        hots.append(hot)

    es = [jnp.exp(tv - tops[0]) for tv in tops]
    denom = es[0] + es[1] + es[2] + es[3]
    gate = jnp.zeros((TM, LANES), F32)
    for k in range(TOP_K):
        gate = jnp.where(lane == k, es[k] / denom, gate)
    gate_ref[...] = gate

    multi = jnp.zeros((TM, LANES), F32)
    for k in range(TOP_K):
        multi = jnp.where(hots[k], 1.0, multi)
    r_i = lax.broadcasted_iota(I32, (TM, TM), 0)
    c_i = lax.broadcasted_iota(I32, (TM, TM), 1)
    strict_lower = jnp.where(c_i < r_i, 1.0, 0.0).astype(BF16)
    before = _dot(strict_lower, multi.astype(BF16)) + carry_scr[...]
    ei = jnp.zeros((TM, LANES), I32)
    for k in range(TOP_K):
        rank = jnp.sum(jnp.where(hots[k], before, 0.0), axis=-1, keepdims=True).astype(I32)
        ei = jnp.where(lane == k, idxs[k], ei)
        ei = jnp.where(lane == TOP_K + k, rank, ei)
    ei_ref[...] = ei
    carry_scr[...] = carry_scr[...] + jnp.sum(multi, axis=0, keepdims=True)
    cnt_ref[...] = carry_scr[...].astype(I32)


def _route(logits):
    row = lambda t: (t, 0)
    return pl.pallas_call(
        _route_kernel,
        grid=(N_TILES,),
        in_specs=[pl.BlockSpec((TM, LANES), row)],
        out_specs=[pl.BlockSpec((TM, LANES), row), pl.BlockSpec((TM, LANES), row),
                   pl.BlockSpec((1, LANES), lambda t: (0, 0))],
        out_shape=[jax.ShapeDtypeStruct((N_TOK, LANES), I32), jax.ShapeDtypeStruct((N_TOK, LANES), F32),
                   jax.ShapeDtypeStruct((1, LANES), I32)],
        scratch_shapes=[pltpu.VMEM((1, LANES), F32)],
        compiler_params=_cparams(("arbitrary",)),
        name="route",
    )(logits)


DISP_CHUNK = 1024
DISP_STEPS = N_ROWS // DISP_CHUNK


def _dispatch_kernel(dest_ref, x_hbm, init_hbm, out_hbm, sem):
    del init_hbm
    base = pl.program_id(0) * DISP_CHUNK

    def row_copy(a):
        tok = lax.shift_right_logical(base + a, 2)
        return pltpu.make_async_copy(x_hbm.at[pl.ds(tok, 1)], out_hbm.at[pl.ds(dest_ref[a], 1)], sem)

    def start(a, c):
        row_copy(a).start()
        return c

    def wait(a, c):
        row_copy(a).wait()
        return c

    lax.fori_loop(0, DISP_CHUNK, start, 0)
    lax.fori_loop(0, DISP_CHUNK, wait, 0)


def _dispatch(dest_flat, xn, init):
    return pl.pallas_call(
        _dispatch_kernel,
        grid=(DISP_STEPS,),
        in_specs=[
            pl.BlockSpec((DISP_CHUNK,), lambda s: (s,), memory_space=pltpu.SMEM),
            pl.BlockSpec(memory_space=pl.ANY),
            pl.BlockSpec(memory_space=pl.ANY),
        ],
        out_specs=pl.BlockSpec(memory_space=pl.ANY),
        out_shape=jax.ShapeDtypeStruct((N_PAD, D_MODEL), F32),
        scratch_shapes=[pltpu.SemaphoreType.DMA(())],
        input_output_aliases={2: 0},
        compiler_params=_cparams(("arbitrary",)),
        name="dispatch",
    )(dest_flat, xn, init)


EX_TF = 512
EX_NF = D_FF // EX_TF


def _experts_kernel(be_ref, nu_ref, x_ref, wg_ref, wu_ref, bg_ref, bu_ref, wd_ref, bd_ref, y_ref, xb_scr):
    del be_ref
    g = pl.program_id(0)
    f = pl.program_id(1)

    @pl.when(g < nu_ref[0])
    def _():
        @pl.when(f == 0)
        def _():
            xb_scr[...] = x_ref[...].astype(BF16)
            y_ref[...] = jnp.broadcast_to(bd_ref[0], y_ref.shape)

        xb = xb_scr[...]
        gg = jnp.minimum(_dot(xb, wg_ref[0]) + bg_ref[0], SWIGLU_LIMIT)
        uu = jnp.clip(_dot(xb, wu_ref[0]) + bu_ref[0], -SWIGLU_LIMIT, SWIGLU_LIMIT)
        hdn = (uu + 1.0) * (gg * _sigmoid(SWIGLU_ALPHA * gg))
        y_ref[...] += _dot(hdn.astype(BF16), wd_ref[0])

    @pl.when(jnp.logical_and(g >= nu_ref[0], f == 0))
    def _():
        y_ref[...] = jnp.zeros(y_ref.shape, F32)


def _experts(block_e, n_used, x_sorted, w_gu, b_gu, w_down, b_down):
    gsel = lambda g, nu: jnp.minimum(g, nu[0] - 1)
    fsel = lambda g, f, nu: jnp.where(g < nu[0], f, EX_NF - 1)
    return pl.pallas_call(
        _experts_kernel,
        grid_spec=pltpu.PrefetchScalarGridSpec(
            num_scalar_prefetch=2,
            grid=(N_BLOCKS, EX_NF),
            in_specs=[
                pl.BlockSpec((EXP_BLK, D_MODEL), lambda g, f, be, nu: (gsel(g, nu), 0)),
                pl.BlockSpec((1, D_MODEL, EX_TF), lambda g, f, be, nu: (be[gsel(g, nu)], 0, fsel(g, f, nu))),
                pl.BlockSpec((1, D_MODEL, EX_TF), lambda g, f, be, nu: (be[gsel(g, nu)], 0, EX_NF + fsel(g, f, nu))),
                pl.BlockSpec((1, 1, EX_TF), lambda g, f, be, nu: (be[gsel(g, nu)], 0, fsel(g, f, nu))),
                pl.BlockSpec((1, 1, EX_TF), lambda g, f, be, nu: (be[gsel(g, nu)], 0, EX_NF + fsel(g, f, nu))),
                pl.BlockSpec((1, EX_TF, D_MODEL), lambda g, f, be, nu: (be[gsel(g, nu)], fsel(g, f, nu), 0)),
                pl.BlockSpec((1, 1, D_MODEL), lambda g, f, be, nu: (be[gsel(g, nu)], 0, 0)),
            ],
            out_specs=pl.BlockSpec((EXP_BLK, D_MODEL), lambda g, f, be, nu: (g, 0)),
            scratch_shapes=[pltpu.VMEM((EXP_BLK, D_MODEL), BF16)],
        ),
        out_shape=jax.ShapeDtypeStruct((N_PAD, D_MODEL), F32),
        compiler_params=_cparams(("arbitrary", "arbitrary")),
        name="experts",
    )(block_e, n_used, x_sorted, w_gu, w_gu, b_gu, b_gu, w_down, b_down)


def _combine_kernel(dest_ref, y_hbm, h_ref, gate_ref, o_ref, buf, sem):
    def row_copy(a):
        tok = lax.shift_right_logical(a, 2)
        k = jnp.bitwise_and(a, TOP_K - 1)
        return pltpu.make_async_copy(y_hbm.at[pl.ds(dest_ref[a], 1)], buf.at[k, pl.ds(tok, 1)], sem)

    def start(a, c):
        row_copy(a).start()
        return c

    def wait(a, c):
        row_copy(a).wait()
        return c

    lax.fori_loop(0, TM * TOP_K, start, 0)
    lax.fori_loop(0, TM * TOP_K, wait, 0)
    gate = gate_ref[...]
    acc = h_ref[...]
    for k in range(TOP_K):
        acc = acc + gate[:, k:k + 1] * buf[k]
    o_ref[...] = acc


def _combine(dest_flat, y_sorted, h, gate):
    row = lambda t: (t, 0)
    return pl.pallas_call(
        _combine_kernel,
        grid=(N_TILES,),
        in_specs=[
            pl.BlockSpec((TM * TOP_K,), lambda t: (t,), memory_space=pltpu.SMEM),
            pl.BlockSpec(memory_space=pl.ANY),
            pl.BlockSpec((TM, D_MODEL), row),
            pl.BlockSpec((TM, LANES), row),
        ],
        out_specs=pl.BlockSpec((TM, D_MODEL), row),
        out_shape=jax.ShapeDtypeStruct((N_TOK, D_MODEL), F32),
        scratch_shapes=[pltpu.VMEM((TOP_K, TM, D_MODEL), F32), pltpu.SemaphoreType.DMA(())],
        compiler_params=_cparams(("arbitrary",)),
        name="combine",
    )(dest_flat, y_sorted, h, gate)


PL_TN = 512
PL_NJ = D_MODEL // PL_TN


def _ple_final_kernel(h_ref, p_ref, wp_ref, gp_ref, gg_ref, wg_ref, gf_ref, yp_ref, ys_ref,
                      hn_scr, e_scr, h3_scr):
    i = pl.program_id(0)
    j = pl.program_id(1)

    @pl.when(j == 0)
    def _():
        hn_scr[...] = _rms(h_ref[...], gg_ref[...]).astype(BF16)
        e_scr[...] = _rms(_dot(p_ref[...].astype(BF16), wp_ref[...]), gp_ref[...])

    gate = _sigmoid(_dot(hn_scr[...], wg_ref[...]))
    for jj in range(PL_NJ):
        @pl.when(j == jj)
        def _(jj=jj):
            sl = slice(jj * PL_TN, (jj + 1) * PL_TN)
            h3_scr[:, sl] = h_ref[:, sl] + gate * e_scr[:, sl]

    @pl.when(jnp.logical_and(j == PL_NJ - 1, i < P_TILES))
    def _():
        yp_ref[...] = _rms(h3_scr[...], gf_ref[...])

    @pl.when(jnp.logical_and(j == PL_NJ - 1, i >= P_TILES))
    def _():
        ys_ref[...] = _rms(h3_scr[...], gf_ref[...])


def _ple_final(h2, p_all, w_proj, g_proj, g_gate, w_gate, g_final):
    row = lambda i, j: (i, 0)
    prow = lambda i, j: (jnp.minimum(i, P_TILES - 1), 0)
    srow = lambda i, j: (jnp.maximum(i - P_TILES, 0), 0)
    const = lambda i, j: (0, 0)
    return pl.pallas_call(
        _ple_final_kernel,
        grid=(N_TILES, PL_NJ),
        in_specs=[
            pl.BlockSpec((TM, D_MODEL), row),
            pl.BlockSpec((TM, D_PLE), row),
            pl.BlockSpec((D_PLE, D_MODEL), const),
            pl.BlockSpec((1, D_MODEL), const),
            pl.BlockSpec((1, D_MODEL), const),
            pl.BlockSpec((D_MODEL, PL_TN), lambda i, j: (0, j)),
            pl.BlockSpec((1, D_MODEL), const),
        ],
        out_specs=[pl.BlockSpec((TM, D_MODEL), prow), pl.BlockSpec((TM, D_MODEL), srow)],
        out_shape=[jax.ShapeDtypeStruct((N_P, D_MODEL), F32), jax.ShapeDtypeStruct((N_S, D_MODEL), F32)],
        scratch_shapes=[pltpu.VMEM((TM, D_MODEL), BF16), pltpu.VMEM((TM, D_MODEL), F32),
                        pltpu.VMEM((TM, D_MODEL), F32)],
        compiler_params=_cparams(("arbitrary", "arbitrary")),
        name="ple_final",
    )(h2, p_all, w_proj, g_proj, g_gate, w_gate, g_final)


def kernel(x_prompt, x_sample, cache_k, cache_v, cache_logf, state_conv, state_rglru, p_prompt, p_sample,
           g_mix, w_in, b_fgate, conv_w, conv_b, rg_w_a, rg_b_a, rg_w_i, rg_b_i, rg_lambda,
           g_out_rnn, g_out_fox, w_out, g_moe, router_w, router_b, w_gu, b_gu, w_down, b_down,
           w_ple_proj, g_ple_proj, g_ple_gate, w_ple_gate, g_final):
    assert g_mix.shape[0] == 1, "one layer"
    row = lambda v: v.reshape(1, -1)
    lane_pad = lambda a: jnp.pad(a, ((0, 0), (0, LANES - a.shape[1])))

    x_all = jnp.concatenate([x_prompt.reshape(N_P, D_MODEL), x_sample.reshape(N_S, D_MODEL)], axis=0)
    p_all = jnp.concatenate([p_prompt[0].reshape(N_P, D_PLE), p_sample[0].reshape(N_S, D_PLE)], axis=0)

    w_main = w_in[0][:, :D_MAIN].astype(BF16)
    w_f = lane_pad(w_in[0][:, D_MAIN:]).astype(BF16)
    b_f = lane_pad(row(b_fgate[0]))
    xy, q_all, k_p, k_s, v_p, v_s, logf = _in_proj(x_all, row(g_mix[0]), w_main, w_f, b_f)
    logf_p = logf[:N_P, :FOX_HEADS].reshape(BATCH, SEQ, FOX_HEADS)
    logf_s = logf[N_P:, :FOX_HEADS].reshape(DEC_BATCH, DEC_SEQ, FOX_HEADS)

    rg = (conv_w[0], row(conv_b[0]), rg_w_a[0].astype(BF16), row(rg_b_a[0]), rg_w_i[0].astype(BF16),
          row(rg_b_i[0]), row(rg_lambda[0]))
    o_rnn_p, conv_p, hl_p = _rglru(xy, 0, BATCH, SEQ // TM, TM,
                                   jnp.zeros((BATCH, CONV_W - 1, D_RNN), F32), jnp.zeros((BATCH, 1, D_RNN), F32),
                                   *rg, name="rglru_prompt")
    o_rnn_s, conv_s, hl_s = _rglru(xy, N_P, DEC_BATCH, 1, DEC_SEQ,
                                   state_conv[0], state_rglru[0].reshape(DEC_BATCH, 1, D_RNN),
                                   *rg, name="rglru_sample")

    c_p = _cumsum_rows(logf_p.transpose(1, 0, 2).reshape(SEQ, BATCH * FOX_HEADS), "cumsum_prompt")
    c_pT = c_p.T
    o_fox_p = _fox_prompt(q_all, k_p, v_p, c_pT[:, :, None], c_pT[:, None, :])

    lf_all = jnp.concatenate([cache_logf[0], logf_s], axis=1)
    c_s = _cumsum_rows(lf_all.transpose(1, 0, 2).reshape(PAST_LEN + DEC_SEQ, DEC_BATCH * FOX_HEADS),
                       "cumsum_sample")
    ck = jnp.pad(c_s.T.reshape(DEC_BATCH, FOX_HEADS, PAST_LEN + DEC_SEQ), ((0, 0), (0, 0), (0, LANES - DEC_SEQ)))
    cq = (c_s[PAST_LEN:].reshape(DEC_SEQ, DEC_BATCH, FOX_HEADS).transpose(1, 0, 2)
          .reshape(DEC_BATCH, FS_ROWS, 1))
    o_fox_s = _fox_sample(q_all, cache_k[0].reshape(DEC_BATCH, PAST_LEN, D_FOX),
                          cache_v[0].reshape(DEC_BATCH, PAST_LEN, D_FOX), k_s, v_s, cq, ck)

    rw = lane_pad(router_w[0])
    rw_hi = rw.astype(BF16)
    rw_lo = (rw - rw_hi.astype(F32)).astype(BF16)
    rb = jnp.concatenate([row(router_b[0]), jnp.full((1, LANES - N_EXPERTS), NEG, F32)], axis=1)
    h1, xn_moe, logits = _out_proj(o_rnn_p, o_rnn_s, o_fox_p, o_fox_s, x_all, row(g_out_rnn[0]), row(g_out_fox[0]),
                                   w_out[0].astype(BF16), row(g_moe[0]), rw_hi, rw_lo, rb)

    ei, gate, cnt = _route(logits)
    counts = cnt[0, :N_EXPERTS]
    padded = (counts + EXP_BLK - 1) // EXP_BLK * EXP_BLK
    pad_end = jnp.cumsum(padded)
    pad_start = pad_end - padded
    dest = (pad_start[ei[:, :TOP_K]] + ei[:, TOP_K:2 * TOP_K]).reshape(N_ROWS)
    n_used = (pad_end[-1] // EXP_BLK).reshape(1).astype(I32)
    block_e = jnp.minimum(jnp.searchsorted(pad_end, jnp.arange(N_BLOCKS, dtype=I32) * EXP_BLK, side='right'),
                          N_EXPERTS - 1).astype(I32)
    x_sorted = _dispatch(dest, xn_moe, jnp.zeros((N_PAD, D_MODEL), F32))
    y_sorted = _experts(block_e, n_used, x_sorted, w_gu[0].astype(BF16), b_gu[0][:, None, :],
                        w_down[0].astype(BF16), b_down[0][:, None, :])
    h2 = _combine(dest, y_sorted, h1, gate)

    y_p, y_s = _ple_final(h2, p_all, w_ple_proj[0].astype(BF16), row(g_ple_proj[0]), row(g_ple_gate[0]),
                          w_ple_gate[0].astype(BF16), row(g_final))

    shp = (1, BATCH, SEQ, FOX_HEADS, FOX_HEAD_DIM)
    shs = (1, DEC_BATCH, DEC_SEQ, FOX_HEADS, FOX_HEAD_DIM)
    return (y_p.reshape(BATCH, SEQ, D_MODEL), y_s.reshape(DEC_BATCH, DEC_SEQ, D_MODEL),
            k_p.reshape(shp), v_p.reshape(shp), logf_p[None],
            conv_p[None], hl_p.reshape(1, BATCH, D_RNN),
            k_s.reshape(shs), v_s.reshape(shs), logf_s[None],
            conv_s[None], hl_s.reshape(1, DEC_BATCH, D_RNN))
```

```python
import functools

import jax
import jax.numpy as jnp
from jax import lax
from jax.experimental import pallas as pl
from jax.experimental.pallas import tpu as pltpu

F32 = jnp.float32
BF16 = jnp.bfloat16
I32 = jnp.int32

D_MODEL = 2048
BATCH = 4
SEQ = 2048
DEC_BATCH = 16
DEC_SEQ = 16
PAST_LEN = 4096
D_RNN = 1024
RG_BLOCKS = 8
RG_BLOCK_W = 128
CONV_W = 4
RG_C = 8.0
FOX_HEADS = 8
FOX_HEAD_DIM = 128
D_FOX = 1024
D_MAIN = 2 * D_RNN + 3 * D_FOX
N_EXPERTS = 32
TOP_K = 4
D_FF = 2048
SWIGLU_LIMIT = 7.0
SWIGLU_ALPHA = 1.702
D_PLE = 256
EPS = 1e-6

LANES = 128
N_P = BATCH * SEQ
N_S = DEC_BATCH * DEC_SEQ
N_TOK = N_P + N_S
TM = 256
N_TILES = N_TOK // TM
P_TILES = N_P // TM
N_ROWS = N_TOK * TOP_K
EXP_BLK = 256
N_BLOCKS = (N_ROWS + N_EXPERTS * (EXP_BLK - 1) + EXP_BLK - 1) // EXP_BLK
N_PAD = N_BLOCKS * EXP_BLK
NEG = -1e30
VMEM_LIMIT = 48 * 1024 * 1024


def _cparams(sem):
    return pltpu.CompilerParams(dimension_semantics=sem, vmem_limit_bytes=VMEM_LIMIT)


def _rms(x, g):
    return x * lax.rsqrt(jnp.mean(x * x, axis=-1, keepdims=True) + EPS) * g


def _log_sigmoid(z):
    return jnp.minimum(z, 0.0) - jnp.log1p(jnp.exp(-jnp.abs(z)))


def _sigmoid(z):
    return 1.0 / (1.0 + jnp.exp(-z))


def _expm1(x):
    u = jnp.exp(x)
    degenerate = jnp.logical_or(u == 1.0, u == 0.0)
    val = (u - 1.0) * x / jnp.log(jnp.where(degenerate, 0.5, u))
    return jnp.where(u == 1.0, x, jnp.where(u == 0.0, -1.0, val))


def _gelu_tanh(y):
    return 0.5 * y * (1.0 + jnp.tanh(0.7978845608028654 * (y + 0.044715 * (y * y * y))))


def _dot(a, b):
    return jnp.dot(a, b, preferred_element_type=F32)


def _dot_nt(a, b):
    return lax.dot_general(a, b, (((1,), (1,)), ((), ())), preferred_element_type=F32)


IN_TN = 512
IN_NJ = D_MAIN // IN_TN


def _in_proj_kernel(x_ref, g_ref, w_ref, wf_ref, bf_ref,
                    xy_ref, q_ref, kp_ref, ks_ref, vp_ref, vs_ref, logf_ref, xn_scr):
    i = pl.program_id(0)
    j = pl.program_id(1)

    @pl.when(j == 0)
    def _():
        xb = _rms(x_ref[...], g_ref[...]).astype(BF16)
        xn_scr[...] = xb
        logf_ref[...] = _log_sigmoid(_dot(xb, wf_ref[...]) + bf_ref[...])

    r = _dot(xn_scr[...], w_ref[...])
    is_prompt = i < P_TILES
    for jj in range(IN_NJ):
        if jj < 4:
            @pl.when(j == jj)
            def _(jj=jj):
                xy_ref[:, jj * IN_TN:(jj + 1) * IN_TN] = r
        elif jj < 6:
            @pl.when(j == jj)
            def _(jj=jj):
                q_ref[:, (jj - 4) * IN_TN:(jj - 3) * IN_TN] = r
        else:
            p_ref, s_ref, j0 = (kp_ref, ks_ref, 6) if jj < 8 else (vp_ref, vs_ref, 8)

            @pl.when(jnp.logical_and(j == jj, is_prompt))
            def _(jj=jj, p_ref=p_ref, j0=j0):
                p_ref[:, (jj - j0) * IN_TN:(jj - j0 + 1) * IN_TN] = r

            @pl.when(jnp.logical_and(j == jj, jnp.logical_not(is_prompt)))
            def _(jj=jj, s_ref=s_ref, j0=j0):
                s_ref[:, (jj - j0) * IN_TN:(jj - j0 + 1) * IN_TN] = r


def _in_proj(x_all, g_mix, w_main, w_f, b_f):
    row = lambda i, j: (i, 0)
    prow = lambda i, j: (jnp.minimum(i, P_TILES - 1), 0)
    srow = lambda i, j: (jnp.maximum(i - P_TILES, 0), 0)
    const = lambda i, j: (0, 0)
    return pl.pallas_call(
        _in_proj_kernel,
        grid=(N_TILES, IN_NJ),
        in_specs=[
            pl.BlockSpec((TM, D_MODEL), row),
            pl.BlockSpec((1, D_MODEL), const),
            pl.BlockSpec((D_MODEL, IN_TN), lambda i, j: (0, j)),
            pl.BlockSpec((D_MODEL, LANES), const),
            pl.BlockSpec((1, LANES), const),
        ],
        out_specs=[
            pl.BlockSpec((TM, 2 * D_RNN), row),
            pl.BlockSpec((TM, D_FOX), row),
            pl.BlockSpec((TM, D_FOX), prow),
            pl.BlockSpec((TM, D_FOX), srow),
            pl.BlockSpec((TM, D_FOX), prow),
            pl.BlockSpec((TM, D_FOX), srow),
            pl.BlockSpec((TM, LANES), row),
        ],
        out_shape=[
            jax.ShapeDtypeStruct((N_TOK, 2 * D_RNN), F32),
            jax.ShapeDtypeStruct((N_TOK, D_FOX), F32),
            jax.ShapeDtypeStruct((N_P, D_FOX), F32),
            jax.ShapeDtypeStruct((N_S, D_FOX), F32),
            jax.ShapeDtypeStruct((N_P, D_FOX), F32),
            jax.ShapeDtypeStruct((N_S, D_FOX), F32),
            jax.ShapeDtypeStruct((N_TOK, LANES), F32),
        ],
        scratch_shapes=[pltpu.VMEM((TM, D_MODEL), BF16)],
        compiler_params=_cparams(("arbitrary", "arbitrary")),
        name="in_proj",
    )(x_all, g_mix, w_main, w_f, b_f)


def _prefix_steps(n):
    s = 1
    while s < n:
        yield s
        s *= 2


def _rglru_kernel(x_ref, y_ref, past_ref, h0_ref, cw_ref, cb_ref, wa_ref, ba_ref, wi_ref, bi_ref, lam_ref,
                  o_ref, conv_ref, hlast_ref, xp_scr, h_scr, *, tt, nt):
    t = pl.program_id(1)
    pad = 8

    @pl.when(t == 0)
    def _():
        xp_scr[pad - (CONV_W - 1):pad, :] = past_ref[0]
        h_scr[...] = h0_ref[0]

    xp_scr[pad:pad + tt, :] = x_ref[...]
    xc = cb_ref[...] + xp_scr[pad - 3:pad - 3 + tt, :] * cw_ref[0:1, :]
    for jw in range(1, CONV_W):
        xc = xc + xp_scr[pad - 3 + jw:pad - 3 + jw + tt, :] * cw_ref[jw:jw + 1, :]

    r_parts, i_parts = [], []
    for n in range(RG_BLOCKS):
        xb = xc[:, n * RG_BLOCK_W:(n + 1) * RG_BLOCK_W].astype(BF16)
        r_parts.append(_dot(xb, wa_ref[n]))
        i_parts.append(_dot(xb, wi_ref[n]))
    r = _sigmoid(jnp.concatenate(r_parts, axis=-1) + ba_ref[...])
    ig = _sigmoid(jnp.concatenate(i_parts, axis=-1) + bi_ref[...])
    log_a = RG_C * r * _log_sigmoid(lam_ref[...])
    a = jnp.exp(log_a)
    b = jnp.sqrt(-_expm1(2.0 * log_a)) * (ig * xc)

    rowi = lax.broadcasted_iota(I32, (tt, D_RNN), 0)
    for s in _prefix_steps(tt):
        keep = rowi >= s
        a_sh = jnp.where(keep, pltpu.roll(a, s, 0), 1.0)
        b_sh = jnp.where(keep, pltpu.roll(b, s, 0), 0.0)
        b = a * b_sh + b
        a = a * a_sh
    h = a * h_scr[...] + b
    h_scr[...] = h[tt - 1:tt, :]
    o_ref[...] = h * _gelu_tanh(y_ref[...])

    tail = xp_scr[pad + tt - (CONV_W - 1):pad + tt, :]
    xp_scr[pad - (CONV_W - 1):pad, :] = tail

    @pl.when(t == nt - 1)
    def _():
        conv_ref[0] = tail
        hlast_ref[0] = h[tt - 1:tt, :]


def _rglru(xy, row0, nb, nt, tt, conv_past, h0, cw, cb, wa, ba, wi, bi, lam, name):
    blk0 = row0 // tt
    vec = lambda b, t: (0, 0)
    w3 = lambda b, t: (0, 0, 0)
    kern = functools.partial(_rglru_kernel, tt=tt, nt=nt)
    return pl.pallas_call(
        kern,
        grid=(nb, nt),
        in_specs=[
            pl.BlockSpec((tt, D_RNN), lambda b, t: (blk0 + b * nt + t, 0)),
            pl.BlockSpec((tt, D_RNN), lambda b, t: (blk0 + b * nt + t, 1)),
            pl.BlockSpec((1, CONV_W - 1, D_RNN), lambda b, t: (b, 0, 0)),
            pl.BlockSpec((1, 1, D_RNN), lambda b, t: (b, 0, 0)),
            pl.BlockSpec((CONV_W, D_RNN), vec),
            pl.BlockSpec((1, D_RNN), vec),
            pl.BlockSpec((RG_BLOCKS, RG_BLOCK_W, RG_BLOCK_W), w3),
            pl.BlockSpec((1, D_RNN), vec),
            pl.BlockSpec((RG_BLOCKS, RG_BLOCK_W, RG_BLOCK_W), w3),
            pl.BlockSpec((1, D_RNN), vec),
            pl.BlockSpec((1, D_RNN), vec),
        ],
        out_specs=[
            pl.BlockSpec((tt, D_RNN), lambda b, t: (b * nt + t, 0)),
            pl.BlockSpec((1, CONV_W - 1, D_RNN), lambda b, t: (b, 0, 0)),
            pl.BlockSpec((1, 1, D_RNN), lambda b, t: (b, 0, 0)),
        ],
        out_shape=[
            jax.ShapeDtypeStruct((nb * nt * tt, D_RNN), F32),
            jax.ShapeDtypeStruct((nb, CONV_W - 1, D_RNN), F32),
            jax.ShapeDtypeStruct((nb, 1, D_RNN), F32),
        ],
        scratch_shapes=[pltpu.VMEM((tt + 8, D_RNN), F32), pltpu.VMEM((1, D_RNN), F32)],
        compiler_params=_cparams(("arbitrary", "arbitrary")),
        name=name,
    )(xy, xy, conv_past, h0, cw, cb, wa, ba, wi, bi, lam)


def _cumsum_kernel(x_ref, o_ref, *, n):
    x = x_ref[...]
    rowi = lax.broadcasted_iota(I32, x.shape, 0)
    for s in _prefix_steps(n):
        x = x + jnp.where(rowi >= s, pltpu.roll(x, s, 0), 0.0)
    o_ref[...] = x


def _cumsum_rows(x, name):
    return pl.pallas_call(
        functools.partial(_cumsum_kernel, n=x.shape[0]),
        out_shape=jax.ShapeDtypeStruct(x.shape, F32),
        compiler_params=pltpu.CompilerParams(vmem_limit_bytes=VMEM_LIMIT),
        name=name,
    )(x)


FP_T = 512
FP_NT = SEQ // FP_T
FOX_SCALE = FOX_HEAD_DIM ** -0.5


def _fox_prompt_kernel(q_ref, k_ref, v_ref, cq_ref, ck_ref, o_ref, m_scr, l_scr, acc_scr):
    qi = pl.program_id(1)
    ki = pl.program_id(2)

    @pl.when(ki == 0)
    def _():
        m_scr[...] = jnp.full(m_scr.shape, NEG, F32)
        l_scr[...] = jnp.zeros(l_scr.shape, F32)
        acc_scr[...] = jnp.zeros(acc_scr.shape, F32)

    @pl.when(ki <= qi)
    def _():
        s = _dot_nt(q_ref[...].astype(BF16), k_ref[...].astype(BF16)) * FOX_SCALE
        s = s + (cq_ref[0] - ck_ref[0])
        rowi = lax.broadcasted_iota(I32, (FP_T, FP_T), 0)
        coli = lax.broadcasted_iota(I32, (FP_T, FP_T), 1)
        s = jnp.where(jnp.logical_or(ki < qi, coli <= rowi), s, NEG)
        m_old = m_scr[...]
        m_new = jnp.maximum(m_old, jnp.max(s, axis=-1, keepdims=True))
        alpha = jnp.exp(m_old - m_new)
        p = jnp.exp(s - m_new)
        l_scr[...] = alpha * l_scr[...] + jnp.sum(p, axis=-1, keepdims=True)
        acc_scr[...] = alpha * acc_scr[...] + _dot(p.astype(BF16), v_ref[...].astype(BF16))
        m_scr[...] = m_new

    @pl.when(ki == FP_NT - 1)
    def _():
        o_ref[...] = acc_scr[...] / l_scr[...]


def _fox_prompt(q_all, k_p, v_p, c_col, c_row):
    qmap = lambda bh, qi, ki: ((bh // FOX_HEADS) * FP_NT + qi, bh % FOX_HEADS)
    kmap = lambda bh, qi, ki: ((bh // FOX_HEADS) * FP_NT + jnp.minimum(ki, qi), bh % FOX_HEADS)
    return pl.pallas_call(
        _fox_prompt_kernel,
        grid=(BATCH * FOX_HEADS, FP_NT, FP_NT),
        in_specs=[
            pl.BlockSpec((FP_T, FOX_HEAD_DIM), qmap),
            pl.BlockSpec((FP_T, FOX_HEAD_DIM), kmap),
            pl.BlockSpec((FP_T, FOX_HEAD_DIM), kmap),
            pl.BlockSpec((1, FP_T, 1), lambda bh, qi, ki: (bh, qi, 0)),
            pl.BlockSpec((1, 1, FP_T), lambda bh, qi, ki: (bh, 0, jnp.minimum(ki, qi))),
        ],
        out_specs=pl.BlockSpec((FP_T, FOX_HEAD_DIM), qmap),
        out_shape=jax.ShapeDtypeStruct((N_P, D_FOX), F32),
        scratch_shapes=[pltpu.VMEM((FP_T, 1), F32), pltpu.VMEM((FP_T, 1), F32),
                        pltpu.VMEM((FP_T, FOX_HEAD_DIM), F32)],
        compiler_params=_cparams(("arbitrary", "arbitrary", "arbitrary")),
        name="fox_prompt",
    )(q_all, k_p, v_p, c_col, c_row)


FS_TK = 1024
FS_NT = PAST_LEN // FS_TK
FS_ROWS = DEC_SEQ * FOX_HEADS


FS_COLS = FS_TK * FOX_HEADS


def _fox_sample_kernel(q_ref, kc_ref, vc_ref, kn_ref, vn_ref, cq_ref, ckp_ref, ckn_ref, o_ref,
                       m_scr, l_scr, acc_scr):
    kt = pl.program_id(1)

    @pl.when(kt == 0)
    def _():
        m_scr[...] = jnp.full(m_scr.shape, NEG, F32)
        l_scr[...] = jnp.zeros(l_scr.shape, F32)
        acc_scr[...] = jnp.zeros(acc_scr.shape, F32)

    qb = q_ref[0].astype(BF16)

    def step(k2, v2, ck, causal):
        n = k2.shape[0]
        s = _dot_nt(qb, k2.astype(BF16)) * FOX_SCALE + (cq_ref[0] - ck)
        rowi = lax.broadcasted_iota(I32, (FS_ROWS, n), 0)
        coli = lax.broadcasted_iota(I32, (FS_ROWS, n), 1)
        keep = jnp.bitwise_and(coli, FOX_HEADS - 1) == jnp.bitwise_and(rowi, FOX_HEADS - 1)
        s = jnp.where(keep, s, NEG)
        if causal:
            s = jnp.where(jnp.right_shift(coli, 3) <= jnp.right_shift(rowi, 3), s, NEG)
        m_old = m_scr[...]
        m_new = jnp.maximum(m_old, jnp.max(s, axis=-1, keepdims=True))
        alpha = jnp.exp(m_old - m_new)
        p = jnp.exp(s - m_new)
        l_scr[...] = alpha * l_scr[...] + jnp.sum(p, axis=-1, keepdims=True)
        acc_scr[...] = alpha * acc_scr[...] + _dot(p.astype(BF16), v2.astype(BF16))
        m_scr[...] = m_new

    step(kc_ref[0, 0].reshape(FS_COLS, FOX_HEAD_DIM), vc_ref[0, 0].reshape(FS_COLS, FOX_HEAD_DIM),
         ckp_ref[0], False)

    @pl.when(kt == FS_NT - 1)
    def _():
        step(kn_ref[0], vn_ref[0], ckn_ref[0], True)
        o_ref[0] = acc_scr[...] / l_scr[...]


def _fox_sample(q_s, cache_k, cache_v, k_n, v_n, cq, ck):
    per_req = lambda b, kt: (b, 0, 0)
    cache = lambda b, kt: (0, b, kt, 0, 0)
    return pl.pallas_call(
        _fox_sample_kernel,
        grid=(DEC_BATCH, FS_NT),
        in_specs=[
            pl.BlockSpec((1, FS_ROWS, FOX_HEAD_DIM), per_req),
            pl.BlockSpec((1, 1, FS_TK, FOX_HEADS, FOX_HEAD_DIM), cache),
            pl.BlockSpec((1, 1, FS_TK, FOX_HEADS, FOX_HEAD_DIM), cache),
            pl.BlockSpec((1, FS_ROWS, FOX_HEAD_DIM), per_req),
            pl.BlockSpec((1, FS_ROWS, FOX_HEAD_DIM), per_req),
            pl.BlockSpec((1, FS_ROWS, 1), per_req),
            pl.BlockSpec((1, 1, FS_COLS), lambda b, kt: (b, 0, kt)),
            pl.BlockSpec((1, 1, FS_ROWS), lambda b, kt: (b, 0, PAST_LEN * FOX_HEADS // FS_ROWS)),
        ],
        out_specs=pl.BlockSpec((1, FS_ROWS, FOX_HEAD_DIM), per_req),
        out_shape=jax.ShapeDtypeStruct((DEC_BATCH, FS_ROWS, FOX_HEAD_DIM), F32),
        scratch_shapes=[pltpu.VMEM((FS_ROWS, 1), F32), pltpu.VMEM((FS_ROWS, 1), F32),
                        pltpu.VMEM((FS_ROWS, FOX_HEAD_DIM), F32)],
        compiler_params=_cparams(("arbitrary", "arbitrary")),
        name="fox_sample",
    )(q_s, cache_k, cache_v, k_n, v_n, cq, ck, ck)


OP_TN = 512
OP_NJ = D_MODEL // OP_TN


def _out_proj_kernel(orp_ref, ors_ref, ofp_ref, ofs_ref, x_ref, g1_ref, g2_ref, w_ref, gm_ref,
                     rwh_ref, rwl_ref, rb_ref, h_ref, xn_ref, lg_ref, mix_scr):
    i = pl.program_id(0)
    j = pl.program_id(1)

    @pl.when(jnp.logical_and(j == 0, i < P_TILES))
    def _():
        mix_scr[:, :D_RNN] = _rms(orp_ref[...], g1_ref[...]).astype(BF16)
        mix_scr[:, D_RNN:] = _rms(ofp_ref[...], g2_ref[...]).astype(BF16)

    @pl.when(jnp.logical_and(j == 0, i >= P_TILES))
    def _():
        mix_scr[:, :D_RNN] = _rms(ors_ref[...], g1_ref[...]).astype(BF16)
        mix_scr[:, D_RNN:] = _rms(ofs_ref[...], g2_ref[...]).astype(BF16)

    r = _dot(mix_scr[...], w_ref[...])
    for jj in range(OP_NJ):
        @pl.when(j == jj)
        def _(jj=jj):
            h_ref[:, jj * OP_TN:(jj + 1) * OP_TN] = x_ref[:, jj * OP_TN:(jj + 1) * OP_TN] + r

    @pl.when(j == OP_NJ - 1)
    def _():
        xn = _rms(h_ref[...], gm_ref[...])
        xn_ref[...] = xn
        xh = xn.astype(BF16)
        xl = (xn - xh.astype(F32)).astype(BF16)
        lg_ref[...] = (_dot(xh, rwh_ref[...]) + _dot(xl, rwh_ref[...]) + _dot(xh, rwl_ref[...])) + rb_ref[...]


def _out_proj(o_rnn_p, o_rnn_s, o_fox_p, o_fox_s, x_all, g1, g2, w_out, g_moe, rw_hi, rw_lo, rb):
    row = lambda i, j: (i, 0)
    prow = lambda i, j: (jnp.minimum(i, P_TILES - 1), 0)
    srow = lambda i, j: (jnp.maximum(i - P_TILES, 0), 0)
    const = lambda i, j: (0, 0)
    return pl.pallas_call(
        _out_proj_kernel,
        grid=(N_TILES, OP_NJ),
        in_specs=[
            pl.BlockSpec((TM, D_RNN), prow),
            pl.BlockSpec((TM, D_RNN), srow),
            pl.BlockSpec((TM, D_FOX), prow),
            pl.BlockSpec((TM, D_FOX), srow),
            pl.BlockSpec((TM, D_MODEL), row),
            pl.BlockSpec((1, D_RNN), const),
            pl.BlockSpec((1, D_FOX), const),
            pl.BlockSpec((D_MODEL, OP_TN), lambda i, j: (0, j)),
            pl.BlockSpec((1, D_MODEL), const),
            pl.BlockSpec((D_MODEL, LANES), const),
            pl.BlockSpec((D_MODEL, LANES), const),
            pl.BlockSpec((1, LANES), const),
        ],
        out_specs=[
            pl.BlockSpec((TM, D_MODEL), row),
            pl.BlockSpec((TM, D_MODEL), row),
            pl.BlockSpec((TM, LANES), row),
        ],
        out_shape=[
            jax.ShapeDtypeStruct((N_TOK, D_MODEL), F32),
            jax.ShapeDtypeStruct((N_TOK, D_MODEL), F32),
            jax.ShapeDtypeStruct((N_TOK, LANES), F32),
        ],
        scratch_shapes=[pltpu.VMEM((TM, D_MODEL), BF16)],
        compiler_params=_cparams(("arbitrary", "arbitrary")),
        name="out_proj",
    )(o_rnn_p, o_rnn_s, o_fox_p, o_fox_s, x_all, g1, g2, w_out, g_moe, rw_hi, rw_lo, rb)


def _route_kernel(lg_ref, ei_ref, gate_ref, cnt_ref, carry_scr):
    t = pl.program_id(0)

    @pl.when(t == 0)
    def _():
        carry_scr[...] = jnp.zeros(carry_scr.shape, F32)

    lane = lax.broadcasted_iota(I32, (TM, LANES), 1)
    lane_f = lane.astype(F32)
    work = lg_ref[...]
    tops, idxs, hots = [], [], []
    for _ in range(TOP_K):
        m = jnp.max(work, axis=-1, keepdims=True)
        idx_f = jnp.min(jnp.where(work == m, lane_f, float(LANES)), axis=-1, keepdims=True)
        hot = lane_f == idx_f
        work = jnp.where(hot, -jnp.inf, work)
        tops.append(m)
        idxs.append(idx_f.astype(I32))
        hots.append(hot)

    es = [jnp.exp(tv - tops[0]) for tv in tops]
    denom = es[0] + es[1] + es[2] + es[3]
    gate = jnp.zeros((TM, LANES), F32)
    for k in range(TOP_K):
        gate = jnp.where(lane == k, es[k] / denom, gate)
    gate_ref[...] = gate

    multi = jnp.zeros((TM, LANES), F32)
    for k in range(TOP_K):
        multi = jnp.where(hots[k], 1.0, multi)
    r_i = lax.broadcasted_iota(I32, (TM, TM), 0)
    c_i = lax.broadcasted_iota(I32, (TM, TM), 1)
    strict_lower = jnp.where(c_i < r_i, 1.0, 0.0).astype(BF16)
    before = _dot(strict_lower, multi.astype(BF16)) + carry_scr[...]
    ei = jnp.zeros((TM, LANES), I32)
    for k in range(TOP_K):
        rank = jnp.sum(jnp.where(hots[k], before, 0.0), axis=-1, keepdims=True).astype(I32)
        ei = jnp.where(lane == k, idxs[k], ei)
        ei = jnp.where(lane == TOP_K + k, rank, ei)
    ei_ref[...] = ei
    carry_scr[...] = carry_scr[...] + jnp.sum(multi, axis=0, keepdims=True)
    cnt_ref[...] = carry_scr[...].astype(I32)


def _route(logits):
    row = lambda t: (t, 0)
    return pl.pallas_call(
        _route_kernel,
        grid=(N_TILES,),
        in_specs=[pl.BlockSpec((TM, LANES), row)],
        out_specs=[pl.BlockSpec((TM, LANES), row), pl.BlockSpec((TM, LANES), row),
                   pl.BlockSpec((1, LANES), lambda t: (0, 0))],
        out_shape=[jax.ShapeDtypeStruct((N_TOK, LANES), I32), jax.ShapeDtypeStruct((N_TOK, LANES), F32),
                   jax.ShapeDtypeStruct((1, LANES), I32)],
        scratch_shapes=[pltpu.VMEM((1, LANES), F32)],
        compiler_params=_cparams(("arbitrary",)),
        name="route",
    )(logits)


TILE_ROWS = TM * TOP_K
DMA_GROUP = 8
WAIT_GROUP = 32


def _start_rows(n, make_copy):
    def body(gi, c):
        for u in range(DMA_GROUP):
            make_copy(gi * DMA_GROUP + u).start(priority=u % 2)
        return c
    lax.fori_loop(0, n // DMA_GROUP, body, 0)


def _wait_rows(n, copy):
    def body(gi, c):
        for _ in range(WAIT_GROUP):
            copy.wait()
        return c
    lax.fori_loop(0, n // WAIT_GROUP, body, 0)


def _dispatch_kernel(meta_ref, dest_ref, x_ref, out_hbm, zero_scr, sem, zsem):
    t = pl.program_id(0)

    def row_copy(a):
        tok = lax.shift_right_logical(a, 2)
        return pltpu.make_async_copy(x_ref.at[pl.ds(tok, 1)], out_hbm.at[pl.ds(dest_ref[a], 1)], sem)

    _start_rows(TILE_ROWS, row_copy)

    @pl.when(t == 0)
    def _():
        zero_scr[...] = jnp.zeros(zero_scr.shape, F32)

        def fill(act):
            def per_row(r, c):
                act(pltpu.make_async_copy(zero_scr.at[pl.ds(0, 1)], out_hbm.at[pl.ds(r, 1)], zsem))
                return c

            def per_expert(e, c):
                lax.fori_loop(meta_ref[N_EXPERTS + e] + meta_ref[e], meta_ref[2 * N_EXPERTS + e], per_row, 0)
                return c
            lax.fori_loop(0, N_EXPERTS, per_expert, 0)

            def per_block(g, c):
                rows = pl.ds(pl.multiple_of(g * EXP_BLK, EXP_BLK), EXP_BLK)
                act(pltpu.make_async_copy(zero_scr, out_hbm.at[rows], zsem))
                return c
            lax.fori_loop(meta_ref[3 * N_EXPERTS], N_BLOCKS, per_block, 0)

        fill(lambda cp: cp.start())
        fill(lambda cp: cp.wait())

    _wait_rows(TILE_ROWS, pltpu.make_async_copy(x_ref.at[pl.ds(0, 1)], out_hbm.at[pl.ds(0, 1)], sem))


def _dispatch(meta, dest_flat, xn):
    return pl.pallas_call(
        _dispatch_kernel,
        grid_spec=pltpu.PrefetchScalarGridSpec(
            num_scalar_prefetch=1,
            grid=(N_TILES,),
            in_specs=[
                pl.BlockSpec((TILE_ROWS,), lambda t, meta: (t,), memory_space=pltpu.SMEM),
                pl.BlockSpec((TM, D_MODEL), lambda t, meta: (t, 0)),
            ],
            out_specs=pl.BlockSpec(memory_space=pl.ANY),
            scratch_shapes=[pltpu.VMEM((EXP_BLK, D_MODEL), F32), pltpu.SemaphoreType.DMA(()),
                            pltpu.SemaphoreType.DMA(())],
        ),
        out_shape=jax.ShapeDtypeStruct((N_PAD, D_MODEL), F32),
        compiler_params=_cparams(("arbitrary",)),
        name="dispatch",
    )(meta, dest_flat, xn)


EX_TF = 512
EX_NF = D_FF // EX_TF


def _experts_kernel(be_ref, nu_ref, x_ref, wg_ref, wu_ref, bg_ref, bu_ref, wd_ref, bd_ref, y_ref, xb_scr):
    del be_ref
    g = pl.program_id(0)
    f = pl.program_id(1)

    @pl.when(g < nu_ref[0])
    def _():
        @pl.when(f == 0)
        def _():
            xb_scr[...] = x_ref[...].astype(BF16)
            y_ref[...] = jnp.broadcast_to(bd_ref[0], y_ref.shape)

        xb = xb_scr[...]
        gg = jnp.minimum(_dot(xb, wg_ref[0]) + bg_ref[0], SWIGLU_LIMIT)
        uu = jnp.clip(_dot(xb, wu_ref[0]) + bu_ref[0], -SWIGLU_LIMIT, SWIGLU_LIMIT)
        hdn = (uu + 1.0) * (gg * _sigmoid(SWIGLU_ALPHA * gg))
        y_ref[...] += _dot(hdn.astype(BF16), wd_ref[0])

    @pl.when(jnp.logical_and(g >= nu_ref[0], f == 0))
    def _():
        y_ref[...] = jnp.zeros(y_ref.shape, F32)


def _experts(block_e, n_used, x_sorted, w_gu, b_gu, w_down, b_down):
    gsel = lambda g, nu: jnp.minimum(g, nu[0] - 1)
    fsel = lambda g, f, nu: jnp.where(g < nu[0], f, EX_NF - 1)
    return pl.pallas_call(
        _experts_kernel,
        grid_spec=pltpu.PrefetchScalarGridSpec(
            num_scalar_prefetch=2,
            grid=(N_BLOCKS, EX_NF),
            in_specs=[
                pl.BlockSpec((EXP_BLK, D_MODEL), lambda g, f, be, nu: (gsel(g, nu), 0)),
                pl.BlockSpec((1, D_MODEL, EX_TF), lambda g, f, be, nu: (be[gsel(g, nu)], 0, fsel(g, f, nu))),
                pl.BlockSpec((1, D_MODEL, EX_TF), lambda g, f, be, nu: (be[gsel(g, nu)], 0, EX_NF + fsel(g, f, nu))),
                pl.BlockSpec((1, 1, EX_TF), lambda g, f, be, nu: (be[gsel(g, nu)], 0, fsel(g, f, nu))),
                pl.BlockSpec((1, 1, EX_TF), lambda g, f, be, nu: (be[gsel(g, nu)], 0, EX_NF + fsel(g, f, nu))),
                pl.BlockSpec((1, EX_TF, D_MODEL), lambda g, f, be, nu: (be[gsel(g, nu)], fsel(g, f, nu), 0)),
                pl.BlockSpec((1, 1, D_MODEL), lambda g, f, be, nu: (be[gsel(g, nu)], 0, 0)),
            ],
            out_specs=pl.BlockSpec((EXP_BLK, D_MODEL), lambda g, f, be, nu: (g, 0)),
            scratch_shapes=[pltpu.VMEM((EXP_BLK, D_MODEL), BF16)],
        ),
        out_shape=jax.ShapeDtypeStruct((N_PAD, D_MODEL), F32),
        compiler_params=_cparams(("arbitrary", "arbitrary")),
        name="experts",
    )(block_e, n_used, x_sorted, w_gu, w_gu, b_gu, b_gu, w_down, b_down)


def _combine_kernel(dcur_ref, dnext_ref, y_hbm, h_ref, gate_ref, o_ref, buf, sems):
    t = pl.program_id(0)
    slot = jnp.bitwise_and(t, 1)

    def row_copy(d_ref, sl):
        def make(a):
            tok = lax.shift_right_logical(a, 2)
            k = jnp.bitwise_and(a, TOP_K - 1)
            return pltpu.make_async_copy(y_hbm.at[pl.ds(d_ref[a], 1)], buf.at[sl, k, pl.ds(tok, 1)], sems.at[sl])
        return make

    @pl.when(t == 0)
    def _():
        _start_rows(TILE_ROWS, row_copy(dcur_ref, 0))

    @pl.when(t + 1 < N_TILES)
    def _():
        _start_rows(TILE_ROWS, row_copy(dnext_ref, 1 - slot))

    _wait_rows(TILE_ROWS, pltpu.make_async_copy(y_hbm.at[pl.ds(0, 1)], buf.at[slot, 0, pl.ds(0, 1)], sems.at[slot]))
    gate = gate_ref[...]
    acc = h_ref[...]
    for k in range(TOP_K):
        acc = acc + gate[:, k:k + 1] * buf[slot, k]
    o_ref[...] = acc


def _combine(dest_flat, y_sorted, h, gate):
    row = lambda t: (t, 0)
    return pl.pallas_call(
        _combine_kernel,
        grid=(N_TILES,),
        in_specs=[
            pl.BlockSpec((TILE_ROWS,), lambda t: (t,), memory_space=pltpu.SMEM),
            pl.BlockSpec((TILE_ROWS,), lambda t: (jnp.minimum(t + 1, N_TILES - 1),), memory_space=pltpu.SMEM),
            pl.BlockSpec(memory_space=pl.ANY),
            pl.BlockSpec((TM, D_MODEL), row),
            pl.BlockSpec((TM, LANES), row),
        ],
        out_specs=pl.BlockSpec((TM, D_MODEL), row),
        out_shape=jax.ShapeDtypeStruct((N_TOK, D_MODEL), F32),
        scratch_shapes=[pltpu.VMEM((2, TOP_K, TM, D_MODEL), F32), pltpu.SemaphoreType.DMA((2,))],
        compiler_params=_cparams(("arbitrary",)),
        name="combine",
    )(dest_flat, dest_flat, y_sorted, h, gate)


PL_TN = 512
PL_NJ = D_MODEL // PL_TN


def _ple_final_kernel(h_ref, p_ref, wp_ref, gp_ref, gg_ref, wg_ref, gf_ref, yp_ref, ys_ref,
                      hn_scr, e_scr, h3_scr):
    i = pl.program_id(0)
    j = pl.program_id(1)

    @pl.when(j == 0)
    def _():
        hn_scr[...] = _rms(h_ref[...], gg_ref[...]).astype(BF16)
        e_scr[...] = _rms(_dot(p_ref[...].astype(BF16), wp_ref[...]), gp_ref[...])

    gate = _sigmoid(_dot(hn_scr[...], wg_ref[...]))
    for jj in range(PL_NJ):
        @pl.when(j == jj)
        def _(jj=jj):
            sl = slice(jj * PL_TN, (jj + 1) * PL_TN)
            h3_scr[:, sl] = h_ref[:, sl] + gate * e_scr[:, sl]

    @pl.when(jnp.logical_and(j == PL_NJ - 1, i < P_TILES))
    def _():
        yp_ref[...] = _rms(h3_scr[...], gf_ref[...])

    @pl.when(jnp.logical_and(j == PL_NJ - 1, i >= P_TILES))
    def _():
        ys_ref[...] = _rms(h3_scr[...], gf_ref[...])


def _ple_final(h2, p_all, w_proj, g_proj, g_gate, w_gate, g_final):
    row = lambda i, j: (i, 0)
    prow = lambda i, j: (jnp.minimum(i, P_TILES - 1), 0)
    srow = lambda i, j: (jnp.maximum(i - P_TILES, 0), 0)
    const = lambda i, j: (0, 0)
    return pl.pallas_call(
        _ple_final_kernel,
        grid=(N_TILES, PL_NJ),
        in_specs=[
            pl.BlockSpec((TM, D_MODEL), row),
            pl.BlockSpec((TM, D_PLE), row),
            pl.BlockSpec((D_PLE, D_MODEL), const),
            pl.BlockSpec((1, D_MODEL), const),
            pl.BlockSpec((1, D_MODEL), const),
            pl.BlockSpec((D_MODEL, PL_TN), lambda i, j: (0, j)),
            pl.BlockSpec((1, D_MODEL), const),
        ],
        out_specs=[pl.BlockSpec((TM, D_MODEL), prow), pl.BlockSpec((TM, D_MODEL), srow)],
        out_shape=[jax.ShapeDtypeStruct((N_P, D_MODEL), F32), jax.ShapeDtypeStruct((N_S, D_MODEL), F32)],
        scratch_shapes=[pltpu.VMEM((TM, D_MODEL), BF16), pltpu.VMEM((TM, D_MODEL), F32),
                        pltpu.VMEM((TM, D_MODEL), F32)],
        compiler_params=_cparams(("arbitrary", "arbitrary")),
        name="ple_final",
    )(h2, p_all, w_proj, g_proj, g_gate, w_gate, g_final)


def kernel(x_prompt, x_sample, cache_k, cache_v, cache_logf, state_conv, state_rglru, p_prompt, p_sample,
           g_mix, w_in, b_fgate, conv_w, conv_b, rg_w_a, rg_b_a, rg_w_i, rg_b_i, rg_lambda,
           g_out_rnn, g_out_fox, w_out, g_moe, router_w, router_b, w_gu, b_gu, w_down, b_down,
           w_ple_proj, g_ple_proj, g_ple_gate, w_ple_gate, g_final):
    assert g_mix.shape[0] == 1, "one layer"
    row = lambda v: v.reshape(1, -1)
    lane_pad = lambda a: jnp.pad(a, ((0, 0), (0, LANES - a.shape[1])))

    x_all = jnp.concatenate([x_prompt.reshape(N_P, D_MODEL), x_sample.reshape(N_S, D_MODEL)], axis=0)
    p_all = jnp.concatenate([p_prompt[0].reshape(N_P, D_PLE), p_sample[0].reshape(N_S, D_PLE)], axis=0)

    w_main = w_in[0][:, :D_MAIN].astype(BF16)
    w_f = lane_pad(w_in[0][:, D_MAIN:]).astype(BF16)
    b_f = lane_pad(row(b_fgate[0]))
    xy, q_all, k_p, k_s, v_p, v_s, logf = _in_proj(x_all, row(g_mix[0]), w_main, w_f, b_f)
    logf_p = logf[:N_P, :FOX_HEADS].reshape(BATCH, SEQ, FOX_HEADS)
    logf_s = logf[N_P:, :FOX_HEADS].reshape(DEC_BATCH, DEC_SEQ, FOX_HEADS)

    rg = (conv_w[0], row(conv_b[0]), rg_w_a[0].astype(BF16), row(rg_b_a[0]), rg_w_i[0].astype(BF16),
          row(rg_b_i[0]), row(rg_lambda[0]))
    o_rnn_p, conv_p, hl_p = _rglru(xy, 0, BATCH, SEQ // TM, TM,
                                   jnp.zeros((BATCH, CONV_W - 1, D_RNN), F32), jnp.zeros((BATCH, 1, D_RNN), F32),
                                   *rg, name="rglru_prompt")
    o_rnn_s, conv_s, hl_s = _rglru(xy, N_P, DEC_BATCH, 1, DEC_SEQ,
                                   state_conv[0], state_rglru[0].reshape(DEC_BATCH, 1, D_RNN),
                                   *rg, name="rglru_sample")

    c_p = _cumsum_rows(logf_p.transpose(1, 0, 2).reshape(SEQ, BATCH * FOX_HEADS), "cumsum_prompt")
    c_pT = c_p.T
    o_fox_p = _fox_prompt(q_all, k_p, v_p, c_pT[:, :, None], c_pT[:, None, :])

    lf_all = jnp.concatenate([cache_logf[0], logf_s], axis=1)
    c_s = _cumsum_rows(lf_all.transpose(1, 0, 2).reshape(PAST_LEN + DEC_SEQ, DEC_BATCH * FOX_HEADS),
                       "cumsum_sample")
    c_bth = c_s.reshape(PAST_LEN + DEC_SEQ, DEC_BATCH, FOX_HEADS).transpose(1, 0, 2)
    ck = c_bth.reshape(DEC_BATCH, 1, (PAST_LEN + DEC_SEQ) * FOX_HEADS)
    cq = c_bth[:, PAST_LEN:].reshape(DEC_BATCH, FS_ROWS, 1)
    per_req = lambda a: a.reshape(DEC_BATCH, FS_ROWS, FOX_HEAD_DIM)
    o_fox_s = _fox_sample(per_req(q_all[N_P:]), cache_k, cache_v, per_req(k_s), per_req(v_s), cq, ck)
    o_fox_s = o_fox_s.reshape(N_S, D_FOX)

    rw = lane_pad(router_w[0])
    rw_hi = rw.astype(BF16)
    rw_lo = (rw - rw_hi.astype(F32)).astype(BF16)
    rb = jnp.concatenate([row(router_b[0]), jnp.full((1, LANES - N_EXPERTS), NEG, F32)], axis=1)
    h1, xn_moe, logits = _out_proj(o_rnn_p, o_rnn_s, o_fox_p, o_fox_s, x_all, row(g_out_rnn[0]), row(g_out_fox[0]),
                                   w_out[0].astype(BF16), row(g_moe[0]), rw_hi, rw_lo, rb)

    ei, gate, cnt = _route(logits)
    counts = cnt[0, :N_EXPERTS]
    padded = (counts + EXP_BLK - 1) // EXP_BLK * EXP_BLK
    pad_end = jnp.cumsum(padded)
    pad_start = pad_end - padded
    dest = (pad_start[ei[:, :TOP_K]] + ei[:, TOP_K:2 * TOP_K]).reshape(N_ROWS)
    n_used = (pad_end[-1] // EXP_BLK).reshape(1).astype(I32)
    block_row0 = jnp.arange(N_BLOCKS, dtype=I32) * EXP_BLK
    block_e = jnp.minimum(jnp.sum((pad_end[None, :] <= block_row0[:, None]).astype(I32), axis=1), N_EXPERTS - 1)
    meta = jnp.concatenate([counts, pad_start, pad_end, n_used]).astype(I32)
    x_sorted = _dispatch(meta, dest, xn_moe)
    y_sorted = _experts(block_e, n_used, x_sorted, w_gu[0].astype(BF16), b_gu[0][:, None, :],
                        w_down[0].astype(BF16), b_down[0][:, None, :])
    h2 = _combine(dest, y_sorted, h1, gate)

    y_p, y_s = _ple_final(h2, p_all, w_ple_proj[0].astype(BF16), row(g_ple_proj[0]), row(g_ple_gate[0]),
                          w_ple_gate[0].astype(BF16), row(g_final))

    shp = (1, BATCH, SEQ, FOX_HEADS, FOX_HEAD_DIM)
    shs = (1, DEC_BATCH, DEC_SEQ, FOX_HEADS, FOX_HEAD_DIM)
    return (y_p.reshape(BATCH, SEQ, D_MODEL), y_s.reshape(DEC_BATCH, DEC_SEQ, D_MODEL),
            k_p.reshape(shp), v_p.reshape(shp), logf_p[None],
            conv_p[None], hl_p.reshape(1, BATCH, D_RNN),
            k_s.reshape(shs), v_s.reshape(shs), logf_s[None],
            conv_s[None], hl_s.reshape(1, DEC_BATCH, D_RNN))
```

```python
import functools

import jax
import jax.numpy as jnp
from jax import lax
from jax.experimental import pallas as pl
from jax.experimental.pallas import tpu as pltpu

F32 = jnp.float32
BF16 = jnp.bfloat16
I32 = jnp.int32

D_MODEL = 2048
BATCH = 4
SEQ = 2048
DEC_BATCH = 16
DEC_SEQ = 16
PAST_LEN = 4096
D_RNN = 1024
RG_BLOCKS = 8
RG_BLOCK_W = 128
CONV_W = 4
RG_C = 8.0
FOX_HEADS = 8
FOX_HEAD_DIM = 128
D_FOX = 1024
D_MAIN = 2 * D_RNN + 3 * D_FOX
N_EXPERTS = 32
TOP_K = 4
D_FF = 2048
SWIGLU_LIMIT = 7.0
SWIGLU_ALPHA = 1.702
D_PLE = 256
EPS = 1e-6

LANES = 128
N_P = BATCH * SEQ
N_S = DEC_BATCH * DEC_SEQ
N_TOK = N_P + N_S
TM = 256
N_TILES = N_TOK // TM
P_TILES = N_P // TM
N_ROWS = N_TOK * TOP_K
EXP_BLK = 256
N_BLOCKS = (N_ROWS + N_EXPERTS * (EXP_BLK - 1) + EXP_BLK - 1) // EXP_BLK
N_PAD = N_BLOCKS * EXP_BLK
NEG = -1e30
VMEM_LIMIT = 48 * 1024 * 1024


def _cparams(sem):
    return pltpu.CompilerParams(dimension_semantics=sem, vmem_limit_bytes=VMEM_LIMIT)


def _rms(x, g):
    return x * lax.rsqrt(jnp.mean(x * x, axis=-1, keepdims=True) + EPS) * g


def _log_sigmoid(z):
    return jnp.minimum(z, 0.0) - jnp.log1p(jnp.exp(-jnp.abs(z)))


def _sigmoid(z):
    return 1.0 / (1.0 + jnp.exp(-z))


def _expm1(x):
    u = jnp.exp(x)
    degenerate = jnp.logical_or(u == 1.0, u == 0.0)
    val = (u - 1.0) * x / jnp.log(jnp.where(degenerate, 0.5, u))
    return jnp.where(u == 1.0, x, jnp.where(u == 0.0, -1.0, val))


def _gelu_tanh(y):
    return 0.5 * y * (1.0 + jnp.tanh(0.7978845608028654 * (y + 0.044715 * (y * y * y))))


def _dot(a, b):
    return jnp.dot(a, b, preferred_element_type=F32)


def _dot_nt(a, b):
    return lax.dot_general(a, b, (((1,), (1,)), ((), ())), preferred_element_type=F32)


IN_TN = 512
IN_NJ = D_MAIN // IN_TN


def _in_proj_kernel(x_ref, g_ref, w_ref, wf_ref, bf_ref,
                    xy_ref, q_ref, kp_ref, ks_ref, vp_ref, vs_ref, logf_ref, xn_scr):
    i = pl.program_id(0)
    j = pl.program_id(1)

    @pl.when(j == 0)
    def _():
        xb = _rms(x_ref[...], g_ref[...]).astype(BF16)
        xn_scr[...] = xb
        logf_ref[...] = _log_sigmoid(_dot(xb, wf_ref[...]) + bf_ref[...])

    r = _dot(xn_scr[...], w_ref[...])
    is_prompt = i < P_TILES
    for jj in range(IN_NJ):
        if jj < 4:
            @pl.when(j == jj)
            def _(jj=jj):
                xy_ref[:, jj * IN_TN:(jj + 1) * IN_TN] = r
        elif jj < 6:
            @pl.when(j == jj)
            def _(jj=jj):
                q_ref[:, (jj - 4) * IN_TN:(jj - 3) * IN_TN] = r
        else:
            p_ref, s_ref, j0 = (kp_ref, ks_ref, 6) if jj < 8 else (vp_ref, vs_ref, 8)

            @pl.when(jnp.logical_and(j == jj, is_prompt))
            def _(jj=jj, p_ref=p_ref, j0=j0):
                p_ref[:, (jj - j0) * IN_TN:(jj - j0 + 1) * IN_TN] = r

            @pl.when(jnp.logical_and(j == jj, jnp.logical_not(is_prompt)))
            def _(jj=jj, s_ref=s_ref, j0=j0):
                s_ref[:, (jj - j0) * IN_TN:(jj - j0 + 1) * IN_TN] = r


def _in_proj(x_all, g_mix, w_main, w_f, b_f):
    row = lambda i, j: (i, 0)
    prow = lambda i, j: (jnp.minimum(i, P_TILES - 1), 0)
    srow = lambda i, j: (jnp.maximum(i - P_TILES, 0), 0)
    const = lambda i, j: (0, 0)
    return pl.pallas_call(
        _in_proj_kernel,
        grid=(N_TILES, IN_NJ),
        in_specs=[
            pl.BlockSpec((TM, D_MODEL), row),
            pl.BlockSpec((1, D_MODEL), const),
            pl.BlockSpec((D_MODEL, IN_TN), lambda i, j: (0, j)),
            pl.BlockSpec((D_MODEL, LANES), const),
            pl.BlockSpec((1, LANES), const),
        ],
        out_specs=[
            pl.BlockSpec((TM, 2 * D_RNN), row),
            pl.BlockSpec((TM, D_FOX), row),
            pl.BlockSpec((TM, D_FOX), prow),
            pl.BlockSpec((TM, D_FOX), srow),
            pl.BlockSpec((TM, D_FOX), prow),
            pl.BlockSpec((TM, D_FOX), srow),
            pl.BlockSpec((TM, LANES), row),
        ],
        out_shape=[
            jax.ShapeDtypeStruct((N_TOK, 2 * D_RNN), F32),
            jax.ShapeDtypeStruct((N_TOK, D_FOX), F32),
            jax.ShapeDtypeStruct((N_P, D_FOX), F32),
            jax.ShapeDtypeStruct((N_S, D_FOX), F32),
            jax.ShapeDtypeStruct((N_P, D_FOX), F32),
            jax.ShapeDtypeStruct((N_S, D_FOX), F32),
            jax.ShapeDtypeStruct((N_TOK, LANES), F32),
        ],
        scratch_shapes=[pltpu.VMEM((TM, D_MODEL), BF16)],
        compiler_params=_cparams(("arbitrary", "arbitrary")),
        name="in_proj",
    )(x_all, g_mix, w_main, w_f, b_f)


def _prefix_steps(n):
    s = 1
    while s < n:
        yield s
        s *= 2


def _rglru_kernel(x_ref, y_ref, past_ref, h0_ref, cw_ref, cb_ref, wa_ref, ba_ref, wi_ref, bi_ref, lam_ref,
                  o_ref, conv_ref, hlast_ref, xp_scr, h_scr, *, tt, nt):
    t = pl.program_id(1)
    pad = 8

    @pl.when(t == 0)
    def _():
        xp_scr[pad - (CONV_W - 1):pad, :] = past_ref[0]
        h_scr[...] = h0_ref[0]

    xp_scr[pad:pad + tt, :] = x_ref[...]
    xc = cb_ref[...] + xp_scr[pad - 3:pad - 3 + tt, :] * cw_ref[0:1, :]
    for jw in range(1, CONV_W):
        xc = xc + xp_scr[pad - 3 + jw:pad - 3 + jw + tt, :] * cw_ref[jw:jw + 1, :]

    r_parts, i_parts = [], []
    for n in range(RG_BLOCKS):
        xb = xc[:, n * RG_BLOCK_W:(n + 1) * RG_BLOCK_W].astype(BF16)
        r_parts.append(_dot(xb, wa_ref[n]))
        i_parts.append(_dot(xb, wi_ref[n]))
    r = _sigmoid(jnp.concatenate(r_parts, axis=-1) + ba_ref[...])
    ig = _sigmoid(jnp.concatenate(i_parts, axis=-1) + bi_ref[...])
    log_a = RG_C * r * _log_sigmoid(lam_ref[...])
    a = jnp.exp(log_a)
    b = jnp.sqrt(-_expm1(2.0 * log_a)) * (ig * xc)

    rowi = lax.broadcasted_iota(I32, (tt, D_RNN), 0)
    for s in _prefix_steps(tt):
        keep = rowi >= s
        a_sh = jnp.where(keep, pltpu.roll(a, s, 0), 1.0)
        b_sh = jnp.where(keep, pltpu.roll(b, s, 0), 0.0)
        b = a * b_sh + b
        a = a * a_sh
    h = a * h_scr[...] + b
    h_scr[...] = h[tt - 1:tt, :]
    o_ref[...] = h * _gelu_tanh(y_ref[...])

    tail = xp_scr[pad + tt - (CONV_W - 1):pad + tt, :]
    xp_scr[pad - (CONV_W - 1):pad, :] = tail

    @pl.when(t == nt - 1)
    def _():
        conv_ref[0] = tail
        hlast_ref[0] = h[tt - 1:tt, :]


def _rglru(xy, row0, nb, nt, tt, conv_past, h0, cw, cb, wa, ba, wi, bi, lam, name):
    blk0 = row0 // tt
    vec = lambda b, t: (0, 0)
    w3 = lambda b, t: (0, 0, 0)
    kern = functools.partial(_rglru_kernel, tt=tt, nt=nt)
    return pl.pallas_call(
        kern,
        grid=(nb, nt),
        in_specs=[
            pl.BlockSpec((tt, D_RNN), lambda b, t: (blk0 + b * nt + t, 0)),
            pl.BlockSpec((tt, D_RNN), lambda b, t: (blk0 + b * nt + t, 1)),
            pl.BlockSpec((1, CONV_W - 1, D_RNN), lambda b, t: (b, 0, 0)),
            pl.BlockSpec((1, 1, D_RNN), lambda b, t: (b, 0, 0)),
            pl.BlockSpec((CONV_W, D_RNN), vec),
            pl.BlockSpec((1, D_RNN), vec),
            pl.BlockSpec((RG_BLOCKS, RG_BLOCK_W, RG_BLOCK_W), w3),
            pl.BlockSpec((1, D_RNN), vec),
            pl.BlockSpec((RG_BLOCKS, RG_BLOCK_W, RG_BLOCK_W), w3),
            pl.BlockSpec((1, D_RNN), vec),
            pl.BlockSpec((1, D_RNN), vec),
        ],
        out_specs=[
            pl.BlockSpec((tt, D_RNN), lambda b, t: (b * nt + t, 0)),
            pl.BlockSpec((1, CONV_W - 1, D_RNN), lambda b, t: (b, 0, 0)),
            pl.BlockSpec((1, 1, D_RNN), lambda b, t: (b, 0, 0)),
        ],
        out_shape=[
            jax.ShapeDtypeStruct((nb * nt * tt, D_RNN), F32),
            jax.ShapeDtypeStruct((nb, CONV_W - 1, D_RNN), F32),
            jax.ShapeDtypeStruct((nb, 1, D_RNN), F32),
        ],
        scratch_shapes=[pltpu.VMEM((tt + 8, D_RNN), F32), pltpu.VMEM((1, D_RNN), F32)],
        compiler_params=_cparams(("arbitrary", "arbitrary")),
        name=name,
    )(xy, xy, conv_past, h0, cw, cb, wa, ba, wi, bi, lam)


def _cumsum_kernel(x_ref, o_ref, *, n):
    x = x_ref[...]
    rowi = lax.broadcasted_iota(I32, x.shape, 0)
    for s in _prefix_steps(n):
        x = x + jnp.where(rowi >= s, pltpu.roll(x, s, 0), 0.0)
    o_ref[...] = x


def _cumsum_rows(x, name):
    return pl.pallas_call(
        functools.partial(_cumsum_kernel, n=x.shape[0]),
        out_shape=jax.ShapeDtypeStruct(x.shape, F32),
        compiler_params=pltpu.CompilerParams(vmem_limit_bytes=VMEM_LIMIT),
        name=name,
    )(x)


FP_T = 512
FP_NT = SEQ // FP_T
FOX_SCALE = FOX_HEAD_DIM ** -0.5


def _fox_prompt_kernel(q_ref, k_ref, v_ref, cq_ref, ck_ref, o_ref, m_scr, l_scr, acc_scr):
    qi = pl.program_id(1)
    ki = pl.program_id(2)

    @pl.when(ki == 0)
    def _():
        m_scr[...] = jnp.full(m_scr.shape, NEG, F32)
        l_scr[...] = jnp.zeros(l_scr.shape, F32)
        acc_scr[...] = jnp.zeros(acc_scr.shape, F32)

    @pl.when(ki <= qi)
    def _():
        s = _dot_nt(q_ref[...].astype(BF16), k_ref[...].astype(BF16)) * FOX_SCALE
        s = s + (cq_ref[0] - ck_ref[0])
        rowi = lax.broadcasted_iota(I32, (FP_T, FP_T), 0)
        coli = lax.broadcasted_iota(I32, (FP_T, FP_T), 1)
        s = jnp.where(jnp.logical_or(ki < qi, coli <= rowi), s, NEG)
        m_old = m_scr[...]
        m_new = jnp.maximum(m_old, jnp.max(s, axis=-1, keepdims=True))
        alpha = jnp.exp(m_old - m_new)
        p = jnp.exp(s - m_new)
        l_scr[...] = alpha * l_scr[...] + jnp.sum(p, axis=-1, keepdims=True)
        acc_scr[...] = alpha * acc_scr[...] + _dot(p.astype(BF16), v_ref[...].astype(BF16))
        m_scr[...] = m_new

    @pl.when(ki == FP_NT - 1)
    def _():
        o_ref[...] = acc_scr[...] / l_scr[...]


def _fox_prompt(q_all, k_p, v_p, c_col, c_row):
    qmap = lambda bh, qi, ki: ((bh // FOX_HEADS) * FP_NT + qi, bh % FOX_HEADS)
    kmap = lambda bh, qi, ki: ((bh // FOX_HEADS) * FP_NT + jnp.minimum(ki, qi), bh % FOX_HEADS)
    return pl.pallas_call(
        _fox_prompt_kernel,
        grid=(BATCH * FOX_HEADS, FP_NT, FP_NT),
        in_specs=[
            pl.BlockSpec((FP_T, FOX_HEAD_DIM), qmap),
            pl.BlockSpec((FP_T, FOX_HEAD_DIM), kmap),
            pl.BlockSpec((FP_T, FOX_HEAD_DIM), kmap),
            pl.BlockSpec((1, FP_T, 1), lambda bh, qi, ki: (bh, qi, 0)),
            pl.BlockSpec((1, 1, FP_T), lambda bh, qi, ki: (bh, 0, jnp.minimum(ki, qi))),
        ],
        out_specs=pl.BlockSpec((FP_T, FOX_HEAD_DIM), qmap),
        out_shape=jax.ShapeDtypeStruct((N_P, D_FOX), F32),
        scratch_shapes=[pltpu.VMEM((FP_T, 1), F32), pltpu.VMEM((FP_T, 1), F32),
                        pltpu.VMEM((FP_T, FOX_HEAD_DIM), F32)],
        compiler_params=_cparams(("arbitrary", "arbitrary", "arbitrary")),
        name="fox_prompt",
    )(q_all, k_p, v_p, c_col, c_row)


FS_TK = 1024
FS_NT = PAST_LEN // FS_TK
FS_ROWS = DEC_SEQ * FOX_HEADS


FS_COLS = FS_TK * FOX_HEADS


def _fox_sample_kernel(q_ref, kc_ref, vc_ref, kn_ref, vn_ref, cq_ref, ckp_ref, ckn_ref, o_ref,
                       m_scr, l_scr, acc_scr):
    kt = pl.program_id(1)

    @pl.when(kt == 0)
    def _():
        m_scr[...] = jnp.full(m_scr.shape, NEG, F32)
        l_scr[...] = jnp.zeros(l_scr.shape, F32)
        acc_scr[...] = jnp.zeros(acc_scr.shape, F32)

    qb = q_ref[0].astype(BF16)

    def step(k2, v2, ck, causal):
        n = k2.shape[0]
        s = _dot_nt(qb, k2.astype(BF16)) * FOX_SCALE + (cq_ref[0] - ck)
        rowi = lax.broadcasted_iota(I32, (FS_ROWS, n), 0)
        coli = lax.broadcasted_iota(I32, (FS_ROWS, n), 1)
        keep = jnp.bitwise_and(coli, FOX_HEADS - 1) == jnp.bitwise_and(rowi, FOX_HEADS - 1)
        s = jnp.where(keep, s, NEG)
        if causal:
            s = jnp.where(jnp.right_shift(coli, 3) <= jnp.right_shift(rowi, 3), s, NEG)
        m_old = m_scr[...]
        m_new = jnp.maximum(m_old, jnp.max(s, axis=-1, keepdims=True))
        alpha = jnp.exp(m_old - m_new)
        p = jnp.exp(s - m_new)
        l_scr[...] = alpha * l_scr[...] + jnp.sum(p, axis=-1, keepdims=True)
        acc_scr[...] = alpha * acc_scr[...] + _dot(p.astype(BF16), v2.astype(BF16))
        m_scr[...] = m_new

    step(kc_ref[0, 0].reshape(FS_COLS, FOX_HEAD_DIM), vc_ref[0, 0].reshape(FS_COLS, FOX_HEAD_DIM),
         ckp_ref[0], False)

    @pl.when(kt == FS_NT - 1)
    def _():
        step(kn_ref[0], vn_ref[0], ckn_ref[0], True)
        o_ref[0] = acc_scr[...] / l_scr[...]


def _fox_sample(q_s, cache_k, cache_v, k_n, v_n, cq, ck):
    per_req = lambda b, kt: (b, 0, 0)
    cache = lambda b, kt: (0, b, kt, 0, 0)
    return pl.pallas_call(
        _fox_sample_kernel,
        grid=(DEC_BATCH, FS_NT),
        in_specs=[
            pl.BlockSpec((1, FS_ROWS, FOX_HEAD_DIM), per_req),
            pl.BlockSpec((1, 1, FS_TK, FOX_HEADS, FOX_HEAD_DIM), cache),
            pl.BlockSpec((1, 1, FS_TK, FOX_HEADS, FOX_HEAD_DIM), cache),
            pl.BlockSpec((1, FS_ROWS, FOX_HEAD_DIM), per_req),
            pl.BlockSpec((1, FS_ROWS, FOX_HEAD_DIM), per_req),
            pl.BlockSpec((1, FS_ROWS, 1), per_req),
            pl.BlockSpec((1, 1, FS_COLS), lambda b, kt: (b, 0, kt)),
            pl.BlockSpec((1, 1, FS_ROWS), lambda b, kt: (b, 0, PAST_LEN * FOX_HEADS // FS_ROWS)),
        ],
        out_specs=pl.BlockSpec((1, FS_ROWS, FOX_HEAD_DIM), per_req),
        out_shape=jax.ShapeDtypeStruct((DEC_BATCH, FS_ROWS, FOX_HEAD_DIM), F32),
        scratch_shapes=[pltpu.VMEM((FS_ROWS, 1), F32), pltpu.VMEM((FS_ROWS, 1), F32),
                        pltpu.VMEM((FS_ROWS, FOX_HEAD_DIM), F32)],
        compiler_params=_cparams(("arbitrary", "arbitrary")),
        name="fox_sample",
    )(q_s, cache_k, cache_v, k_n, v_n, cq, ck, ck)


OP_TN = 512
OP_NJ = D_MODEL // OP_TN


def _out_proj_kernel(orp_ref, ors_ref, ofp_ref, ofs_ref, x_ref, g1_ref, g2_ref, w_ref, gm_ref,
                     rwh_ref, rwl_ref, rb_ref, h_ref, xn_ref, lg_ref, mix_scr):
    i = pl.program_id(0)
    j = pl.program_id(1)

    @pl.when(jnp.logical_and(j == 0, i < P_TILES))
    def _():
        mix_scr[:, :D_RNN] = _rms(orp_ref[...], g1_ref[...]).astype(BF16)
        mix_scr[:, D_RNN:] = _rms(ofp_ref[...], g2_ref[...]).astype(BF16)

    @pl.when(jnp.logical_and(j == 0, i >= P_TILES))
    def _():
        mix_scr[:, :D_RNN] = _rms(ors_ref[...], g1_ref[...]).astype(BF16)
        mix_scr[:, D_RNN:] = _rms(ofs_ref[...], g2_ref[...]).astype(BF16)

    r = _dot(mix_scr[...], w_ref[...])
    for jj in range(OP_NJ):
        @pl.when(j == jj)
        def _(jj=jj):
            h_ref[:, jj * OP_TN:(jj + 1) * OP_TN] = x_ref[:, jj * OP_TN:(jj + 1) * OP_TN] + r

    @pl.when(j == OP_NJ - 1)
    def _():
        xn = _rms(h_ref[...], gm_ref[...])
        xn_ref[...] = xn
        xh = xn.astype(BF16)
        xl = (xn - xh.astype(F32)).astype(BF16)
        lg_ref[...] = (_dot(xh, rwh_ref[...]) + _dot(xl, rwh_ref[...]) + _dot(xh, rwl_ref[...])) + rb_ref[...]


def _out_proj(o_rnn_p, o_rnn_s, o_fox_p, o_fox_s, x_all, g1, g2, w_out, g_moe, rw_hi, rw_lo, rb):
    row = lambda i, j: (i, 0)
    prow = lambda i, j: (jnp.minimum(i, P_TILES - 1), 0)
    srow = lambda i, j: (jnp.maximum(i - P_TILES, 0), 0)
    const = lambda i, j: (0, 0)
    return pl.pallas_call(
        _out_proj_kernel,
        grid=(N_TILES, OP_NJ),
        in_specs=[
            pl.BlockSpec((TM, D_RNN), prow),
            pl.BlockSpec((TM, D_RNN), srow),
            pl.BlockSpec((TM, D_FOX), prow),
            pl.BlockSpec((TM, D_FOX), srow),
            pl.BlockSpec((TM, D_MODEL), row),
            pl.BlockSpec((1, D_RNN), const),
            pl.BlockSpec((1, D_FOX), const),
            pl.BlockSpec((D_MODEL, OP_TN), lambda i, j: (0, j)),
            pl.BlockSpec((1, D_MODEL), const),
            pl.BlockSpec((D_MODEL, LANES), const),
            pl.BlockSpec((D_MODEL, LANES), const),
            pl.BlockSpec((1, LANES), const),
        ],
        out_specs=[
            pl.BlockSpec((TM, D_MODEL), row),
            pl.BlockSpec((TM, D_MODEL), row),
            pl.BlockSpec((TM, LANES), row),
        ],
        out_shape=[
            jax.ShapeDtypeStruct((N_TOK, D_MODEL), F32),
            jax.ShapeDtypeStruct((N_TOK, D_MODEL), F32),
            jax.ShapeDtypeStruct((N_TOK, LANES), F32),
        ],
        scratch_shapes=[pltpu.VMEM((TM, D_MODEL), BF16)],
        compiler_params=_cparams(("arbitrary", "arbitrary")),
        name="out_proj",
    )(o_rnn_p, o_rnn_s, o_fox_p, o_fox_s, x_all, g1, g2, w_out, g_moe, rw_hi, rw_lo, rb)


def _route_kernel(lg_ref, ei_ref, gate_ref, cnt_ref, carry_scr):
    t = pl.program_id(0)

    @pl.when(t == 0)
    def _():
        carry_scr[...] = jnp.zeros(carry_scr.shape, F32)

    lane = lax.broadcasted_iota(I32, (TM, LANES), 1)
    lane_f = lane.astype(F32)
    work = lg_ref[...]
    tops, idxs, hots = [], [], []
    for _ in range(TOP_K):
        m = jnp.max(work, axis=-1, keepdims=True)
        idx_f = jnp.min(jnp.where(work == m, lane_f, float(LANES)), axis=-1, keepdims=True)
        hot = lane_f == idx_f
        work = jnp.where(hot, -jnp.inf, work)
        tops.append(m)
        idxs.append(idx_f.astype(I32))
        hots.append(hot)

    es = [jnp.exp(tv - tops[0]) for tv in tops]
    denom = es[0] + es[1] + es[2] + es[3]
    gate = jnp.zeros((TM, LANES), F32)
    for k in range(TOP_K):
        gate = jnp.where(lane == k, es[k] / denom, gate)
    gate_ref[...] = gate

    multi = jnp.zeros((TM, LANES), F32)
    for k in range(TOP_K):
        multi = jnp.where(hots[k], 1.0, multi)
    r_i = lax.broadcasted_iota(I32, (TM, TM), 0)
    c_i = lax.broadcasted_iota(I32, (TM, TM), 1)
    strict_lower = jnp.where(c_i < r_i, 1.0, 0.0).astype(BF16)
    before = _dot(strict_lower, multi.astype(BF16)) + carry_scr[...]
    ei = jnp.zeros((TM, LANES), I32)
    for k in range(TOP_K):
        rank = jnp.sum(jnp.where(hots[k], before, 0.0), axis=-1, keepdims=True).astype(I32)
        ei = jnp.where(lane == k, idxs[k], ei)
        ei = jnp.where(lane == TOP_K + k, rank, ei)
    ei_ref[...] = ei
    carry_scr[...] = carry_scr[...] + jnp.sum(multi, axis=0, keepdims=True)
    cnt_ref[...] = carry_scr[...].astype(I32)


def _route(logits):
    row = lambda t: (t, 0)
    return pl.pallas_call(
        _route_kernel,
        grid=(N_TILES,),
        in_specs=[pl.BlockSpec((TM, LANES), row)],
        out_specs=[pl.BlockSpec((TM, LANES), row), pl.BlockSpec((TM, LANES), row),
                   pl.BlockSpec((1, LANES), lambda t: (0, 0))],
        out_shape=[jax.ShapeDtypeStruct((N_TOK, LANES), I32), jax.ShapeDtypeStruct((N_TOK, LANES), F32),
                   jax.ShapeDtypeStruct((1, LANES), I32)],
        scratch_shapes=[pltpu.VMEM((1, LANES), F32)],
        compiler_params=_cparams(("arbitrary",)),
        name="route",
    )(logits)


TILE_ROWS = TM * TOP_K
DMA_GROUP = 8
WAIT_GROUP = 32


def _start_rows(n, make_copy):
    def body(gi, c):
        for u in range(DMA_GROUP):
            make_copy(gi * DMA_GROUP + u).start(priority=u % 2)
        return c
    lax.fori_loop(0, n // DMA_GROUP, body, 0)


def _wait_rows(n, copy):
    def body(gi, c):
        for _ in range(WAIT_GROUP):
            copy.wait()
        return c
    lax.fori_loop(0, n // WAIT_GROUP, body, 0)


def _dispatch_kernel(meta_ref, dest_ref, x_ref, out_hbm, zero_scr, sem, zsem):
    t = pl.program_id(0)

    def row_copy(a):
        tok = lax.shift_right_logical(a, 2)
        return pltpu.make_async_copy(x_ref.at[pl.ds(tok, 1)], out_hbm.at[pl.ds(dest_ref[a], 1)], sem)

    _start_rows(TILE_ROWS, row_copy)

    @pl.when(t == 0)
    def _():
        zero_scr[...] = jnp.zeros(zero_scr.shape, F32)

        def fill(act):
            def per_row(r, c):
                act(pltpu.make_async_copy(zero_scr.at[pl.ds(0, 1)], out_hbm.at[pl.ds(r, 1)], zsem))
                return c

            def per_expert(e, c):
                lax.fori_loop(meta_ref[N_EXPERTS + e] + meta_ref[e], meta_ref[2 * N_EXPERTS + e], per_row, 0)
                return c
            lax.fori_loop(0, N_EXPERTS, per_expert, 0)

            def per_block(g, c):
                rows = pl.ds(pl.multiple_of(g * EXP_BLK, EXP_BLK), EXP_BLK)
                act(pltpu.make_async_copy(zero_scr, out_hbm.at[rows], zsem))
                return c
            lax.fori_loop(meta_ref[3 * N_EXPERTS], N_BLOCKS, per_block, 0)

        fill(lambda cp: cp.start())
        fill(lambda cp: cp.wait())

    _wait_rows(TILE_ROWS, pltpu.make_async_copy(x_ref.at[pl.ds(0, 1)], out_hbm.at[pl.ds(0, 1)], sem))


def _dispatch(meta, dest_flat, xn):
    return pl.pallas_call(
        _dispatch_kernel,
        grid_spec=pltpu.PrefetchScalarGridSpec(
            num_scalar_prefetch=1,
            grid=(N_TILES,),
            in_specs=[
                pl.BlockSpec((TILE_ROWS,), lambda t, meta: (t,), memory_space=pltpu.SMEM),
                pl.BlockSpec((TM, D_MODEL), lambda t, meta: (t, 0)),
            ],
            out_specs=pl.BlockSpec(memory_space=pl.ANY),
            scratch_shapes=[pltpu.VMEM((EXP_BLK, D_MODEL), F32), pltpu.SemaphoreType.DMA(()),
                            pltpu.SemaphoreType.DMA(())],
        ),
        out_shape=jax.ShapeDtypeStruct((N_PAD, D_MODEL), F32),
        compiler_params=_cparams(("arbitrary",)),
        name="dispatch",
    )(meta, dest_flat, xn)


EX_SUBS = 4
EX_ROWS = EX_SUBS * EXP_BLK
EX_TF = 256
EX_NF = D_FF // EX_TF
EX_TN = 512
N_ITEMS = (N_BLOCKS + (EX_SUBS - 1) * N_EXPERTS) // EX_SUBS
EXPERTS_VMEM_LIMIT = 56 * 1024 * 1024


def _experts_kernel(ie_ref, ib_ref, ins_ref, ni_ref, nu_ref,
                    x0_ref, x1_ref, x2_ref, x3_ref, wg_ref, wu_ref, bg_ref, bu_ref, wd_ref, bd_ref, y_hbm,
                    xb_scr, acc_scr, wgb_scr, wub_scr, wdb_scr, osem, zsem):
    del ie_ref
    w = pl.program_id(0)
    f = pl.program_id(1)
    n_items = ni_ref[0]
    x_refs = (x0_ref, x1_ref, x2_ref, x3_ref)

    def block_rows(blk):
        return pl.ds(pl.multiple_of(blk * EXP_BLK, EXP_BLK), EXP_BLK)

    def for_item_blocks(item, act):
        for sb in range(EX_SUBS):
            @pl.when(sb < ins_ref[item])
            def _(sb=sb):
                act(pltpu.make_async_copy(acc_scr.at[pl.ds(sb * EXP_BLK, EXP_BLK)],
                                          y_hbm.at[block_rows(ib_ref[item] + sb)], osem))

    @pl.when(jnp.logical_and(w == 0, f == 0))
    def _():
        acc_scr[0:EXP_BLK, :] = jnp.zeros((EXP_BLK, D_MODEL), F32)

        def tail(act):
            def body(g, c):
                act(pltpu.make_async_copy(acc_scr.at[pl.ds(0, EXP_BLK)], y_hbm.at[block_rows(g)], zsem))
                return c
            lax.fori_loop(nu_ref[0], N_BLOCKS, body, 0)
        tail(lambda cp: cp.start())
        tail(lambda cp: cp.wait())

    @pl.when(w < n_items)
    def _():
        nsub = ins_ref[w]

        @pl.when(f == 0)
        def _():
            for sb in range(EX_SUBS):
                @pl.when(sb < nsub)
                def _(sb=sb):
                    xb_scr[sb * EXP_BLK:(sb + 1) * EXP_BLK, :] = x_refs[sb][...].astype(BF16)

        for n in range(1, EX_SUBS + 1):
            @pl.when(nsub == n)
            def _(n=n):
                m = n * EXP_BLK
                wgb_scr[...] = wg_ref[0, 0].astype(BF16)
                wub_scr[...] = wu_ref[0, 0].astype(BF16)
                wdb_scr[...] = wd_ref[0, 0].astype(BF16)
                xb = xb_scr[0:m, :]
                gg = jnp.minimum(_dot(xb, wgb_scr[...]) + bg_ref[0], SWIGLU_LIMIT)
                uu = jnp.clip(_dot(xb, wub_scr[...]) + bu_ref[0], -SWIGLU_LIMIT, SWIGLU_LIMIT)
                hb = ((uu + 1.0) * (gg * _sigmoid(SWIGLU_ALPHA * gg))).astype(BF16)

                @pl.when(jnp.logical_and(f == 0, w > 0))
                def _():
                    for_item_blocks(w - 1, lambda cp: cp.wait())

                for c in range(D_MODEL // EX_TN):
                    cols = slice(c * EX_TN, (c + 1) * EX_TN)
                    part = _dot(hb, wdb_scr[:, cols])
                    base = jnp.where(f == 0, jnp.broadcast_to(bd_ref[0][:, cols], (m, EX_TN)), acc_scr[0:m, cols])
                    acc_scr[0:m, cols] = base + part

        @pl.when(f == EX_NF - 1)
        def _():
            for_item_blocks(w, lambda cp: cp.start())

    @pl.when(jnp.logical_and(w == N_ITEMS - 1, f == EX_NF - 1))
    def _():
        for_item_blocks(n_items - 1, lambda cp: cp.wait())


def _experts(item_e, item_blk, item_nsub, n_items, n_used, x_sorted, w_gu, b_gu, w_down, b_down):
    def item(w, ni):
        return jnp.minimum(w, ni[0] - 1)

    def fsel(w, f, ni):
        return jnp.where(w < ni[0], f, EX_NF - 1)

    def x_spec(sb):
        return pl.BlockSpec(
            (EXP_BLK, D_MODEL),
            lambda w, f, ie, ib, ins, ni, nu: (ib[item(w, ni)] + jnp.minimum(sb, ins[item(w, ni)] - 1), 0))

    return pl.pallas_call(
        _experts_kernel,
        grid_spec=pltpu.PrefetchScalarGridSpec(
            num_scalar_prefetch=5,
            grid=(N_ITEMS, EX_NF),
            in_specs=[
                x_spec(0), x_spec(1), x_spec(2), x_spec(3),
                pl.BlockSpec((1, 1, D_MODEL, EX_TF),
                             lambda w, f, ie, ib, ins, ni, nu: (0, ie[item(w, ni)], 0, fsel(w, f, ni))),
                pl.BlockSpec((1, 1, D_MODEL, EX_TF),
                             lambda w, f, ie, ib, ins, ni, nu: (0, ie[item(w, ni)], 0, EX_NF + fsel(w, f, ni))),
                pl.BlockSpec((1, 1, EX_TF), lambda w, f, ie, ib, ins, ni, nu: (ie[item(w, ni)], 0, fsel(w, f, ni))),
                pl.BlockSpec((1, 1, EX_TF),
                             lambda w, f, ie, ib, ins, ni, nu: (ie[item(w, ni)], 0, EX_NF + fsel(w, f, ni))),
                pl.BlockSpec((1, 1, EX_TF, D_MODEL),
                             lambda w, f, ie, ib, ins, ni, nu: (0, ie[item(w, ni)], fsel(w, f, ni), 0)),
                pl.BlockSpec((1, 1, D_MODEL), lambda w, f, ie, ib, ins, ni, nu: (ie[item(w, ni)], 0, 0)),
            ],
            out_specs=pl.BlockSpec(memory_space=pl.ANY),
            scratch_shapes=[
                pltpu.VMEM((EX_ROWS, D_MODEL), BF16),
                pltpu.VMEM((EX_ROWS, D_MODEL), F32),
                pltpu.VMEM((D_MODEL, EX_TF), BF16),
                pltpu.VMEM((D_MODEL, EX_TF), BF16),
                pltpu.VMEM((EX_TF, D_MODEL), BF16),
                pltpu.SemaphoreType.DMA(()),
                pltpu.SemaphoreType.DMA(()),
            ],
        ),
        out_shape=jax.ShapeDtypeStruct((N_PAD, D_MODEL), F32),
        compiler_params=pltpu.CompilerParams(dimension_semantics=("arbitrary", "arbitrary"),
                                             vmem_limit_bytes=EXPERTS_VMEM_LIMIT),
        name="experts",
    )(item_e, item_blk, item_nsub, n_items, n_used, x_sorted, x_sorted, x_sorted, x_sorted,
      w_gu, w_gu, b_gu, b_gu, w_down, b_down)


def _combine_kernel(dcur_ref, dnext_ref, y_hbm, h_ref, gate_ref, o_ref, buf, sems):
    t = pl.program_id(0)
    slot = jnp.bitwise_and(t, 1)

    def row_copy(d_ref, sl):
        def make(a):
            tok = lax.shift_right_logical(a, 2)
            k = jnp.bitwise_and(a, TOP_K - 1)
            return pltpu.make_async_copy(y_hbm.at[pl.ds(d_ref[a], 1)], buf.at[sl, k, pl.ds(tok, 1)], sems.at[sl])
        return make

    @pl.when(t == 0)
    def _():
        _start_rows(TILE_ROWS, row_copy(dcur_ref, 0))

    @pl.when(t + 1 < N_TILES)
    def _():
        _start_rows(TILE_ROWS, row_copy(dnext_ref, 1 - slot))

    _wait_rows(TILE_ROWS, pltpu.make_async_copy(y_hbm.at[pl.ds(0, 1)], buf.at[slot, 0, pl.ds(0, 1)], sems.at[slot]))
    gate = gate_ref[...]
    acc = h_ref[...]
    for k in range(TOP_K):
        acc = acc + gate[:, k:k + 1] * buf[slot, k]
    o_ref[...] = acc


def _combine(dest_flat, y_sorted, h, gate):
    row = lambda t: (t, 0)
    return pl.pallas_call(
        _combine_kernel,
        grid=(N_TILES,),
        in_specs=[
            pl.BlockSpec((TILE_ROWS,), lambda t: (t,), memory_space=pltpu.SMEM),
            pl.BlockSpec((TILE_ROWS,), lambda t: (jnp.minimum(t + 1, N_TILES - 1),), memory_space=pltpu.SMEM),
            pl.BlockSpec(memory_space=pl.ANY),
            pl.BlockSpec((TM, D_MODEL), row),
            pl.BlockSpec((TM, LANES), row),
        ],
        out_specs=pl.BlockSpec((TM, D_MODEL), row),
        out_shape=jax.ShapeDtypeStruct((N_TOK, D_MODEL), F32),
        scratch_shapes=[pltpu.VMEM((2, TOP_K, TM, D_MODEL), F32), pltpu.SemaphoreType.DMA((2,))],
        compiler_params=_cparams(("arbitrary",)),
        name="combine",
    )(dest_flat, dest_flat, y_sorted, h, gate)


PL_TN = 512
PL_NJ = D_MODEL // PL_TN


def _ple_final_kernel(h_ref, p_ref, wp_ref, gp_ref, gg_ref, wg_ref, gf_ref, yp_ref, ys_ref,
                      hn_scr, e_scr, h3_scr):
    i = pl.program_id(0)
    j = pl.program_id(1)

    @pl.when(j == 0)
    def _():
        hn_scr[...] = _rms(h_ref[...], gg_ref[...]).astype(BF16)
        e_scr[...] = _rms(_dot(p_ref[...].astype(BF16), wp_ref[...]), gp_ref[...])

    gate = _sigmoid(_dot(hn_scr[...], wg_ref[...]))
    for jj in range(PL_NJ):
        @pl.when(j == jj)
        def _(jj=jj):
            sl = slice(jj * PL_TN, (jj + 1) * PL_TN)
            h3_scr[:, sl] = h_ref[:, sl] + gate * e_scr[:, sl]

    @pl.when(jnp.logical_and(j == PL_NJ - 1, i < P_TILES))
    def _():
        yp_ref[...] = _rms(h3_scr[...], gf_ref[...])

    @pl.when(jnp.logical_and(j == PL_NJ - 1, i >= P_TILES))
    def _():
        ys_ref[...] = _rms(h3_scr[...], gf_ref[...])


def _ple_final(h2, p_all, w_proj, g_proj, g_gate, w_gate, g_final):
    row = lambda i, j: (i, 0)
    prow = lambda i, j: (jnp.minimum(i, P_TILES - 1), 0)
    srow = lambda i, j: (jnp.maximum(i - P_TILES, 0), 0)
    const = lambda i, j: (0, 0)
    return pl.pallas_call(
        _ple_final_kernel,
        grid=(N_TILES, PL_NJ),
        in_specs=[
            pl.BlockSpec((TM, D_MODEL), row),
            pl.BlockSpec((TM, D_PLE), row),
            pl.BlockSpec((D_PLE, D_MODEL), const),
            pl.BlockSpec((1, D_MODEL), const),
            pl.BlockSpec((1, D_MODEL), const),
            pl.BlockSpec((D_MODEL, PL_TN), lambda i, j: (0, j)),
            pl.BlockSpec((1, D_MODEL), const),
        ],
        out_specs=[pl.BlockSpec((TM, D_MODEL), prow), pl.BlockSpec((TM, D_MODEL), srow)],
        out_shape=[jax.ShapeDtypeStruct((N_P, D_MODEL), F32), jax.ShapeDtypeStruct((N_S, D_MODEL), F32)],
        scratch_shapes=[pltpu.VMEM((TM, D_MODEL), BF16), pltpu.VMEM((TM, D_MODEL), F32),
                        pltpu.VMEM((TM, D_MODEL), F32)],
        compiler_params=_cparams(("arbitrary", "arbitrary")),
        name="ple_final",
    )(h2, p_all, w_proj, g_proj, g_gate, w_gate, g_final)


def kernel(x_prompt, x_sample, cache_k, cache_v, cache_logf, state_conv, state_rglru, p_prompt, p_sample,
           g_mix, w_in, b_fgate, conv_w, conv_b, rg_w_a, rg_b_a, rg_w_i, rg_b_i, rg_lambda,
           g_out_rnn, g_out_fox, w_out, g_moe, router_w, router_b, w_gu, b_gu, w_down, b_down,
           w_ple_proj, g_ple_proj, g_ple_gate, w_ple_gate, g_final):
    assert g_mix.shape[0] == 1, "one layer"
    row = lambda v: v.reshape(1, -1)
    lane_pad = lambda a: jnp.pad(a, ((0, 0), (0, LANES - a.shape[1])))

    x_all = jnp.concatenate([x_prompt.reshape(N_P, D_MODEL), x_sample.reshape(N_S, D_MODEL)], axis=0)
    p_all = jnp.concatenate([p_prompt[0].reshape(N_P, D_PLE), p_sample[0].reshape(N_S, D_PLE)], axis=0)

    w_main = w_in[0][:, :D_MAIN].astype(BF16)
    w_f = lane_pad(w_in[0][:, D_MAIN:]).astype(BF16)
    b_f = lane_pad(row(b_fgate[0]))
    xy, q_all, k_p, k_s, v_p, v_s, logf = _in_proj(x_all, row(g_mix[0]), w_main, w_f, b_f)
    logf_p = logf[:N_P, :FOX_HEADS].reshape(BATCH, SEQ, FOX_HEADS)
    logf_s = logf[N_P:, :FOX_HEADS].reshape(DEC_BATCH, DEC_SEQ, FOX_HEADS)

    rg = (conv_w[0], row(conv_b[0]), rg_w_a[0].astype(BF16), row(rg_b_a[0]), rg_w_i[0].astype(BF16),
          row(rg_b_i[0]), row(rg_lambda[0]))
    o_rnn_p, conv_p, hl_p = _rglru(xy, 0, BATCH, SEQ // TM, TM,
                                   jnp.zeros((BATCH, CONV_W - 1, D_RNN), F32), jnp.zeros((BATCH, 1, D_RNN), F32),
                                   *rg, name="rglru_prompt")
    o_rnn_s, conv_s, hl_s = _rglru(xy, N_P, DEC_BATCH, 1, DEC_SEQ,
                                   state_conv[0], state_rglru[0].reshape(DEC_BATCH, 1, D_RNN),
                                   *rg, name="rglru_sample")

    c_p = _cumsum_rows(logf_p.transpose(1, 0, 2).reshape(SEQ, BATCH * FOX_HEADS), "cumsum_prompt")
    c_pT = c_p.T
    o_fox_p = _fox_prompt(q_all, k_p, v_p, c_pT[:, :, None], c_pT[:, None, :])

    lf_all = jnp.concatenate([cache_logf[0], logf_s], axis=1)
    c_s = _cumsum_rows(lf_all.transpose(1, 0, 2).reshape(PAST_LEN + DEC_SEQ, DEC_BATCH * FOX_HEADS),
                       "cumsum_sample")
    c_bth = c_s.reshape(PAST_LEN + DEC_SEQ, DEC_BATCH, FOX_HEADS).transpose(1, 0, 2)
    ck = c_bth.reshape(DEC_BATCH, 1, (PAST_LEN + DEC_SEQ) * FOX_HEADS)
    cq = c_bth[:, PAST_LEN:].reshape(DEC_BATCH, FS_ROWS, 1)
    per_req = lambda a: a.reshape(DEC_BATCH, FS_ROWS, FOX_HEAD_DIM)
    o_fox_s = _fox_sample(per_req(q_all[N_P:]), cache_k, cache_v, per_req(k_s), per_req(v_s), cq, ck)
    o_fox_s = o_fox_s.reshape(N_S, D_FOX)

    rw = lane_pad(router_w[0])
    rw_hi = rw.astype(BF16)
    rw_lo = (rw - rw_hi.astype(F32)).astype(BF16)
    rb = jnp.concatenate([row(router_b[0]), jnp.full((1, LANES - N_EXPERTS), NEG, F32)], axis=1)
    h1, xn_moe, logits = _out_proj(o_rnn_p, o_rnn_s, o_fox_p, o_fox_s, x_all, row(g_out_rnn[0]), row(g_out_fox[0]),
                                   w_out[0].astype(BF16), row(g_moe[0]), rw_hi, rw_lo, rb)

    ei, gate, cnt = _route(logits)
    counts = cnt[0, :N_EXPERTS]
    padded = (counts + EXP_BLK - 1) // EXP_BLK * EXP_BLK
    pad_end = jnp.cumsum(padded)
    pad_start = pad_end - padded
    dest = (pad_start[ei[:, :TOP_K]] + ei[:, TOP_K:2 * TOP_K]).reshape(N_ROWS)
    n_used = (pad_end[-1] // EXP_BLK).reshape(1).astype(I32)
    meta = jnp.concatenate([counts, pad_start, pad_end, n_used]).astype(I32)
    x_sorted = _dispatch(meta, dest, xn_moe)
    blocks_e = padded // EXP_BLK
    items_e = (blocks_e + EX_SUBS - 1) // EX_SUBS
    item_end = jnp.cumsum(items_e)
    item_first = item_end - items_e
    item_id = jnp.arange(N_ITEMS, dtype=I32)
    item_e = jnp.minimum(jnp.sum((item_end[None, :] <= item_id[:, None]).astype(I32), axis=1), N_EXPERTS - 1)
    item_run = item_id - item_first[item_e]
    item_blk = jnp.clip(pad_start[item_e] // EXP_BLK + EX_SUBS * item_run, 0, N_BLOCKS - 1).astype(I32)
    item_nsub = jnp.clip(blocks_e[item_e] - EX_SUBS * item_run, 1, EX_SUBS).astype(I32)
    n_items = item_end[-1].reshape(1).astype(I32)
    y_sorted = _experts(item_e.astype(I32), item_blk, item_nsub, n_items, n_used, x_sorted,
                        w_gu, b_gu[0][:, None, :], w_down, b_down[0][:, None, :])
    h2 = _combine(dest, y_sorted, h1, gate)

    y_p, y_s = _ple_final(h2, p_all, w_ple_proj[0].astype(BF16), row(g_ple_proj[0]), row(g_ple_gate[0]),
                          w_ple_gate[0].astype(BF16), row(g_final))

    shp = (1, BATCH, SEQ, FOX_HEADS, FOX_HEAD_DIM)
    shs = (1, DEC_BATCH, DEC_SEQ, FOX_HEADS, FOX_HEAD_DIM)
    return (y_p.reshape(BATCH, SEQ, D_MODEL), y_s.reshape(DEC_BATCH, DEC_SEQ, D_MODEL),
            k_p.reshape(shp), v_p.reshape(shp), logf_p[None],
            conv_p[None], hl_p.reshape(1, BATCH, D_RNN),
            k_s.reshape(shs), v_s.reshape(shs), logf_s[None],
            conv_s[None], hl_s.reshape(1, DEC_BATCH, D_RNN))
```

```python
import functools

import jax
import jax.numpy as jnp
from jax import lax
from jax.experimental import pallas as pl
from jax.experimental.pallas import tpu as pltpu

F32 = jnp.float32
BF16 = jnp.bfloat16
I32 = jnp.int32

D_MODEL = 2048
BATCH = 4
SEQ = 2048
DEC_BATCH = 16
DEC_SEQ = 16
PAST_LEN = 4096
D_RNN = 1024
RG_BLOCKS = 8
RG_BLOCK_W = 128
CONV_W = 4
RG_C = 8.0
FOX_HEADS = 8
FOX_HEAD_DIM = 128
D_FOX = 1024
D_MAIN = 2 * D_RNN + 3 * D_FOX
N_EXPERTS = 32
TOP_K = 4
D_FF = 2048
SWIGLU_LIMIT = 7.0
SWIGLU_ALPHA = 1.702
D_PLE = 256
EPS = 1e-6

LANES = 128
N_P = BATCH * SEQ
N_S = DEC_BATCH * DEC_SEQ
N_TOK = N_P + N_S
TM = 256
TM_IN = 1024
TM_OUT = 512
N_TILES = N_TOK // TM
P_TILES = N_P // TM
N_ROWS = N_TOK * TOP_K
EXP_BLK = 256
N_BLOCKS = (N_ROWS + N_EXPERTS * (EXP_BLK - 1) + EXP_BLK - 1) // EXP_BLK
N_PAD = N_BLOCKS * EXP_BLK
NEG = -1e30
VMEM_LIMIT = 48 * 1024 * 1024


def _cparams(sem):
    return pltpu.CompilerParams(dimension_semantics=sem, vmem_limit_bytes=VMEM_LIMIT)


def _rms(x, g):
    return x * lax.rsqrt(jnp.mean(x * x, axis=-1, keepdims=True) + EPS) * g


def _log_sigmoid(z):
    return jnp.minimum(z, 0.0) - jnp.log1p(jnp.exp(-jnp.abs(z)))


def _sigmoid(z):
    return 1.0 / (1.0 + jnp.exp(-z))


def _expm1(x):
    u = jnp.exp(x)
    degenerate = jnp.logical_or(u == 1.0, u == 0.0)
    val = (u - 1.0) * x / jnp.log(jnp.where(degenerate, 0.5, u))
    return jnp.where(u == 1.0, x, jnp.where(u == 0.0, -1.0, val))


def _gelu_tanh(y):
    return 0.5 * y * (1.0 + jnp.tanh(0.7978845608028654 * (y + 0.044715 * (y * y * y))))


def _dot(a, b):
    return jnp.dot(a, b, preferred_element_type=F32)


def _dot_nt(a, b):
    return lax.dot_general(a, b, (((1,), (1,)), ((), ())), preferred_element_type=F32)


IN_TN = 512
IN_NJ = D_MAIN // IN_TN
COL_Q = 2 * D_RNN
COL_K = COL_Q + D_FOX
COL_V = COL_K + D_FOX


def _in_proj_kernel(x_ref, g_ref, w_ref, wf_ref, bf_ref, proj_ref, logf_ref, xn_scr):
    @pl.when(pl.program_id(1) == 0)
    def _():
        xb = _rms(x_ref[...], g_ref[...]).astype(BF16)
        xn_scr[...] = xb
        logf_ref[...] = _log_sigmoid(_dot(xb, wf_ref[...]) + bf_ref[...])

    proj_ref[...] = _dot(xn_scr[...], w_ref[...])


def _in_proj(x, tm, g_mix, w_main, w_f, b_f, name):
    n = x.shape[0]
    row = lambda i, j: (i, 0)
    const = lambda i, j: (0, 0)
    return pl.pallas_call(
        _in_proj_kernel,
        grid=(n // tm, IN_NJ),
        in_specs=[
            pl.BlockSpec((tm, D_MODEL), row),
            pl.BlockSpec((1, D_MODEL), const),
            pl.BlockSpec((D_MODEL, IN_TN), lambda i, j: (0, j)),
            pl.BlockSpec((D_MODEL, LANES), const),
            pl.BlockSpec((1, LANES), const),
        ],
        out_specs=[pl.BlockSpec((tm, IN_TN), lambda i, j: (i, j)), pl.BlockSpec((tm, LANES), row)],
        out_shape=[jax.ShapeDtypeStruct((n, D_MAIN), F32), jax.ShapeDtypeStruct((n, LANES), F32)],
        scratch_shapes=[pltpu.VMEM((tm, D_MODEL), BF16)],
        compiler_params=_cparams(("arbitrary", "arbitrary")),
        name=name,
    )(x, g_mix, w_main, w_f, b_f)


def _prefix_steps(n):
    s = 1
    while s < n:
        yield s
        s *= 2


def _rglru_kernel(x_ref, y_ref, past_ref, h0_ref, cw_ref, cb_ref, wa_ref, ba_ref, wi_ref, bi_ref, lam_ref,
                  o_ref, conv_ref, hlast_ref, xp_scr, h_scr, *, tt, nt):
    t = pl.program_id(1)
    pad = 8

    @pl.when(t == 0)
    def _():
        xp_scr[pad - (CONV_W - 1):pad, :] = past_ref[0]
        h_scr[...] = h0_ref[0]

    xp_scr[pad:pad + tt, :] = x_ref[...]
    xc = cb_ref[...] + xp_scr[pad - 3:pad - 3 + tt, :] * cw_ref[0:1, :]
    for jw in range(1, CONV_W):
        xc = xc + xp_scr[pad - 3 + jw:pad - 3 + jw + tt, :] * cw_ref[jw:jw + 1, :]

    r_parts, i_parts = [], []
    for n in range(RG_BLOCKS):
        xb = xc[:, n * RG_BLOCK_W:(n + 1) * RG_BLOCK_W].astype(BF16)
        r_parts.append(_dot(xb, wa_ref[n]))
        i_parts.append(_dot(xb, wi_ref[n]))
    r = _sigmoid(jnp.concatenate(r_parts, axis=-1) + ba_ref[...])
    ig = _sigmoid(jnp.concatenate(i_parts, axis=-1) + bi_ref[...])
    log_a = RG_C * r * _log_sigmoid(lam_ref[...])
    a = jnp.exp(log_a)
    b = jnp.sqrt(-_expm1(2.0 * log_a)) * (ig * xc)

    rowi = lax.broadcasted_iota(I32, (tt, D_RNN), 0)
    for s in _prefix_steps(tt):
        keep = rowi >= s
        a_sh = jnp.where(keep, pltpu.roll(a, s, 0), 1.0)
        b_sh = jnp.where(keep, pltpu.roll(b, s, 0), 0.0)
        b = a * b_sh + b
        a = a * a_sh
    h = a * h_scr[...] + b
    h_scr[...] = h[tt - 1:tt, :]
    o_ref[...] = h * _gelu_tanh(y_ref[...])

    tail = xp_scr[pad + tt - (CONV_W - 1):pad + tt, :]
    xp_scr[pad - (CONV_W - 1):pad, :] = tail

    @pl.when(t == nt - 1)
    def _():
        conv_ref[0] = tail
        hlast_ref[0] = h[tt - 1:tt, :]


def _rglru(xy, row0, nb, nt, tt, conv_past, h0, cw, cb, wa, ba, wi, bi, lam, name):
    blk0 = row0 // tt
    vec = lambda b, t: (0, 0)
    w3 = lambda b, t: (0, 0, 0)
    kern = functools.partial(_rglru_kernel, tt=tt, nt=nt)
    return pl.pallas_call(
        kern,
        grid=(nb, nt),
        in_specs=[
            pl.BlockSpec((tt, D_RNN), lambda b, t: (blk0 + b * nt + t, 0)),
            pl.BlockSpec((tt, D_RNN), lambda b, t: (blk0 + b * nt + t, 1)),
            pl.BlockSpec((1, CONV_W - 1, D_RNN), lambda b, t: (b, 0, 0)),
            pl.BlockSpec((1, 1, D_RNN), lambda b, t: (b, 0, 0)),
            pl.BlockSpec((CONV_W, D_RNN), vec),
            pl.BlockSpec((1, D_RNN), vec),
            pl.BlockSpec((RG_BLOCKS, RG_BLOCK_W, RG_BLOCK_W), w3),
            pl.BlockSpec((1, D_RNN), vec),
            pl.BlockSpec((RG_BLOCKS, RG_BLOCK_W, RG_BLOCK_W), w3),
            pl.BlockSpec((1, D_RNN), vec),
            pl.BlockSpec((1, D_RNN), vec),
        ],
        out_specs=[
            pl.BlockSpec((tt, D_RNN), lambda b, t: (b * nt + t, 0)),
            pl.BlockSpec((1, CONV_W - 1, D_RNN), lambda b, t: (b, 0, 0)),
            pl.BlockSpec((1, 1, D_RNN), lambda b, t: (b, 0, 0)),
        ],
        out_shape=[
            jax.ShapeDtypeStruct((nb * nt * tt, D_RNN), F32),
            jax.ShapeDtypeStruct((nb, CONV_W - 1, D_RNN), F32),
            jax.ShapeDtypeStruct((nb, 1, D_RNN), F32),
        ],
        scratch_shapes=[pltpu.VMEM((tt + 8, D_RNN), F32), pltpu.VMEM((1, D_RNN), F32)],
        compiler_params=_cparams(("arbitrary", "arbitrary")),
        name=name,
    )(xy, xy, conv_past, h0, cw, cb, wa, ba, wi, bi, lam)


def _cumsum_kernel(x_ref, o_ref, *, n):
    x = x_ref[...]
    rowi = lax.broadcasted_iota(I32, x.shape, 0)
    for s in _prefix_steps(n):
        x = x + jnp.where(rowi >= s, pltpu.roll(x, s, 0), 0.0)
    o_ref[...] = x


def _cumsum_rows(x, name):
    return pl.pallas_call(
        functools.partial(_cumsum_kernel, n=x.shape[0]),
        out_shape=jax.ShapeDtypeStruct(x.shape, F32),
        compiler_params=pltpu.CompilerParams(vmem_limit_bytes=VMEM_LIMIT),
        name=name,
    )(x)


FP_T = 512
FP_NT = SEQ // FP_T
FOX_SCALE = FOX_HEAD_DIM ** -0.5


FP_HB = 4
FP_W = FP_HB * FOX_HEAD_DIM
FP_GROUPS = FOX_HEADS // FP_HB


def _fox_prompt_kernel(q_ref, k_ref, v_ref, cq_ref, ck_ref, o_ref, m_scr, l_scr, acc_scr):
    qi = pl.program_id(1)
    ki = pl.program_id(2)

    @pl.when(ki == 0)
    def _():
        m_scr[...] = jnp.full(m_scr.shape, NEG, F32)
        l_scr[...] = jnp.zeros(l_scr.shape, F32)
        acc_scr[...] = jnp.zeros(acc_scr.shape, F32)

    def update(diagonal):
        for hh in range(FP_HB):
            cols = slice(hh * FOX_HEAD_DIM, (hh + 1) * FOX_HEAD_DIM)
            s = _dot_nt(q_ref[:, cols].astype(BF16), k_ref[:, cols].astype(BF16)) * FOX_SCALE
            s = s + (cq_ref[hh] - ck_ref[hh])
            if diagonal:
                rowi = lax.broadcasted_iota(I32, (FP_T, FP_T), 0)
                coli = lax.broadcasted_iota(I32, (FP_T, FP_T), 1)
                s = jnp.where(coli <= rowi, s, NEG)
            m_old = m_scr[hh]
            m_new = jnp.maximum(m_old, jnp.max(s, axis=-1, keepdims=True))
            alpha = jnp.exp(m_old - m_new)
            p = jnp.exp(s - m_new)
            l_scr[hh] = alpha * l_scr[hh] + jnp.sum(p, axis=-1, keepdims=True)
            acc_scr[:, cols] = alpha * acc_scr[:, cols] + _dot(p.astype(BF16), v_ref[:, cols].astype(BF16))
            m_scr[hh] = m_new

    @pl.when(ki < qi)
    def _():
        update(False)

    @pl.when(ki == qi)
    def _():
        update(True)

    @pl.when(ki == FP_NT - 1)
    def _():
        for hh in range(FP_HB):
            cols = slice(hh * FOX_HEAD_DIM, (hh + 1) * FOX_HEAD_DIM)
            o_ref[:, cols] = acc_scr[:, cols] / l_scr[hh]


def _fox_prompt(proj_p, c_col, c_row):
    def rows(g, blk):
        return (g // FP_GROUPS) * FP_NT + blk

    def colblk(g, col0):
        return col0 // FP_W + g % FP_GROUPS

    seen = lambda qi, ki: jnp.minimum(ki, qi)
    return pl.pallas_call(
        _fox_prompt_kernel,
        grid=(BATCH * FP_GROUPS, FP_NT, FP_NT),
        in_specs=[
            pl.BlockSpec((FP_T, FP_W), lambda g, qi, ki: (rows(g, qi), colblk(g, COL_Q))),
            pl.BlockSpec((FP_T, FP_W), lambda g, qi, ki: (rows(g, seen(qi, ki)), colblk(g, COL_K))),
            pl.BlockSpec((FP_T, FP_W), lambda g, qi, ki: (rows(g, seen(qi, ki)), colblk(g, COL_V))),
            pl.BlockSpec((FP_HB, FP_T, 1), lambda g, qi, ki: (g, qi, 0)),
            pl.BlockSpec((FP_HB, 1, FP_T), lambda g, qi, ki: (g, 0, seen(qi, ki))),
        ],
        out_specs=pl.BlockSpec((FP_T, FP_W), lambda g, qi, ki: (rows(g, qi), g % FP_GROUPS)),
        out_shape=jax.ShapeDtypeStruct((N_P, D_FOX), F32),
        scratch_shapes=[pltpu.VMEM((FP_HB, FP_T, 1), F32), pltpu.VMEM((FP_HB, FP_T, 1), F32),
                        pltpu.VMEM((FP_T, FP_W), F32)],
        compiler_params=_cparams(("arbitrary", "arbitrary", "arbitrary")),
        name="fox_prompt",
    )(proj_p, proj_p, proj_p, c_col, c_row)


FS_TK = 1024
FS_NT = PAST_LEN // FS_TK
FS_ROWS = DEC_SEQ * FOX_HEADS


FS_COLS = FS_TK * FOX_HEADS


def _fox_sample_kernel(q_ref, kc_ref, vc_ref, kn_ref, vn_ref, cq_ref, ckp_ref, ckn_ref, o_ref,
                       m_scr, l_scr, acc_scr):
    kt = pl.program_id(1)

    @pl.when(kt == 0)
    def _():
        m_scr[...] = jnp.full(m_scr.shape, NEG, F32)
        l_scr[...] = jnp.zeros(l_scr.shape, F32)
        acc_scr[...] = jnp.zeros(acc_scr.shape, F32)

    qb = q_ref[0].astype(BF16)

    def step(k2, v2, ck, causal):
        n = k2.shape[0]
        s = _dot_nt(qb, k2.astype(BF16)) * FOX_SCALE + (cq_ref[0] - ck)
        rowi = lax.broadcasted_iota(I32, (FS_ROWS, n), 0)
        coli = lax.broadcasted_iota(I32, (FS_ROWS, n), 1)
        keep = jnp.bitwise_and(coli, FOX_HEADS - 1) == jnp.bitwise_and(rowi, FOX_HEADS - 1)
        s = jnp.where(keep, s, NEG)
        if causal:
            s = jnp.where(jnp.right_shift(coli, 3) <= jnp.right_shift(rowi, 3), s, NEG)
        m_old = m_scr[...]
        m_new = jnp.maximum(m_old, jnp.max(s, axis=-1, keepdims=True))
        alpha = jnp.exp(m_old - m_new)
        p = jnp.exp(s - m_new)
        l_scr[...] = alpha * l_scr[...] + jnp.sum(p, axis=-1, keepdims=True)
        acc_scr[...] = alpha * acc_scr[...] + _dot(p.astype(BF16), v2.astype(BF16))
        m_scr[...] = m_new

    step(kc_ref[0, 0].reshape(FS_COLS, FOX_HEAD_DIM), vc_ref[0, 0].reshape(FS_COLS, FOX_HEAD_DIM),
         ckp_ref[0], False)

    @pl.when(kt == FS_NT - 1)
    def _():
        step(kn_ref[0], vn_ref[0], ckn_ref[0], True)
        o_ref[0] = acc_scr[...] / l_scr[...]


def _fox_sample(q_s, cache_k, cache_v, k_n, v_n, cq, ck):
    per_req = lambda b, kt: (b, 0, 0)
    cache = lambda b, kt: (0, b, kt, 0, 0)
    return pl.pallas_call(
        _fox_sample_kernel,
        grid=(DEC_BATCH, FS_NT),
        in_specs=[
            pl.BlockSpec((1, FS_ROWS, FOX_HEAD_DIM), per_req),
            pl.BlockSpec((1, 1, FS_TK, FOX_HEADS, FOX_HEAD_DIM), cache),
            pl.BlockSpec((1, 1, FS_TK, FOX_HEADS, FOX_HEAD_DIM), cache),
            pl.BlockSpec((1, FS_ROWS, FOX_HEAD_DIM), per_req),
            pl.BlockSpec((1, FS_ROWS, FOX_HEAD_DIM), per_req),
            pl.BlockSpec((1, FS_ROWS, 1), per_req),
            pl.BlockSpec((1, 1, FS_COLS), lambda b, kt: (b, 0, kt)),
            pl.BlockSpec((1, 1, FS_ROWS), lambda b, kt: (b, 0, PAST_LEN * FOX_HEADS // FS_ROWS)),
        ],
        out_specs=pl.BlockSpec((1, FS_ROWS, FOX_HEAD_DIM), per_req),
        out_shape=jax.ShapeDtypeStruct((DEC_BATCH, FS_ROWS, FOX_HEAD_DIM), F32),
        scratch_shapes=[pltpu.VMEM((FS_ROWS, 1), F32), pltpu.VMEM((FS_ROWS, 1), F32),
                        pltpu.VMEM((FS_ROWS, FOX_HEAD_DIM), F32)],
        compiler_params=_cparams(("arbitrary", "arbitrary")),
        name="fox_sample",
    )(q_s, cache_k, cache_v, k_n, v_n, cq, ck, ck)


OP_TN = 512
OP_NJ = D_MODEL // OP_TN


def _out_proj_kernel(orn_ref, ofx_ref, x_ref, g1_ref, g2_ref, w_ref, gm_ref,
                     rwh_ref, rwl_ref, rb_ref, h_ref, xn_ref, lg_ref, mix_scr):
    j = pl.program_id(1)

    @pl.when(j == 0)
    def _():
        mix_scr[:, :D_RNN] = _rms(orn_ref[...], g1_ref[...]).astype(BF16)
        mix_scr[:, D_RNN:] = _rms(ofx_ref[...], g2_ref[...]).astype(BF16)

    r = _dot(mix_scr[...], w_ref[...])
    for jj in range(OP_NJ):
        @pl.when(j == jj)
        def _(jj=jj):
            h_ref[:, jj * OP_TN:(jj + 1) * OP_TN] = x_ref[:, jj * OP_TN:(jj + 1) * OP_TN] + r

    @pl.when(j == OP_NJ - 1)
    def _():
        xn = _rms(h_ref[...], gm_ref[...])
        xn_ref[...] = xn
        xh = xn.astype(BF16)
        xl = (xn - xh.astype(F32)).astype(BF16)
        lg_ref[...] = (_dot(xh, rwh_ref[...]) + _dot(xl, rwh_ref[...]) + _dot(xh, rwl_ref[...])) + rb_ref[...]


def _out_proj(o_rnn, o_fox, x, tm, g1, g2, w_out, g_moe, rw_hi, rw_lo, rb, name):
    n = x.shape[0]
    row = lambda i, j: (i, 0)
    const = lambda i, j: (0, 0)
    return pl.pallas_call(
        _out_proj_kernel,
        grid=(n // tm, OP_NJ),
        in_specs=[
            pl.BlockSpec((tm, D_RNN), row),
            pl.BlockSpec((tm, D_FOX), row),
            pl.BlockSpec((tm, D_MODEL), row),
            pl.BlockSpec((1, D_RNN), const),
            pl.BlockSpec((1, D_FOX), const),
            pl.BlockSpec((D_MODEL, OP_TN), lambda i, j: (0, j)),
            pl.BlockSpec((1, D_MODEL), const),
            pl.BlockSpec((D_MODEL, LANES), const),
            pl.BlockSpec((D_MODEL, LANES), const),
            pl.BlockSpec((1, LANES), const),
        ],
        out_specs=[
            pl.BlockSpec((tm, D_MODEL), row),
            pl.BlockSpec((tm, D_MODEL), row),
            pl.BlockSpec((tm, LANES), row),
        ],
        out_shape=[
            jax.ShapeDtypeStruct((n, D_MODEL), F32),
            jax.ShapeDtypeStruct((n, D_MODEL), F32),
            jax.ShapeDtypeStruct((n, LANES), F32),
        ],
        scratch_shapes=[pltpu.VMEM((tm, D_MODEL), BF16)],
        compiler_params=_cparams(("arbitrary", "arbitrary")),
        name=name,
    )(o_rnn, o_fox, x, g1, g2, w_out, g_moe, rw_hi, rw_lo, rb)


def _route_kernel(lgp_ref, lgs_ref, ei_ref, gate_ref, cnt_ref, carry_scr):
    t = pl.program_id(0)

    @pl.when(t == 0)
    def _():
        carry_scr[...] = jnp.zeros(carry_scr.shape, F32)

    lane = lax.broadcasted_iota(I32, (TM, LANES), 1)
    lane_f = lane.astype(F32)
    work = jnp.where(t < P_TILES, lgp_ref[...], lgs_ref[...])
    tops, idxs, hots = [], [], []
    for _ in range(TOP_K):
        m = jnp.max(work, axis=-1, keepdims=True)
        idx_f = jnp.min(jnp.where(work == m, lane_f, float(LANES)), axis=-1, keepdims=True)
        hot = lane_f == idx_f
        work = jnp.where(hot, -jnp.inf, work)
        tops.append(m)
        idxs.append(idx_f.astype(I32))
        hots.append(hot)

    es = [jnp.exp(tv - tops[0]) for tv in tops]
    denom = es[0] + es[1] + es[2] + es[3]
    gate = jnp.zeros((TM, LANES), F32)
    for k in range(TOP_K):
        gate = jnp.where(lane == k, es[k] / denom, gate)
    gate_ref[...] = gate

    multi = jnp.zeros((TM, LANES), F32)
    for k in range(TOP_K):
        multi = jnp.where(hots[k], 1.0, multi)
    r_i = lax.broadcasted_iota(I32, (TM, TM), 0)
    c_i = lax.broadcasted_iota(I32, (TM, TM), 1)
    strict_lower = jnp.where(c_i < r_i, 1.0, 0.0).astype(BF16)
    before = _dot(strict_lower, multi.astype(BF16)) + carry_scr[...]
    ei = jnp.zeros((TM, LANES), I32)
    for k in range(TOP_K):
        rank = jnp.sum(jnp.where(hots[k], before, 0.0), axis=-1, keepdims=True).astype(I32)
        ei = jnp.where(lane == k, idxs[k], ei)
        ei = jnp.where(lane == TOP_K + k, rank, ei)
    ei_ref[...] = ei
    carry_scr[...] = carry_scr[...] + jnp.sum(multi, axis=0, keepdims=True)
    cnt_ref[...] = carry_scr[...].astype(I32)


def _prompt_tile(t, *_):
    return (jnp.minimum(t, P_TILES - 1), 0)


def _sample_tile(t, *_):
    return (jnp.maximum(t - P_TILES, 0), 0)


def _route(logits_p, logits_s):
    row = lambda t: (t, 0)
    return pl.pallas_call(
        _route_kernel,
        grid=(N_TILES,),
        in_specs=[pl.BlockSpec((TM, LANES), _prompt_tile), pl.BlockSpec((TM, LANES), _sample_tile)],
        out_specs=[pl.BlockSpec((TM, LANES), row), pl.BlockSpec((TM, LANES), row),
                   pl.BlockSpec((1, LANES), lambda t: (0, 0))],
        out_shape=[jax.ShapeDtypeStruct((N_TOK, LANES), I32), jax.ShapeDtypeStruct((N_TOK, LANES), F32),
                   jax.ShapeDtypeStruct((1, LANES), I32)],
        scratch_shapes=[pltpu.VMEM((1, LANES), F32)],
        compiler_params=_cparams(("arbitrary",)),
        name="route",
    )(logits_p, logits_s)


TILE_ROWS = TM * TOP_K
DMA_GROUP = 8
WAIT_GROUP = 32


def _start_rows(n, make_copy):
    def body(gi, c):
        for u in range(DMA_GROUP):
            make_copy(gi * DMA_GROUP + u).start(priority=u % 2)
        return c
    lax.fori_loop(0, n // DMA_GROUP, body, 0)


def _wait_rows(n, copy):
    def body(gi, c):
        for _ in range(WAIT_GROUP):
            copy.wait()
        return c
    lax.fori_loop(0, n // WAIT_GROUP, body, 0)


def _dispatch_kernel(meta_ref, dest_ref, xp_ref, xs_ref, out_hbm, zero_scr, sem, zsem):
    t = pl.program_id(0)

    def scatter(x_ref):
        def row_copy(a):
            tok = lax.shift_right_logical(a, 2)
            return pltpu.make_async_copy(x_ref.at[pl.ds(tok, 1)], out_hbm.at[pl.ds(dest_ref[a], 1)], sem)
        _start_rows(TILE_ROWS, row_copy)

    @pl.when(t < P_TILES)
    def _():
        scatter(xp_ref)

    @pl.when(t >= P_TILES)
    def _():
        scatter(xs_ref)

    @pl.when(t == 0)
    def _():
        zero_scr[...] = jnp.zeros(zero_scr.shape, F32)

        def fill(act):
            def per_row(r, c):
                act(pltpu.make_async_copy(zero_scr.at[pl.ds(0, 1)], out_hbm.at[pl.ds(r, 1)], zsem))
                return c

            def per_expert(e, c):
                lax.fori_loop(meta_ref[N_EXPERTS + e] + meta_ref[e], meta_ref[2 * N_EXPERTS + e], per_row, 0)
                return c
            lax.fori_loop(0, N_EXPERTS, per_expert, 0)

            def per_block(g, c):
                rows = pl.ds(pl.multiple_of(g * EXP_BLK, EXP_BLK), EXP_BLK)
                act(pltpu.make_async_copy(zero_scr, out_hbm.at[rows], zsem))
                return c
            lax.fori_loop(meta_ref[3 * N_EXPERTS], N_BLOCKS, per_block, 0)

        fill(lambda cp: cp.start())
        fill(lambda cp: cp.wait())

    _wait_rows(TILE_ROWS, pltpu.make_async_copy(xp_ref.at[pl.ds(0, 1)], out_hbm.at[pl.ds(0, 1)], sem))


def _dispatch(meta, dest_flat, xn_p, xn_s):
    return pl.pallas_call(
        _dispatch_kernel,
        grid_spec=pltpu.PrefetchScalarGridSpec(
            num_scalar_prefetch=1,
            grid=(N_TILES,),
            in_specs=[
                pl.BlockSpec((TILE_ROWS,), lambda t, meta: (t,), memory_space=pltpu.SMEM),
                pl.BlockSpec((TM, D_MODEL), _prompt_tile),
                pl.BlockSpec((TM, D_MODEL), _sample_tile),
            ],
            out_specs=pl.BlockSpec(memory_space=pl.ANY),
            scratch_shapes=[pltpu.VMEM((EXP_BLK, D_MODEL), F32), pltpu.SemaphoreType.DMA(()),
                            pltpu.SemaphoreType.DMA(())],
        ),
        out_shape=jax.ShapeDtypeStruct((N_PAD, D_MODEL), F32),
        compiler_params=_cparams(("arbitrary",)),
        name="dispatch",
    )(meta, dest_flat, xn_p, xn_s)


EX_SUBS = 4
EX_ROWS = EX_SUBS * EXP_BLK
EX_TF = 256
EX_NF = D_FF // EX_TF
EX_TN = 512
N_ITEMS = (N_BLOCKS + (EX_SUBS - 1) * N_EXPERTS) // EX_SUBS
EXPERTS_VMEM_LIMIT = 56 * 1024 * 1024


def _experts_kernel(ie_ref, ib_ref, ins_ref, ni_ref, nu_ref,
                    x0_ref, x1_ref, x2_ref, x3_ref, wg_ref, wu_ref, bg_ref, bu_ref, wd_ref, bd_ref, y_hbm,
                    xb_scr, acc_scr, wgb_scr, wub_scr, wdb_scr, osem, zsem):
    del ie_ref
    w = pl.program_id(0)
    f = pl.program_id(1)
    n_items = ni_ref[0]
    x_refs = (x0_ref, x1_ref, x2_ref, x3_ref)

    def block_rows(blk):
        return pl.ds(pl.multiple_of(blk * EXP_BLK, EXP_BLK), EXP_BLK)

    def for_item_blocks(item, act):
        for sb in range(EX_SUBS):
            @pl.when(sb < ins_ref[item])
            def _(sb=sb):
                act(pltpu.make_async_copy(acc_scr.at[pl.ds(sb * EXP_BLK, EXP_BLK)],
                                          y_hbm.at[block_rows(ib_ref[item] + sb)], osem))

    @pl.when(jnp.logical_and(w == 0, f == 0))
    def _():
        acc_scr[0:EXP_BLK, :] = jnp.zeros((EXP_BLK, D_MODEL), F32)

        def tail(act):
            def body(g, c):
                act(pltpu.make_async_copy(acc_scr.at[pl.ds(0, EXP_BLK)], y_hbm.at[block_rows(g)], zsem))
                return c
            lax.fori_loop(nu_ref[0], N_BLOCKS, body, 0)
        tail(lambda cp: cp.start())
        tail(lambda cp: cp.wait())

    @pl.when(w < n_items)
    def _():
        nsub = ins_ref[w]

        @pl.when(f == 0)
        def _():
            for sb in range(EX_SUBS):
                @pl.when(sb < nsub)
                def _(sb=sb):
                    xb_scr[sb * EXP_BLK:(sb + 1) * EXP_BLK, :] = x_refs[sb][...].astype(BF16)

        for n in range(1, EX_SUBS + 1):
            @pl.when(nsub == n)
            def _(n=n):
                m = n * EXP_BLK
                wgb_scr[...] = wg_ref[0, 0].astype(BF16)
                wub_scr[...] = wu_ref[0, 0].astype(BF16)
                wdb_scr[...] = wd_ref[0, 0].astype(BF16)
                xb = xb_scr[0:m, :]
                gg = jnp.minimum(_dot(xb, wgb_scr[...]) + bg_ref[0], SWIGLU_LIMIT)
                uu = jnp.clip(_dot(xb, wub_scr[...]) + bu_ref[0], -SWIGLU_LIMIT, SWIGLU_LIMIT)
                hb = ((uu + 1.0) * (gg * _sigmoid(SWIGLU_ALPHA * gg))).astype(BF16)

                @pl.when(jnp.logical_and(f == 0, w > 0))
                def _():
                    for_item_blocks(w - 1, lambda cp: cp.wait())

                for c in range(D_MODEL // EX_TN):
                    cols = slice(c * EX_TN, (c + 1) * EX_TN)
                    part = _dot(hb, wdb_scr[:, cols])
                    base = jnp.where(f == 0, jnp.broadcast_to(bd_ref[0][:, cols], (m, EX_TN)), acc_scr[0:m, cols])
                    acc_scr[0:m, cols] = base + part

        @pl.when(f == EX_NF - 1)
        def _():
            for_item_blocks(w, lambda cp: cp.start())

    @pl.when(jnp.logical_and(w == N_ITEMS - 1, f == EX_NF - 1))
    def _():
        for_item_blocks(n_items - 1, lambda cp: cp.wait())


def _experts(item_e, item_blk, item_nsub, n_items, n_used, x_sorted, w_gu, b_gu, w_down, b_down):
    def item(w, ni):
        return jnp.minimum(w, ni[0] - 1)

    def fsel(w, f, ni):
        return jnp.where(w < ni[0], f, EX_NF - 1)

    def x_spec(sb):
        return pl.BlockSpec(
            (EXP_BLK, D_MODEL),
            lambda w, f, ie, ib, ins, ni, nu: (ib[item(w, ni)] + jnp.minimum(sb, ins[item(w, ni)] - 1), 0))

    return pl.pallas_call(
        _experts_kernel,
        grid_spec=pltpu.PrefetchScalarGridSpec(
            num_scalar_prefetch=5,
            grid=(N_ITEMS, EX_NF),
            in_specs=[
                x_spec(0), x_spec(1), x_spec(2), x_spec(3),
                pl.BlockSpec((1, 1, D_MODEL, EX_TF),
                             lambda w, f, ie, ib, ins, ni, nu: (0, ie[item(w, ni)], 0, fsel(w, f, ni))),
                pl.BlockSpec((1, 1, D_MODEL, EX_TF),
                             lambda w, f, ie, ib, ins, ni, nu: (0, ie[item(w, ni)], 0, EX_NF + fsel(w, f, ni))),
                pl.BlockSpec((1, 1, EX_TF), lambda w, f, ie, ib, ins, ni, nu: (ie[item(w, ni)], 0, fsel(w, f, ni))),
                pl.BlockSpec((1, 1, EX_TF),
                             lambda w, f, ie, ib, ins, ni, nu: (ie[item(w, ni)], 0, EX_NF + fsel(w, f, ni))),
                pl.BlockSpec((1, 1, EX_TF, D_MODEL),
                             lambda w, f, ie, ib, ins, ni, nu: (0, ie[item(w, ni)], fsel(w, f, ni), 0)),
                pl.BlockSpec((1, 1, D_MODEL), lambda w, f, ie, ib, ins, ni, nu: (ie[item(w, ni)], 0, 0)),
            ],
            out_specs=pl.BlockSpec(memory_space=pl.ANY),
            scratch_shapes=[
                pltpu.VMEM((EX_ROWS, D_MODEL), BF16),
                pltpu.VMEM((EX_ROWS, D_MODEL), F32),
                pltpu.VMEM((D_MODEL, EX_TF), BF16),
                pltpu.VMEM((D_MODEL, EX_TF), BF16),
                pltpu.VMEM((EX_TF, D_MODEL), BF16),
                pltpu.SemaphoreType.DMA(()),
                pltpu.SemaphoreType.DMA(()),
            ],
        ),
        out_shape=jax.ShapeDtypeStruct((N_PAD, D_MODEL), F32),
        compiler_params=pltpu.CompilerParams(dimension_semantics=("arbitrary", "arbitrary"),
                                             vmem_limit_bytes=EXPERTS_VMEM_LIMIT),
        name="experts",
    )(item_e, item_blk, item_nsub, n_items, n_used, x_sorted, x_sorted, x_sorted, x_sorted,
      w_gu, w_gu, b_gu, b_gu, w_down, b_down)


def _combine_kernel(dcur_ref, dnext_ref, y_hbm, hp_ref, hs_ref, gate_ref, op_ref, os_ref, buf, sems):
    t = pl.program_id(0)
    slot = jnp.bitwise_and(t, 1)

    def row_copy(d_ref, sl):
        def make(a):
            tok = lax.shift_right_logical(a, 2)
            k = jnp.bitwise_and(a, TOP_K - 1)
            return pltpu.make_async_copy(y_hbm.at[pl.ds(d_ref[a], 1)], buf.at[sl, k, pl.ds(tok, 1)], sems.at[sl])
        return make

    @pl.when(t == 0)
    def _():
        _start_rows(TILE_ROWS, row_copy(dcur_ref, 0))

    @pl.when(t + 1 < N_TILES)
    def _():
        _start_rows(TILE_ROWS, row_copy(dnext_ref, 1 - slot))

    _wait_rows(TILE_ROWS, pltpu.make_async_copy(y_hbm.at[pl.ds(0, 1)], buf.at[slot, 0, pl.ds(0, 1)], sems.at[slot]))
    gate = gate_ref[...]
    moe = gate[:, 0:1] * buf[slot, 0]
    for k in range(1, TOP_K):
        moe = moe + gate[:, k:k + 1] * buf[slot, k]

    @pl.when(t < P_TILES)
    def _():
        op_ref[...] = hp_ref[...] + moe

    @pl.when(t >= P_TILES)
    def _():
        os_ref[...] = hs_ref[...] + moe


def _combine(dest_flat, y_sorted, h_p, h_s, gate):
    return pl.pallas_call(
        _combine_kernel,
        grid=(N_TILES,),
        in_specs=[
            pl.BlockSpec((TILE_ROWS,), lambda t: (t,), memory_space=pltpu.SMEM),
            pl.BlockSpec((TILE_ROWS,), lambda t: (jnp.minimum(t + 1, N_TILES - 1),), memory_space=pltpu.SMEM),
            pl.BlockSpec(memory_space=pl.ANY),
            pl.BlockSpec((TM, D_MODEL), _prompt_tile),
            pl.BlockSpec((TM, D_MODEL), _sample_tile),
            pl.BlockSpec((TM, LANES), lambda t: (t, 0)),
        ],
        out_specs=[pl.BlockSpec((TM, D_MODEL), _prompt_tile), pl.BlockSpec((TM, D_MODEL), _sample_tile)],
        out_shape=[jax.ShapeDtypeStruct((N_P, D_MODEL), F32), jax.ShapeDtypeStruct((N_S, D_MODEL), F32)],
        scratch_shapes=[pltpu.VMEM((2, TOP_K, TM, D_MODEL), F32), pltpu.SemaphoreType.DMA((2,))],
        compiler_params=_cparams(("arbitrary",)),
        name="combine",
    )(dest_flat, dest_flat, y_sorted, h_p, h_s, gate)


PL_TN = 512
PL_NJ = D_MODEL // PL_TN


def _ple_final_kernel(h_ref, p_ref, wp_ref, gp_ref, gg_ref, wg_ref, gf_ref, y_ref, hn_scr, e_scr, h3_scr):
    j = pl.program_id(1)

    @pl.when(j == 0)
    def _():
        hn_scr[...] = _rms(h_ref[...], gg_ref[...]).astype(BF16)
        e_scr[...] = _rms(_dot(p_ref[...].astype(BF16), wp_ref[...]), gp_ref[...])

    gate = _sigmoid(_dot(hn_scr[...], wg_ref[...]))
    for jj in range(PL_NJ):
        @pl.when(j == jj)
        def _(jj=jj):
            sl = slice(jj * PL_TN, (jj + 1) * PL_TN)
            h3_scr[:, sl] = h_ref[:, sl] + gate * e_scr[:, sl]

    @pl.when(j == PL_NJ - 1)
    def _():
        y_ref[...] = _rms(h3_scr[...], gf_ref[...])


def _ple_final(h2, p, tm, w_proj, g_proj, g_gate, w_gate, g_final, name):
    n = h2.shape[0]
    row = lambda i, j: (i, 0)
    const = lambda i, j: (0, 0)
    return pl.pallas_call(
        _ple_final_kernel,
        grid=(n // tm, PL_NJ),
        in_specs=[
            pl.BlockSpec((tm, D_MODEL), row),
            pl.BlockSpec((tm, D_PLE), row),
            pl.BlockSpec((D_PLE, D_MODEL), const),
            pl.BlockSpec((1, D_MODEL), const),
            pl.BlockSpec((1, D_MODEL), const),
            pl.BlockSpec((D_MODEL, PL_TN), lambda i, j: (0, j)),
            pl.BlockSpec((1, D_MODEL), const),
        ],
        out_specs=pl.BlockSpec((tm, D_MODEL), row),
        out_shape=jax.ShapeDtypeStruct((n, D_MODEL), F32),
        scratch_shapes=[pltpu.VMEM((tm, D_MODEL), BF16), pltpu.VMEM((tm, D_MODEL), F32),
                        pltpu.VMEM((tm, D_MODEL), F32)],
        compiler_params=_cparams(("arbitrary", "arbitrary")),
        name=name,
    )(h2, p, w_proj, g_proj, g_gate, w_gate, g_final)


def kernel(x_prompt, x_sample, cache_k, cache_v, cache_logf, state_conv, state_rglru, p_prompt, p_sample,
           g_mix, w_in, b_fgate, conv_w, conv_b, rg_w_a, rg_b_a, rg_w_i, rg_b_i, rg_lambda,
           g_out_rnn, g_out_fox, w_out, g_moe, router_w, router_b, w_gu, b_gu, w_down, b_down,
           w_ple_proj, g_ple_proj, g_ple_gate, w_ple_gate, g_final):
    assert g_mix.shape[0] == 1, "one layer"
    row = lambda v: v.reshape(1, -1)
    lane_pad = lambda a: jnp.pad(a, ((0, 0), (0, LANES - a.shape[1])))

    xp = x_prompt.reshape(N_P, D_MODEL)
    xs = x_sample.reshape(N_S, D_MODEL)

    w_main = w_in[0][:, :D_MAIN].astype(BF16)
    w_f = lane_pad(w_in[0][:, D_MAIN:]).astype(BF16)
    b_f = lane_pad(row(b_fgate[0]))
    proj_p, lf_p = _in_proj(xp, TM_IN, row(g_mix[0]), w_main, w_f, b_f, "in_proj_prompt")
    proj_s, lf_s = _in_proj(xs, N_S, row(g_mix[0]), w_main, w_f, b_f, "in_proj_sample")
    logf_p = lf_p[:, :FOX_HEADS].reshape(BATCH, SEQ, FOX_HEADS)
    logf_s = lf_s[:, :FOX_HEADS].reshape(DEC_BATCH, DEC_SEQ, FOX_HEADS)
    k_p, v_p = proj_p[:, COL_K:COL_V], proj_p[:, COL_V:]
    q_s, k_s, v_s = proj_s[:, COL_Q:COL_K], proj_s[:, COL_K:COL_V], proj_s[:, COL_V:]

    rg = (conv_w[0], row(conv_b[0]), rg_w_a[0].astype(BF16), row(rg_b_a[0]), rg_w_i[0].astype(BF16),
          row(rg_b_i[0]), row(rg_lambda[0]))
    o_rnn_p, conv_p, hl_p = _rglru(proj_p, 0, BATCH, SEQ // TM, TM,
                                   jnp.zeros((BATCH, CONV_W - 1, D_RNN), F32), jnp.zeros((BATCH, 1, D_RNN), F32),
                                   *rg, name="rglru_prompt")
    o_rnn_s, conv_s, hl_s = _rglru(proj_s, 0, DEC_BATCH, 1, DEC_SEQ,
                                   state_conv[0], state_rglru[0].reshape(DEC_BATCH, 1, D_RNN),
                                   *rg, name="rglru_sample")

    c_p = _cumsum_rows(logf_p.transpose(1, 0, 2).reshape(SEQ, BATCH * FOX_HEADS), "cumsum_prompt")
    c_pT = c_p.T
    o_fox_p = _fox_prompt(proj_p, c_pT[:, :, None], c_pT[:, None, :])

    lf_all = jnp.concatenate([cache_logf[0], logf_s], axis=1)
    c_s = _cumsum_rows(lf_all.transpose(1, 0, 2).reshape(PAST_LEN + DEC_SEQ, DEC_BATCH * FOX_HEADS),
                       "cumsum_sample")
    c_bth = c_s.reshape(PAST_LEN + DEC_SEQ, DEC_BATCH, FOX_HEADS).transpose(1, 0, 2)
    ck = c_bth.reshape(DEC_BATCH, 1, (PAST_LEN + DEC_SEQ) * FOX_HEADS)
    cq = c_bth[:, PAST_LEN:].reshape(DEC_BATCH, FS_ROWS, 1)
    per_req = lambda a: a.reshape(DEC_BATCH, FS_ROWS, FOX_HEAD_DIM)
    o_fox_s = _fox_sample(per_req(q_s), cache_k, cache_v, per_req(k_s), per_req(v_s), cq, ck)
    o_fox_s = o_fox_s.reshape(N_S, D_FOX)

    rw = lane_pad(router_w[0])
    rw_hi = rw.astype(BF16)
    rw_lo = (rw - rw_hi.astype(F32)).astype(BF16)
    rb = jnp.concatenate([row(router_b[0]), jnp.full((1, LANES - N_EXPERTS), NEG, F32)], axis=1)
    op_w = (row(g_out_rnn[0]), row(g_out_fox[0]), w_out[0].astype(BF16), row(g_moe[0]), rw_hi, rw_lo, rb)
    h1_p, xn_p, logits_p = _out_proj(o_rnn_p, o_fox_p, xp, TM_OUT, *op_w, name="out_proj_prompt")
    h1_s, xn_s, logits_s = _out_proj(o_rnn_s, o_fox_s, xs, N_S, *op_w, name="out_proj_sample")

    ei, gate, cnt = _route(logits_p, logits_s)
    counts = cnt[0, :N_EXPERTS]
    padded = (counts + EXP_BLK - 1) // EXP_BLK * EXP_BLK
    pad_end = jnp.cumsum(padded)
    pad_start = pad_end - padded
    dest = (pad_start[ei[:, :TOP_K]] + ei[:, TOP_K:2 * TOP_K]).reshape(N_ROWS)
    n_used = (pad_end[-1] // EXP_BLK).reshape(1).astype(I32)
    meta = jnp.concatenate([counts, pad_start, pad_end, n_used]).astype(I32)
    x_sorted = _dispatch(meta, dest, xn_p, xn_s)
    blocks_e = padded // EXP_BLK
    items_e = (blocks_e + EX_SUBS - 1) // EX_SUBS
    item_end = jnp.cumsum(items_e)
    item_first = item_end - items_e
    item_id = jnp.arange(N_ITEMS, dtype=I32)
    item_e = jnp.minimum(jnp.sum((item_end[None, :] <= item_id[:, None]).astype(I32), axis=1), N_EXPERTS - 1)
    item_run = item_id - item_first[item_e]
    item_blk = jnp.clip(pad_start[item_e] // EXP_BLK + EX_SUBS * item_run, 0, N_BLOCKS - 1).astype(I32)
    item_nsub = jnp.clip(blocks_e[item_e] - EX_SUBS * item_run, 1, EX_SUBS).astype(I32)
    n_items = item_end[-1].reshape(1).astype(I32)
    y_sorted = _experts(item_e.astype(I32), item_blk, item_nsub, n_items, n_used, x_sorted,
                        w_gu, b_gu[0][:, None, :], w_down, b_down[0][:, None, :])
    h2_p, h2_s = _combine(dest, y_sorted, h1_p, h1_s, gate)

    pf_w = (w_ple_proj[0].astype(BF16), row(g_ple_proj[0]), row(g_ple_gate[0]), w_ple_gate[0].astype(BF16),
            row(g_final))
    y_p = _ple_final(h2_p, p_prompt[0].reshape(N_P, D_PLE), TM_OUT, *pf_w, name="ple_final_prompt")
    y_s = _ple_final(h2_s, p_sample[0].reshape(N_S, D_PLE), N_S, *pf_w, name="ple_final_sample")

    shp = (1, BATCH, SEQ, FOX_HEADS, FOX_HEAD_DIM)
    shs = (1, DEC_BATCH, DEC_SEQ, FOX_HEADS, FOX_HEAD_DIM)
    return (y_p.reshape(BATCH, SEQ, D_MODEL), y_s.reshape(DEC_BATCH, DEC_SEQ, D_MODEL),
            k_p.reshape(shp), v_p.reshape(shp), logf_p[None],
            conv_p[None], hl_p.reshape(1, BATCH, D_RNN),
            k_s.reshape(shs), v_s.reshape(shs), logf_s[None],
            conv_s[None], hl_s.reshape(1, DEC_BATCH, D_RNN))
```

```python
import functools

import jax
import jax.numpy as jnp
from jax import lax
from jax.experimental import pallas as pl
from jax.experimental.pallas import tpu as pltpu

F32 = jnp.float32
BF16 = jnp.bfloat16
I32 = jnp.int32

D_MODEL = 2048
BATCH = 4
SEQ = 2048
DEC_BATCH = 16
DEC_SEQ = 16
PAST_LEN = 4096
D_RNN = 1024
RG_BLOCKS = 8
RG_BLOCK_W = 128
CONV_W = 4
RG_C = 8.0
FOX_HEADS = 8
FOX_HEAD_DIM = 128
D_FOX = 1024
D_MAIN = 2 * D_RNN + 3 * D_FOX
N_EXPERTS = 32
TOP_K = 4
D_FF = 2048
SWIGLU_LIMIT = 7.0
SWIGLU_ALPHA = 1.702
D_PLE = 256
EPS = 1e-6

LANES = 128
N_P = BATCH * SEQ
N_S = DEC_BATCH * DEC_SEQ
N_TOK = N_P + N_S
TM = 256
TM_IN = 1024
TM_OUT = 512
N_TILES = N_TOK // TM
P_TILES = N_P // TM
N_ROWS = N_TOK * TOP_K
EXP_BLK = 128
N_BLOCKS = (N_ROWS + N_EXPERTS * (EXP_BLK - 1) + EXP_BLK - 1) // EXP_BLK
N_PAD = N_BLOCKS * EXP_BLK
X_ROWS = N_PAD + EXP_BLK
NEG = -1e30
VMEM_LIMIT = 48 * 1024 * 1024


def _cparams(sem):
    return pltpu.CompilerParams(dimension_semantics=sem, vmem_limit_bytes=VMEM_LIMIT)


def _rms(x, g):
    return x * lax.rsqrt(jnp.mean(x * x, axis=-1, keepdims=True) + EPS) * g


def _log_sigmoid(z):
    return jnp.minimum(z, 0.0) - jnp.log1p(jnp.exp(-jnp.abs(z)))


def _sigmoid(z):
    return 1.0 / (1.0 + jnp.exp(-z))


def _expm1(x):
    u = jnp.exp(x)
    degenerate = jnp.logical_or(u == 1.0, u == 0.0)
    val = (u - 1.0) * x / jnp.log(jnp.where(degenerate, 0.5, u))
    return jnp.where(u == 1.0, x, jnp.where(u == 0.0, -1.0, val))


def _gelu_tanh(y):
    return 0.5 * y * (1.0 + jnp.tanh(0.7978845608028654 * (y + 0.044715 * (y * y * y))))


def _dot(a, b):
    return jnp.dot(a, b, preferred_element_type=F32)


def _dot_nt(a, b):
    return lax.dot_general(a, b, (((1,), (1,)), ((), ())), preferred_element_type=F32)


IN_TN = 512
IN_NJ = D_MAIN // IN_TN
COL_Q = 2 * D_RNN
COL_K = COL_Q + D_FOX
COL_V = COL_K + D_FOX


def _in_proj_kernel(x_ref, g_ref, w_ref, wf_ref, bf_ref, proj_ref, logf_ref, xn_scr):
    @pl.when(pl.program_id(1) == 0)
    def _():
        xb = _rms(x_ref[...], g_ref[...]).astype(BF16)
        xn_scr[...] = xb
        logf_ref[...] = _log_sigmoid(_dot(xb, wf_ref[...]) + bf_ref[...])

    proj_ref[...] = _dot(xn_scr[...], w_ref[...])


def _in_proj(x, tm, g_mix, w_main, w_f, b_f, name):
    n = x.shape[0]
    row = lambda i, j: (i, 0)
    const = lambda i, j: (0, 0)
    return pl.pallas_call(
        _in_proj_kernel,
        grid=(n // tm, IN_NJ),
        in_specs=[
            pl.BlockSpec((tm, D_MODEL), row),
            pl.BlockSpec((1, D_MODEL), const),
            pl.BlockSpec((D_MODEL, IN_TN), lambda i, j: (0, j)),
            pl.BlockSpec((D_MODEL, LANES), const),
            pl.BlockSpec((1, LANES), const),
        ],
        out_specs=[pl.BlockSpec((tm, IN_TN), lambda i, j: (i, j)), pl.BlockSpec((tm, LANES), row)],
        out_shape=[jax.ShapeDtypeStruct((n, D_MAIN), F32), jax.ShapeDtypeStruct((n, LANES), F32)],
        scratch_shapes=[pltpu.VMEM((tm, D_MODEL), BF16)],
        compiler_params=_cparams(("arbitrary", "arbitrary")),
        name=name,
    )(x, g_mix, w_main, w_f, b_f)


def _prefix_steps(n):
    s = 1
    while s < n:
        yield s
        s *= 2


def _rglru_kernel(x_ref, y_ref, past_ref, h0_ref, cw_ref, cb_ref, wa_ref, ba_ref, wi_ref, bi_ref, lam_ref,
                  o_ref, conv_ref, hlast_ref, xp_scr, h_scr, *, tt, nt):
    t = pl.program_id(1)
    pad = 8

    @pl.when(t == 0)
    def _():
        xp_scr[pad - (CONV_W - 1):pad, :] = past_ref[0]
        h_scr[...] = h0_ref[0]

    xp_scr[pad:pad + tt, :] = x_ref[...]
    xc = cb_ref[...] + xp_scr[pad - 3:pad - 3 + tt, :] * cw_ref[0:1, :]
    for jw in range(1, CONV_W):
        xc = xc + xp_scr[pad - 3 + jw:pad - 3 + jw + tt, :] * cw_ref[jw:jw + 1, :]

    r_parts, i_parts = [], []
    for n in range(RG_BLOCKS):
        xb = xc[:, n * RG_BLOCK_W:(n + 1) * RG_BLOCK_W].astype(BF16)
        r_parts.append(_dot(xb, wa_ref[n]))
        i_parts.append(_dot(xb, wi_ref[n]))
    r = _sigmoid(jnp.concatenate(r_parts, axis=-1) + ba_ref[...])
    ig = _sigmoid(jnp.concatenate(i_parts, axis=-1) + bi_ref[...])
    log_a = RG_C * r * _log_sigmoid(lam_ref[...])
    a = jnp.exp(log_a)
    b = jnp.sqrt(-_expm1(2.0 * log_a)) * (ig * xc)

    rowi = lax.broadcasted_iota(I32, (tt, D_RNN), 0)
    for s in _prefix_steps(tt):
        keep = rowi >= s
        a_sh = jnp.where(keep, pltpu.roll(a, s, 0), 1.0)
        b_sh = jnp.where(keep, pltpu.roll(b, s, 0), 0.0)
        b = a * b_sh + b
        a = a * a_sh
    h = a * h_scr[...] + b
    h_scr[...] = h[tt - 1:tt, :]
    o_ref[...] = h * _gelu_tanh(y_ref[...])

    tail = xp_scr[pad + tt - (CONV_W - 1):pad + tt, :]
    xp_scr[pad - (CONV_W - 1):pad, :] = tail

    @pl.when(t == nt - 1)
    def _():
        conv_ref[0] = tail
        hlast_ref[0] = h[tt - 1:tt, :]


def _rglru(xy, row0, nb, nt, tt, conv_past, h0, cw, cb, wa, ba, wi, bi, lam, name):
    blk0 = row0 // tt
    vec = lambda b, t: (0, 0)
    w3 = lambda b, t: (0, 0, 0)
    kern = functools.partial(_rglru_kernel, tt=tt, nt=nt)
    return pl.pallas_call(
        kern,
        grid=(nb, nt),
        in_specs=[
            pl.BlockSpec((tt, D_RNN), lambda b, t: (blk0 + b * nt + t, 0)),
            pl.BlockSpec((tt, D_RNN), lambda b, t: (blk0 + b * nt + t, 1)),
            pl.BlockSpec((1, CONV_W - 1, D_RNN), lambda b, t: (b, 0, 0)),
            pl.BlockSpec((1, 1, D_RNN), lambda b, t: (b, 0, 0)),
            pl.BlockSpec((CONV_W, D_RNN), vec),
            pl.BlockSpec((1, D_RNN), vec),
            pl.BlockSpec((RG_BLOCKS, RG_BLOCK_W, RG_BLOCK_W), w3),
            pl.BlockSpec((1, D_RNN), vec),
            pl.BlockSpec((RG_BLOCKS, RG_BLOCK_W, RG_BLOCK_W), w3),
            pl.BlockSpec((1, D_RNN), vec),
            pl.BlockSpec((1, D_RNN), vec),
        ],
        out_specs=[
            pl.BlockSpec((tt, D_RNN), lambda b, t: (b * nt + t, 0)),
            pl.BlockSpec((1, CONV_W - 1, D_RNN), lambda b, t: (b, 0, 0)),
            pl.BlockSpec((1, 1, D_RNN), lambda b, t: (b, 0, 0)),
        ],
        out_shape=[
            jax.ShapeDtypeStruct((nb * nt * tt, D_RNN), F32),
            jax.ShapeDtypeStruct((nb, CONV_W - 1, D_RNN), F32),
            jax.ShapeDtypeStruct((nb, 1, D_RNN), F32),
        ],
        scratch_shapes=[pltpu.VMEM((tt + 8, D_RNN), F32), pltpu.VMEM((1, D_RNN), F32)],
        compiler_params=_cparams(("arbitrary", "arbitrary")),
        name=name,
    )(xy, xy, conv_past, h0, cw, cb, wa, ba, wi, bi, lam)


def _cumsum_kernel(x_ref, o_ref, *, n):
    x = x_ref[...]
    rowi = lax.broadcasted_iota(I32, x.shape, 0)
    for s in _prefix_steps(n):
        x = x + jnp.where(rowi >= s, pltpu.roll(x, s, 0), 0.0)
    o_ref[...] = x


def _cumsum_rows(x, name):
    return pl.pallas_call(
        functools.partial(_cumsum_kernel, n=x.shape[0]),
        out_shape=jax.ShapeDtypeStruct(x.shape, F32),
        compiler_params=pltpu.CompilerParams(vmem_limit_bytes=VMEM_LIMIT),
        name=name,
    )(x)


FP_T = 512
FP_NT = SEQ // FP_T
FOX_SCALE = FOX_HEAD_DIM ** -0.5


FP_HB = 4
FP_W = FP_HB * FOX_HEAD_DIM
FP_GROUPS = FOX_HEADS // FP_HB


def _fox_prompt_kernel(q_ref, k_ref, vt_ref, cq_ref, ck_ref, o_ref, m_scr, l_scr, acc_scr):
    qi = pl.program_id(1)
    ki = pl.program_id(2)

    @pl.when(ki == 0)
    def _():
        m_scr[...] = jnp.full(m_scr.shape, NEG, F32)
        l_scr[...] = jnp.zeros(l_scr.shape, F32)
        acc_scr[...] = jnp.zeros(acc_scr.shape, F32)

    def update(diagonal):
        for hh in range(FP_HB):
            cols = slice(hh * FOX_HEAD_DIM, (hh + 1) * FOX_HEAD_DIM)
            st = _dot_nt(k_ref[:, cols].astype(BF16), q_ref[:, cols].astype(BF16)) * FOX_SCALE
            st = st + (cq_ref[hh] - ck_ref[hh])
            if diagonal:
                key = lax.broadcasted_iota(I32, (FP_T, FP_T), 0)
                qry = lax.broadcasted_iota(I32, (FP_T, FP_T), 1)
                st = jnp.where(key <= qry, st, NEG)
            m_old = m_scr[hh]
            m_new = jnp.maximum(m_old, jnp.max(st, axis=0, keepdims=True))
            alpha = jnp.exp(m_old - m_new)
            p = jnp.exp(st - m_new)
            l_scr[hh] = alpha * l_scr[hh] + jnp.sum(p, axis=0, keepdims=True)
            acc_scr[hh] = alpha * acc_scr[hh] + _dot(vt_ref[cols, :].astype(BF16), p.astype(BF16))
            m_scr[hh] = m_new

    @pl.when(ki < qi)
    def _():
        update(False)

    @pl.when(ki == qi)
    def _():
        update(True)

    @pl.when(ki == FP_NT - 1)
    def _():
        for hh in range(FP_HB):
            cols = slice(hh * FOX_HEAD_DIM, (hh + 1) * FOX_HEAD_DIM)
            o_ref[:, cols] = (acc_scr[hh] / l_scr[hh]).T


def _fox_prompt(proj_p, v_t, c_col, c_row):
    def rows(g, blk):
        return (g // FP_GROUPS) * FP_NT + blk

    def colblk(g, col0):
        return col0 // FP_W + g % FP_GROUPS

    seen = lambda qi, ki: jnp.minimum(ki, qi)
    return pl.pallas_call(
        _fox_prompt_kernel,
        grid=(BATCH * FP_GROUPS, FP_NT, FP_NT),
        in_specs=[
            pl.BlockSpec((FP_T, FP_W), lambda g, qi, ki: (rows(g, qi), colblk(g, COL_Q))),
            pl.BlockSpec((FP_T, FP_W), lambda g, qi, ki: (rows(g, seen(qi, ki)), colblk(g, COL_K))),
            pl.BlockSpec((FP_W, FP_T), lambda g, qi, ki: (g % FP_GROUPS, rows(g, seen(qi, ki)))),
            pl.BlockSpec((FP_HB, 1, FP_T), lambda g, qi, ki: (g, 0, qi)),
            pl.BlockSpec((FP_HB, FP_T, 1), lambda g, qi, ki: (g, seen(qi, ki), 0)),
        ],
        out_specs=pl.BlockSpec((FP_T, FP_W), lambda g, qi, ki: (rows(g, qi), g % FP_GROUPS)),
        out_shape=jax.ShapeDtypeStruct((N_P, D_FOX), F32),
        scratch_shapes=[pltpu.VMEM((FP_HB, 1, FP_T), F32), pltpu.VMEM((FP_HB, 1, FP_T), F32),
                        pltpu.VMEM((FP_HB, FOX_HEAD_DIM, FP_T), F32)],
        compiler_params=_cparams(("arbitrary", "arbitrary", "arbitrary")),
        name="fox_prompt",
    )(proj_p, proj_p, v_t, c_row, c_col)


FS_TK = 1024
FS_NT = PAST_LEN // FS_TK
FS_ROWS = DEC_SEQ * FOX_HEADS


FS_COLS = FS_TK * FOX_HEADS


def _fox_sample_kernel(q_ref, kc_ref, vc_ref, kn_ref, vn_ref, cq_ref, ckp_ref, ckn_ref, o_ref,
                       m_scr, l_scr, acc_scr):
    kt = pl.program_id(1)

    @pl.when(kt == 0)
    def _():
        m_scr[...] = jnp.full(m_scr.shape, NEG, F32)
        l_scr[...] = jnp.zeros(l_scr.shape, F32)
        acc_scr[...] = jnp.zeros(acc_scr.shape, F32)

    qb = q_ref[0].astype(BF16)

    def step(k2, v2, ck, causal):
        n = k2.shape[0]
        s = _dot_nt(qb, k2.astype(BF16)) * FOX_SCALE + (cq_ref[0] - ck)
        rowi = lax.broadcasted_iota(I32, (FS_ROWS, n), 0)
        coli = lax.broadcasted_iota(I32, (FS_ROWS, n), 1)
        keep = jnp.bitwise_and(coli, FOX_HEADS - 1) == jnp.bitwise_and(rowi, FOX_HEADS - 1)
        s = jnp.where(keep, s, NEG)
        if causal:
            s = jnp.where(jnp.right_shift(coli, 3) <= jnp.right_shift(rowi, 3), s, NEG)
        m_old = m_scr[...]
        m_new = jnp.maximum(m_old, jnp.max(s, axis=-1, keepdims=True))
        alpha = jnp.exp(m_old - m_new)
        p = jnp.exp(s - m_new)
        l_scr[...] = alpha * l_scr[...] + jnp.sum(p, axis=-1, keepdims=True)
        acc_scr[...] = alpha * acc_scr[...] + _dot(p.astype(BF16), v2.astype(BF16))
        m_scr[...] = m_new

    step(kc_ref[0, 0].reshape(FS_COLS, FOX_HEAD_DIM), vc_ref[0, 0].reshape(FS_COLS, FOX_HEAD_DIM),
         ckp_ref[0], False)

    @pl.when(kt == FS_NT - 1)
    def _():
        step(kn_ref[0], vn_ref[0], ckn_ref[0], True)
        o_ref[0] = acc_scr[...] / l_scr[...]


def _fox_sample(q_s, cache_k, cache_v, k_n, v_n, cq, ck):
    per_req = lambda b, kt: (b, 0, 0)
    cache = lambda b, kt: (0, b, kt, 0, 0)
    return pl.pallas_call(
        _fox_sample_kernel,
        grid=(DEC_BATCH, FS_NT),
        in_specs=[
            pl.BlockSpec((1, FS_ROWS, FOX_HEAD_DIM), per_req),
            pl.BlockSpec((1, 1, FS_TK, FOX_HEADS, FOX_HEAD_DIM), cache),
            pl.BlockSpec((1, 1, FS_TK, FOX_HEADS, FOX_HEAD_DIM), cache),
            pl.BlockSpec((1, FS_ROWS, FOX_HEAD_DIM), per_req),
            pl.BlockSpec((1, FS_ROWS, FOX_HEAD_DIM), per_req),
            pl.BlockSpec((1, FS_ROWS, 1), per_req),
            pl.BlockSpec((1, 1, FS_COLS), lambda b, kt: (b, 0, kt)),
            pl.BlockSpec((1, 1, FS_ROWS), lambda b, kt: (b, 0, PAST_LEN * FOX_HEADS // FS_ROWS)),
        ],
        out_specs=pl.BlockSpec((1, FS_ROWS, FOX_HEAD_DIM), per_req),
        out_shape=jax.ShapeDtypeStruct((DEC_BATCH, FS_ROWS, FOX_HEAD_DIM), F32),
        scratch_shapes=[pltpu.VMEM((FS_ROWS, 1), F32), pltpu.VMEM((FS_ROWS, 1), F32),
                        pltpu.VMEM((FS_ROWS, FOX_HEAD_DIM), F32)],
        compiler_params=_cparams(("arbitrary", "arbitrary")),
        name="fox_sample",
    )(q_s, cache_k, cache_v, k_n, v_n, cq, ck, ck)


OP_TN = 512
OP_NJ = D_MODEL // OP_TN


def _out_proj_kernel(orn_ref, ofx_ref, x_ref, g1_ref, g2_ref, w_ref, gm_ref,
                     rwh_ref, rwl_ref, rb_ref, h_ref, xn_ref, lg_ref, mix_scr):
    j = pl.program_id(1)

    @pl.when(j == 0)
    def _():
        mix_scr[:, :D_RNN] = _rms(orn_ref[...], g1_ref[...]).astype(BF16)
        mix_scr[:, D_RNN:] = _rms(ofx_ref[...], g2_ref[...]).astype(BF16)

    r = _dot(mix_scr[...], w_ref[...])
    for jj in range(OP_NJ):
        @pl.when(j == jj)
        def _(jj=jj):
            h_ref[:, jj * OP_TN:(jj + 1) * OP_TN] = x_ref[:, jj * OP_TN:(jj + 1) * OP_TN] + r

    @pl.when(j == OP_NJ - 1)
    def _():
        xn = _rms(h_ref[...], gm_ref[...])
        xn_ref[...] = xn
        xh = xn.astype(BF16)
        xl = (xn - xh.astype(F32)).astype(BF16)
        lg_ref[...] = (_dot(xh, rwh_ref[...]) + _dot(xl, rwh_ref[...]) + _dot(xh, rwl_ref[...])) + rb_ref[...]


def _out_proj(o_rnn, o_fox, x, tm, g1, g2, w_out, g_moe, rw_hi, rw_lo, rb, name):
    n = x.shape[0]
    row = lambda i, j: (i, 0)
    const = lambda i, j: (0, 0)
    return pl.pallas_call(
        _out_proj_kernel,
        grid=(n // tm, OP_NJ),
        in_specs=[
            pl.BlockSpec((tm, D_RNN), row),
            pl.BlockSpec((tm, D_FOX), row),
            pl.BlockSpec((tm, D_MODEL), row),
            pl.BlockSpec((1, D_RNN), const),
            pl.BlockSpec((1, D_FOX), const),
            pl.BlockSpec((D_MODEL, OP_TN), lambda i, j: (0, j)),
            pl.BlockSpec((1, D_MODEL), const),
            pl.BlockSpec((D_MODEL, LANES), const),
            pl.BlockSpec((D_MODEL, LANES), const),
            pl.BlockSpec((1, LANES), const),
        ],
        out_specs=[
            pl.BlockSpec((tm, D_MODEL), row),
            pl.BlockSpec((tm, D_MODEL), row),
            pl.BlockSpec((tm, LANES), row),
        ],
        out_shape=[
            jax.ShapeDtypeStruct((n, D_MODEL), F32),
            jax.ShapeDtypeStruct((n, D_MODEL), F32),
            jax.ShapeDtypeStruct((n, LANES), F32),
        ],
        scratch_shapes=[pltpu.VMEM((tm, D_MODEL), BF16)],
        compiler_params=_cparams(("arbitrary", "arbitrary")),
        name=name,
    )(o_rnn, o_fox, x, g1, g2, w_out, g_moe, rw_hi, rw_lo, rb)


def _route_kernel(lgp_ref, lgs_ref, ei_ref, gate_ref, cnt_ref, carry_scr):
    t = pl.program_id(0)

    @pl.when(t == 0)
    def _():
        carry_scr[...] = jnp.zeros(carry_scr.shape, F32)

    lane = lax.broadcasted_iota(I32, (TM, LANES), 1)
    lane_f = lane.astype(F32)
    work = jnp.where(t < P_TILES, lgp_ref[...], lgs_ref[...])
    tops, idxs, hots = [], [], []
    for _ in range(TOP_K):
        m = jnp.max(work, axis=-1, keepdims=True)
        idx_f = jnp.min(jnp.where(work == m, lane_f, float(LANES)), axis=-1, keepdims=True)
        hot = lane_f == idx_f
        work = jnp.where(hot, -jnp.inf, work)
        tops.append(m)
        idxs.append(idx_f.astype(I32))
        hots.append(hot)

    es = [jnp.exp(tv - tops[0]) for tv in tops]
    denom = es[0] + es[1] + es[2] + es[3]
    gate = jnp.zeros((TM, LANES), F32)
    for k in range(TOP_K):
        gate = jnp.where(lane == k, es[k] / denom, gate)
    gate_ref[...] = gate

    multi = jnp.zeros((TM, LANES), F32)
    for k in range(TOP_K):
        multi = jnp.where(hots[k], 1.0, multi)
    r_i = lax.broadcasted_iota(I32, (TM, TM), 0)
    c_i = lax.broadcasted_iota(I32, (TM, TM), 1)
    strict_lower = jnp.where(c_i < r_i, 1.0, 0.0).astype(BF16)
    before = _dot(strict_lower, multi.astype(BF16)) + carry_scr[...]
    ei = jnp.zeros((TM, LANES), I32)
    for k in range(TOP_K):
        rank = jnp.sum(jnp.where(hots[k], before, 0.0), axis=-1, keepdims=True).astype(I32)
        ei = jnp.where(lane == k, idxs[k], ei)
        ei = jnp.where(lane == TOP_K + k, rank, ei)
    ei_ref[...] = ei
    carry_scr[...] = carry_scr[...] + jnp.sum(multi, axis=0, keepdims=True)
    cnt_ref[...] = carry_scr[...].astype(I32)


def _prompt_tile(t, *_):
    return (jnp.minimum(t, P_TILES - 1), 0)


def _sample_tile(t, *_):
    return (jnp.maximum(t - P_TILES, 0), 0)


def _route(logits_p, logits_s):
    row = lambda t: (t, 0)
    return pl.pallas_call(
        _route_kernel,
        grid=(N_TILES,),
        in_specs=[pl.BlockSpec((TM, LANES), _prompt_tile), pl.BlockSpec((TM, LANES), _sample_tile)],
        out_specs=[pl.BlockSpec((TM, LANES), row), pl.BlockSpec((TM, LANES), row),
                   pl.BlockSpec((1, LANES), lambda t: (0, 0))],
        out_shape=[jax.ShapeDtypeStruct((N_TOK, LANES), I32), jax.ShapeDtypeStruct((N_TOK, LANES), F32),
                   jax.ShapeDtypeStruct((1, LANES), I32)],
        scratch_shapes=[pltpu.VMEM((1, LANES), F32)],
        compiler_params=_cparams(("arbitrary",)),
        name="route",
    )(logits_p, logits_s)


TILE_ROWS = TM * TOP_K
DMA_GROUP = 8
WAIT_GROUP = 32


def _start_rows(n, make_copy):
    def body(gi, c):
        for u in range(DMA_GROUP):
            make_copy(gi * DMA_GROUP + u).start(priority=u % 2)
        return c
    lax.fori_loop(0, n // DMA_GROUP, body, 0)


def _wait_rows(n, copy):
    def body(gi, c):
        for _ in range(WAIT_GROUP):
            copy.wait()
        return c
    lax.fori_loop(0, n // WAIT_GROUP, body, 0)


def _dispatch_kernel(meta_ref, dest_ref, xp_ref, xs_ref, out_hbm, zero_scr, sem, zsem):
    t = pl.program_id(0)

    def scatter(x_ref):
        def row_copy(a):
            tok = lax.shift_right_logical(a, 2)
            return pltpu.make_async_copy(x_ref.at[pl.ds(tok, 1)], out_hbm.at[pl.ds(dest_ref[a], 1)], sem)
        _start_rows(TILE_ROWS, row_copy)

    @pl.when(t < P_TILES)
    def _():
        scatter(xp_ref)

    @pl.when(t >= P_TILES)
    def _():
        scatter(xs_ref)

    @pl.when(t == 0)
    def _():
        zero_scr[...] = jnp.zeros(zero_scr.shape, F32)

        def fill(act):
            def per_row(r, c):
                act(pltpu.make_async_copy(zero_scr.at[pl.ds(0, 1)], out_hbm.at[pl.ds(r, 1)], zsem))
                return c

            def per_expert(e, c):
                lax.fori_loop(meta_ref[N_EXPERTS + e] + meta_ref[e], meta_ref[2 * N_EXPERTS + e], per_row, 0)
                return c
            lax.fori_loop(0, N_EXPERTS, per_expert, 0)

            def per_block(g, c):
                rows = pl.ds(pl.multiple_of(g * EXP_BLK, EXP_BLK), EXP_BLK)
                act(pltpu.make_async_copy(zero_scr, out_hbm.at[rows], zsem))
                return c
            lax.fori_loop(meta_ref[3 * N_EXPERTS], X_ROWS // EXP_BLK, per_block, 0)

        fill(lambda cp: cp.start())
        fill(lambda cp: cp.wait())

    _wait_rows(TILE_ROWS, pltpu.make_async_copy(xp_ref.at[pl.ds(0, 1)], out_hbm.at[pl.ds(0, 1)], sem))


def _dispatch(meta, dest_flat, xn_p, xn_s):
    return pl.pallas_call(
        _dispatch_kernel,
        grid_spec=pltpu.PrefetchScalarGridSpec(
            num_scalar_prefetch=1,
            grid=(N_TILES,),
            in_specs=[
                pl.BlockSpec((TILE_ROWS,), lambda t, meta: (t,), memory_space=pltpu.SMEM),
                pl.BlockSpec((TM, D_MODEL), _prompt_tile),
                pl.BlockSpec((TM, D_MODEL), _sample_tile),
            ],
            out_specs=pl.BlockSpec(memory_space=pl.ANY),
            scratch_shapes=[pltpu.VMEM((EXP_BLK, D_MODEL), F32), pltpu.SemaphoreType.DMA(()),
                            pltpu.SemaphoreType.DMA(())],
        ),
        out_shape=jax.ShapeDtypeStruct((X_ROWS, D_MODEL), F32),
        compiler_params=_cparams(("arbitrary",)),
        name="dispatch",
    )(meta, dest_flat, xn_p, xn_s)


EX_BLKS = 12
EX_ROWS = EX_BLKS * EXP_BLK
EX_CHUNK = 2 * EXP_BLK
EX_GROUP = 4 * EXP_BLK
EX_TF = 256
EX_NF = D_FF // EX_TF
EX_TN = 512
N_ITEMS = (N_BLOCKS + (EX_BLKS - 1) * N_EXPERTS) // EX_BLKS
EXPERTS_VMEM_LIMIT = 56 * 1024 * 1024
assert EX_ROWS // EX_CHUNK < EX_NF


def _experts_kernel(ie_ref, ib_ref, inb_ref, ni_ref, nu_ref,
                    x_hbm, wg_ref, wu_ref, bg_ref, bu_ref, wd_ref, bd_ref, y_hbm,
                    stage_scr, xb_scr, acc_scr, wgb_scr, wub_scr, wdb_scr, xsem, osem, zsem):
    del ie_ref
    w = pl.program_id(0)
    f = pl.program_id(1)
    n_items = ni_ref[0]
    slot = jnp.bitwise_and(w, 1)

    def rows_at(row0, n):
        return pl.ds(pl.multiple_of(row0, EXP_BLK), n)

    def n_chunks(item):
        return lax.shift_right_logical(inb_ref[item] + 1, 1)

    def x_copy(item, c):
        st = jnp.bitwise_and(c, 1)
        return pltpu.make_async_copy(x_hbm.at[rows_at(ib_ref[item] * EXP_BLK + c * EX_CHUNK, EX_CHUNK)],
                                     stage_scr.at[st], xsem.at[st])

    def stage_to_bf16(half, c):
        xb_scr[half, rows_at(c * EX_CHUNK, EX_CHUNK), :] = stage_scr[jnp.bitwise_and(c, 1)].astype(BF16)

    def out_copy(item, row0, m):
        return pltpu.make_async_copy(acc_scr.at[rows_at(row0, m)],
                                     y_hbm.at[rows_at(ib_ref[item] * EXP_BLK + row0, m)], osem)

    def wait_out_copies(item):
        nb = inb_ref[item]
        n_big = lax.shift_right_logical(nb, 2)

        def per_group(c, carry):
            out_copy(item, c * EX_GROUP, EX_GROUP).wait()
            return carry
        lax.fori_loop(0, n_big, per_group, 0)

        @pl.when(jnp.bitwise_and(nb, 2) == 2)
        def _():
            out_copy(item, n_big * EX_GROUP, EX_CHUNK).wait()

        @pl.when(jnp.bitwise_and(nb, 1) == 1)
        def _():
            out_copy(item, (nb - 1) * EXP_BLK, EXP_BLK).wait()

    @pl.when(jnp.logical_and(w == 0, f == 0))
    def _():
        acc_scr[0:EXP_BLK, :] = jnp.zeros((EXP_BLK, D_MODEL), F32)

        def tail(act):
            def body(g, c):
                act(pltpu.make_async_copy(acc_scr.at[pl.ds(0, EXP_BLK)], y_hbm.at[rows_at(g * EXP_BLK, EXP_BLK)],
                                          zsem))
                return c
            lax.fori_loop(nu_ref[0], N_BLOCKS, body, 0)
        tail(lambda cp: cp.start())
        tail(lambda cp: cp.wait())

        def first_rows(c, carry):
            x_copy(0, c).start()
            x_copy(0, c).wait()
            stage_to_bf16(0, c)
            return carry
        lax.fori_loop(0, n_chunks(0), first_rows, 0)

    @pl.when(w < n_items)
    def _():
        nxt = jnp.minimum(w + 1, n_items - 1)
        stages_next = jnp.where(w + 1 < n_items, n_chunks(nxt), 0)

        @pl.when(jnp.logical_and(f >= 1, f - 1 < stages_next))
        def _():
            x_copy(nxt, f - 1).wait()
            stage_to_bf16(1 - slot, f - 1)

        @pl.when(f < stages_next)
        def _():
            x_copy(nxt, f).start()

        wgb_scr[...] = wg_ref[0, 0].astype(BF16)
        wub_scr[...] = wu_ref[0, 0].astype(BF16)
        wdb_scr[...] = wd_ref[0, 0].astype(BF16)

        @pl.when(jnp.logical_and(f == 0, w > 0))
        def _():
            wait_out_copies(w - 1)

        def ffn_rows(row0, m):
            rows = rows_at(row0, m)
            xb = xb_scr[slot, rows, :]
            gg = jnp.minimum(_dot(xb, wgb_scr[...]) + bg_ref[0], SWIGLU_LIMIT)
            uu = jnp.clip(_dot(xb, wub_scr[...]) + bu_ref[0], -SWIGLU_LIMIT, SWIGLU_LIMIT)
            hb = ((uu + 1.0) * (gg * _sigmoid(SWIGLU_ALPHA * gg))).astype(BF16)
            for c in range(D_MODEL // EX_TN):
                cols = slice(c * EX_TN, (c + 1) * EX_TN)
                part = _dot(hb, wdb_scr[:, cols])
                base = jnp.where(f == 0, jnp.broadcast_to(bd_ref[0][:, cols], (m, EX_TN)), acc_scr[rows, cols])
                acc_scr[rows, cols] = base + part

            @pl.when(f == EX_NF - 1)
            def _():
                out_copy(w, row0, m).start()

        nb = inb_ref[w]
        n_big = lax.shift_right_logical(nb, 2)

        def per_group(c, carry):
            ffn_rows(c * EX_GROUP, EX_GROUP)
            return carry
        lax.fori_loop(0, n_big, per_group, 0)

        @pl.when(jnp.bitwise_and(nb, 2) == 2)
        def _():
            ffn_rows(n_big * EX_GROUP, EX_CHUNK)

        @pl.when(jnp.bitwise_and(nb, 1) == 1)
        def _():
            ffn_rows((nb - 1) * EXP_BLK, EXP_BLK)

    @pl.when(jnp.logical_and(w == N_ITEMS - 1, f == EX_NF - 1))
    def _():
        wait_out_copies(n_items - 1)


def _experts(item_e, item_blk, item_nblk, n_items, n_used, x_sorted, w_gu, b_gu, w_down, b_down):
    def item(w, ni):
        return jnp.minimum(w, ni[0] - 1)

    def fsel(w, f, ni):
        return jnp.where(w < ni[0], f, EX_NF - 1)

    return pl.pallas_call(
        _experts_kernel,
        grid_spec=pltpu.PrefetchScalarGridSpec(
            num_scalar_prefetch=5,
            grid=(N_ITEMS, EX_NF),
            in_specs=[
                pl.BlockSpec(memory_space=pl.ANY),
                pl.BlockSpec((1, 1, D_MODEL, EX_TF),
                             lambda w, f, ie, ib, ins, ni, nu: (0, ie[item(w, ni)], 0, fsel(w, f, ni))),
                pl.BlockSpec((1, 1, D_MODEL, EX_TF),
                             lambda w, f, ie, ib, ins, ni, nu: (0, ie[item(w, ni)], 0, EX_NF + fsel(w, f, ni))),
                pl.BlockSpec((1, 1, EX_TF), lambda w, f, ie, ib, ins, ni, nu: (ie[item(w, ni)], 0, fsel(w, f, ni))),
                pl.BlockSpec((1, 1, EX_TF),
                             lambda w, f, ie, ib, ins, ni, nu: (ie[item(w, ni)], 0, EX_NF + fsel(w, f, ni))),
                pl.BlockSpec((1, 1, EX_TF, D_MODEL),
                             lambda w, f, ie, ib, ins, ni, nu: (0, ie[item(w, ni)], fsel(w, f, ni), 0)),
                pl.BlockSpec((1, 1, D_MODEL), lambda w, f, ie, ib, ins, ni, nu: (ie[item(w, ni)], 0, 0)),
            ],
            out_specs=pl.BlockSpec(memory_space=pl.ANY),
            scratch_shapes=[
                pltpu.VMEM((2, EX_CHUNK, D_MODEL), F32),
                pltpu.VMEM((2, EX_ROWS, D_MODEL), BF16),
                pltpu.VMEM((EX_ROWS, D_MODEL), F32),
                pltpu.VMEM((D_MODEL, EX_TF), BF16),
                pltpu.VMEM((D_MODEL, EX_TF), BF16),
                pltpu.VMEM((EX_TF, D_MODEL), BF16),
                pltpu.SemaphoreType.DMA((2,)),
                pltpu.SemaphoreType.DMA(()),
                pltpu.SemaphoreType.DMA(()),
            ],
        ),
        out_shape=jax.ShapeDtypeStruct((N_PAD, D_MODEL), F32),
        compiler_params=pltpu.CompilerParams(dimension_semantics=("arbitrary", "arbitrary"),
                                             vmem_limit_bytes=EXPERTS_VMEM_LIMIT),
        name="experts",
    )(item_e, item_blk, item_nblk, n_items, n_used, x_sorted, w_gu, w_gu, b_gu, b_gu, w_down, b_down)


def _combine_kernel(dcur_ref, dnext_ref, y_hbm, hp_ref, hs_ref, gate_ref, op_ref, os_ref, buf, sems):
    t = pl.program_id(0)
    slot = jnp.bitwise_and(t, 1)

    def row_copy(d_ref, sl):
        def make(a):
            tok = lax.shift_right_logical(a, 2)
            k = jnp.bitwise_and(a, TOP_K - 1)
            return pltpu.make_async_copy(y_hbm.at[pl.ds(d_ref[a], 1)], buf.at[sl, k, pl.ds(tok, 1)], sems.at[sl])
        return make

    @pl.when(t == 0)
    def _():
        _start_rows(TILE_ROWS, row_copy(dcur_ref, 0))

    @pl.when(t + 1 < N_TILES)
    def _():
        _start_rows(TILE_ROWS, row_copy(dnext_ref, 1 - slot))

    _wait_rows(TILE_ROWS, pltpu.make_async_copy(y_hbm.at[pl.ds(0, 1)], buf.at[slot, 0, pl.ds(0, 1)], sems.at[slot]))
    gate = gate_ref[...]
    moe = gate[:, 0:1] * buf[slot, 0]
    for k in range(1, TOP_K):
        moe = moe + gate[:, k:k + 1] * buf[slot, k]

    @pl.when(t < P_TILES)
    def _():
        op_ref[...] = hp_ref[...] + moe

    @pl.when(t >= P_TILES)
    def _():
        os_ref[...] = hs_ref[...] + moe


def _combine(dest_flat, y_sorted, h_p, h_s, gate):
    return pl.pallas_call(
        _combine_kernel,
        grid=(N_TILES,),
        in_specs=[
            pl.BlockSpec((TILE_ROWS,), lambda t: (t,), memory_space=pltpu.SMEM),
            pl.BlockSpec((TILE_ROWS,), lambda t: (jnp.minimum(t + 1, N_TILES - 1),), memory_space=pltpu.SMEM),
            pl.BlockSpec(memory_space=pl.ANY),
            pl.BlockSpec((TM, D_MODEL), _prompt_tile),
            pl.BlockSpec((TM, D_MODEL), _sample_tile),
            pl.BlockSpec((TM, LANES), lambda t: (t, 0)),
        ],
        out_specs=[pl.BlockSpec((TM, D_MODEL), _prompt_tile), pl.BlockSpec((TM, D_MODEL), _sample_tile)],
        out_shape=[jax.ShapeDtypeStruct((N_P, D_MODEL), F32), jax.ShapeDtypeStruct((N_S, D_MODEL), F32)],
        scratch_shapes=[pltpu.VMEM((2, TOP_K, TM, D_MODEL), F32), pltpu.SemaphoreType.DMA((2,))],
        compiler_params=_cparams(("arbitrary",)),
        name="combine",
    )(dest_flat, dest_flat, y_sorted, h_p, h_s, gate)


PL_TN = 512
PL_NJ = D_MODEL // PL_TN


def _ple_final_kernel(h_ref, p_ref, wp_ref, gp_ref, gg_ref, wg_ref, gf_ref, y_ref, hn_scr, e_scr, h3_scr):
    j = pl.program_id(1)

    @pl.when(j == 0)
    def _():
        hn_scr[...] = _rms(h_ref[...], gg_ref[...]).astype(BF16)
        e_scr[...] = _rms(_dot(p_ref[...].astype(BF16), wp_ref[...]), gp_ref[...])

    gate = _sigmoid(_dot(hn_scr[...], wg_ref[...]))
    for jj in range(PL_NJ):
        @pl.when(j == jj)
        def _(jj=jj):
            sl = slice(jj * PL_TN, (jj + 1) * PL_TN)
            h3_scr[:, sl] = h_ref[:, sl] + gate * e_scr[:, sl]

    @pl.when(j == PL_NJ - 1)
    def _():
        y_ref[...] = _rms(h3_scr[...], gf_ref[...])


def _ple_final(h2, p, tm, w_proj, g_proj, g_gate, w_gate, g_final, name):
    n = h2.shape[0]
    row = lambda i, j: (i, 0)
    const = lambda i, j: (0, 0)
    return pl.pallas_call(
        _ple_final_kernel,
        grid=(n // tm, PL_NJ),
        in_specs=[
            pl.BlockSpec((tm, D_MODEL), row),
            pl.BlockSpec((tm, D_PLE), row),
            pl.BlockSpec((D_PLE, D_MODEL), const),
            pl.BlockSpec((1, D_MODEL), const),
            pl.BlockSpec((1, D_MODEL), const),
            pl.BlockSpec((D_MODEL, PL_TN), lambda i, j: (0, j)),
            pl.BlockSpec((1, D_MODEL), const),
        ],
        out_specs=pl.BlockSpec((tm, D_MODEL), row),
        out_shape=jax.ShapeDtypeStruct((n, D_MODEL), F32),
        scratch_shapes=[pltpu.VMEM((tm, D_MODEL), BF16), pltpu.VMEM((tm, D_MODEL), F32),
                        pltpu.VMEM((tm, D_MODEL), F32)],
        compiler_params=_cparams(("arbitrary", "arbitrary")),
        name=name,
    )(h2, p, w_proj, g_proj, g_gate, w_gate, g_final)


def kernel(x_prompt, x_sample, cache_k, cache_v, cache_logf, state_conv, state_rglru, p_prompt, p_sample,
           g_mix, w_in, b_fgate, conv_w, conv_b, rg_w_a, rg_b_a, rg_w_i, rg_b_i, rg_lambda,
           g_out_rnn, g_out_fox, w_out, g_moe, router_w, router_b, w_gu, b_gu, w_down, b_down,
           w_ple_proj, g_ple_proj, g_ple_gate, w_ple_gate, g_final):
    assert g_mix.shape[0] == 1, "one layer"
    row = lambda v: v.reshape(1, -1)
    lane_pad = lambda a: jnp.pad(a, ((0, 0), (0, LANES - a.shape[1])))

    xp = x_prompt.reshape(N_P, D_MODEL)
    xs = x_sample.reshape(N_S, D_MODEL)

    w_main = w_in[0][:, :D_MAIN].astype(BF16)
    w_f = lane_pad(w_in[0][:, D_MAIN:]).astype(BF16)
    b_f = lane_pad(row(b_fgate[0]))
    proj_p, lf_p = _in_proj(xp, TM_IN, row(g_mix[0]), w_main, w_f, b_f, "in_proj_prompt")
    proj_s, lf_s = _in_proj(xs, N_S, row(g_mix[0]), w_main, w_f, b_f, "in_proj_sample")
    logf_p = lf_p[:, :FOX_HEADS].reshape(BATCH, SEQ, FOX_HEADS)
    logf_s = lf_s[:, :FOX_HEADS].reshape(DEC_BATCH, DEC_SEQ, FOX_HEADS)
    k_p, v_p = proj_p[:, COL_K:COL_V], proj_p[:, COL_V:]
    q_s, k_s, v_s = proj_s[:, COL_Q:COL_K], proj_s[:, COL_K:COL_V], proj_s[:, COL_V:]

    rg = (conv_w[0], row(conv_b[0]), rg_w_a[0].astype(BF16), row(rg_b_a[0]), rg_w_i[0].astype(BF16),
          row(rg_b_i[0]), row(rg_lambda[0]))
    o_rnn_p, conv_p, hl_p = _rglru(proj_p, 0, BATCH, SEQ // TM, TM,
                                   jnp.zeros((BATCH, CONV_W - 1, D_RNN), F32), jnp.zeros((BATCH, 1, D_RNN), F32),
                                   *rg, name="rglru_prompt")
    o_rnn_s, conv_s, hl_s = _rglru(proj_s, 0, DEC_BATCH, 1, DEC_SEQ,
                                   state_conv[0], state_rglru[0].reshape(DEC_BATCH, 1, D_RNN),
                                   *rg, name="rglru_sample")

    c_p = _cumsum_rows(logf_p.transpose(1, 0, 2).reshape(SEQ, BATCH * FOX_HEADS), "cumsum_prompt")
    c_pT = c_p.T
    o_fox_p = _fox_prompt(proj_p, v_p.T, c_pT[:, :, None], c_pT[:, None, :])

    lf_all = jnp.concatenate([cache_logf[0], logf_s], axis=1)
    c_s = _cumsum_rows(lf_all.transpose(1, 0, 2).reshape(PAST_LEN + DEC_SEQ, DEC_BATCH * FOX_HEADS),
                       "cumsum_sample")
    c_bth = c_s.reshape(PAST_LEN + DEC_SEQ, DEC_BATCH, FOX_HEADS).transpose(1, 0, 2)
    ck = c_bth.reshape(DEC_BATCH, 1, (PAST_LEN + DEC_SEQ) * FOX_HEADS)
    cq = c_bth[:, PAST_LEN:].reshape(DEC_BATCH, FS_ROWS, 1)
    per_req = lambda a: a.reshape(DEC_BATCH, FS_ROWS, FOX_HEAD_DIM)
    o_fox_s = _fox_sample(per_req(q_s), cache_k, cache_v, per_req(k_s), per_req(v_s), cq, ck)
    o_fox_s = o_fox_s.reshape(N_S, D_FOX)

    rw = lane_pad(router_w[0])
    rw_hi = rw.astype(BF16)
    rw_lo = (rw - rw_hi.astype(F32)).astype(BF16)
    rb = jnp.concatenate([row(router_b[0]), jnp.full((1, LANES - N_EXPERTS), NEG, F32)], axis=1)
    op_w = (row(g_out_rnn[0]), row(g_out_fox[0]), w_out[0].astype(BF16), row(g_moe[0]), rw_hi, rw_lo, rb)
    h1_p, xn_p, logits_p = _out_proj(o_rnn_p, o_fox_p, xp, TM_OUT, *op_w, name="out_proj_prompt")
    h1_s, xn_s, logits_s = _out_proj(o_rnn_s, o_fox_s, xs, N_S, *op_w, name="out_proj_sample")

    ei, gate, cnt = _route(logits_p, logits_s)
    counts = cnt[0, :N_EXPERTS]
    padded = (counts + EXP_BLK - 1) // EXP_BLK * EXP_BLK
    pad_end = jnp.cumsum(padded)
    pad_start = pad_end - padded
    dest = (pad_start[ei[:, :TOP_K]] + ei[:, TOP_K:2 * TOP_K]).reshape(N_ROWS)
    n_used = (pad_end[-1] // EXP_BLK).reshape(1).astype(I32)
    meta = jnp.concatenate([counts, pad_start, pad_end, n_used]).astype(I32)
    x_sorted = _dispatch(meta, dest, xn_p, xn_s)
    blocks_e = padded // EXP_BLK
    items_e = (blocks_e + EX_BLKS - 1) // EX_BLKS
    item_end = jnp.cumsum(items_e)
    item_first = item_end - items_e
    item_id = jnp.arange(N_ITEMS, dtype=I32)
    item_e = jnp.minimum(jnp.sum((item_end[None, :] <= item_id[:, None]).astype(I32), axis=1), N_EXPERTS - 1)
    item_run = item_id - item_first[item_e]
    item_blk = jnp.clip(pad_start[item_e] // EXP_BLK + EX_BLKS * item_run, 0, N_BLOCKS - 1).astype(I32)
    item_nblk = jnp.clip(blocks_e[item_e] - EX_BLKS * item_run, 1, EX_BLKS).astype(I32)
    n_items = item_end[-1].reshape(1).astype(I32)
    y_sorted = _experts(item_e.astype(I32), item_blk, item_nblk, n_items, n_used, x_sorted,
                        w_gu, b_gu[0][:, None, :], w_down, b_down[0][:, None, :])
    h2_p, h2_s = _combine(dest, y_sorted, h1_p, h1_s, gate)

    pf_w = (w_ple_proj[0].astype(BF16), row(g_ple_proj[0]), row(g_ple_gate[0]), w_ple_gate[0].astype(BF16),
            row(g_final))
    y_p = _ple_final(h2_p, p_prompt[0].reshape(N_P, D_PLE), TM_OUT, *pf_w, name="ple_final_prompt")
    y_s = _ple_final(h2_s, p_sample[0].reshape(N_S, D_PLE), N_S, *pf_w, name="ple_final_sample")

    shp = (1, BATCH, SEQ, FOX_HEADS, FOX_HEAD_DIM)
    shs = (1, DEC_BATCH, DEC_SEQ, FOX_HEADS, FOX_HEAD_DIM)
    return (y_p.reshape(BATCH, SEQ, D_MODEL), y_s.reshape(DEC_BATCH, DEC_SEQ, D_MODEL),
            k_p.reshape(shp), v_p.reshape(shp), logf_p[None],
            conv_p[None], hl_p.reshape(1, BATCH, D_RNN),
            k_s.reshape(shs), v_s.reshape(shs), logf_s[None],
            conv_s[None], hl_s.reshape(1, DEC_BATCH, D_RNN))
```

```python
import functools

import jax
import jax.numpy as jnp
from jax import lax
from jax.experimental import pallas as pl
from jax.experimental.pallas import tpu as pltpu

F32 = jnp.float32
BF16 = jnp.bfloat16
I32 = jnp.int32

D_MODEL = 2048
BATCH = 4
SEQ = 2048
DEC_BATCH = 16
DEC_SEQ = 16
PAST_LEN = 4096
D_RNN = 1024
RG_BLOCKS = 8
RG_BLOCK_W = 128
CONV_W = 4
RG_C = 8.0
FOX_HEADS = 8
FOX_HEAD_DIM = 128
D_FOX = 1024
D_MAIN = 2 * D_RNN + 3 * D_FOX
N_EXPERTS = 32
TOP_K = 4
D_FF = 2048
SWIGLU_LIMIT = 7.0
SWIGLU_ALPHA = 1.702
D_PLE = 256
EPS = 1e-6

LANES = 128
N_P = BATCH * SEQ
N_S = DEC_BATCH * DEC_SEQ
N_TOK = N_P + N_S
TM = 256
TM_IN = 512
TM_OUT = 512
N_TILES = N_TOK // TM
P_TILES = N_P // TM
N_ROWS = N_TOK * TOP_K
EXP_BLK = 128
N_BLOCKS = (N_ROWS + N_EXPERTS * (EXP_BLK - 1) + EXP_BLK - 1) // EXP_BLK
N_PAD = N_BLOCKS * EXP_BLK
X_ROWS = N_PAD + EXP_BLK
NEG = -1e30
VMEM_LIMIT = 48 * 1024 * 1024


def _cparams(sem):
    return pltpu.CompilerParams(dimension_semantics=sem, vmem_limit_bytes=VMEM_LIMIT)


def _rms(x, g):
    return x * lax.rsqrt(jnp.mean(x * x, axis=-1, keepdims=True) + EPS) * g


def _log_sigmoid(z):
    return jnp.minimum(z, 0.0) - jnp.log1p(jnp.exp(-jnp.abs(z)))


def _sigmoid(z):
    return 1.0 / (1.0 + jnp.exp(-z))


def _expm1(x):
    u = jnp.exp(x)
    degenerate = jnp.logical_or(u == 1.0, u == 0.0)
    val = (u - 1.0) * x / jnp.log(jnp.where(degenerate, 0.5, u))
    return jnp.where(u == 1.0, x, jnp.where(u == 0.0, -1.0, val))


def _gelu_tanh(y):
    return 0.5 * y * (1.0 + jnp.tanh(0.7978845608028654 * (y + 0.044715 * (y * y * y))))


def _dot(a, b):
    return jnp.dot(a, b, preferred_element_type=F32)


def _dot_nt(a, b):
    return lax.dot_general(a, b, (((1,), (1,)), ((), ())), preferred_element_type=F32)


IN_TN = 1024
IN_NJ = D_MAIN // IN_TN
COL_Q = 2 * D_RNN
COL_K = COL_Q + D_FOX
COL_V = COL_K + D_FOX
J_V = COL_V // IN_TN


def _in_proj_kernel(x_ref, g_ref, w_ref, wf_ref, bf_ref, proj_ref, logf_ref, *rest, with_vt):
    j = pl.program_id(1)
    xn_scr = rest[-1]

    @pl.when(j == 0)
    def _():
        xb = _rms(x_ref[...], g_ref[...]).astype(BF16)
        xn_scr[...] = xb
        logf_ref[...] = _log_sigmoid(_dot(xb, wf_ref[...]) + bf_ref[...])

    r = _dot(xn_scr[...], w_ref[...])
    proj_ref[...] = r
    if with_vt:
        vt_ref = rest[0]

        @pl.when(j == J_V)
        def _():
            vt_ref[...] = r.T


def _in_proj(x, tm, g_mix, w_main, w_f, b_f, with_vt, name):
    n = x.shape[0]
    row = lambda i, j: (i, 0)
    const = lambda i, j: (0, 0)
    out_specs = [pl.BlockSpec((tm, IN_TN), lambda i, j: (i, j)), pl.BlockSpec((tm, LANES), row)]
    out_shape = [jax.ShapeDtypeStruct((n, D_MAIN), F32), jax.ShapeDtypeStruct((n, LANES), F32)]
    if with_vt:
        out_specs.append(pl.BlockSpec((D_FOX, tm), lambda i, j: (0, i)))
        out_shape.append(jax.ShapeDtypeStruct((D_FOX, n), F32))
    return pl.pallas_call(
        functools.partial(_in_proj_kernel, with_vt=with_vt),
        grid=(n // tm, IN_NJ),
        in_specs=[
            pl.BlockSpec((tm, D_MODEL), row),
            pl.BlockSpec((1, D_MODEL), const),
            pl.BlockSpec((D_MODEL, IN_TN), lambda i, j: (0, j)),
            pl.BlockSpec((D_MODEL, LANES), const),
            pl.BlockSpec((1, LANES), const),
        ],
        out_specs=out_specs,
        out_shape=out_shape,
        scratch_shapes=[pltpu.VMEM((tm, D_MODEL), BF16)],
        compiler_params=_cparams(("arbitrary", "arbitrary")),
        name=name,
    )(x, g_mix, w_main, w_f, b_f)


def _prefix_steps(n):
    s = 1
    while s < n:
        yield s
        s *= 2


def _rglru_kernel(x_ref, y_ref, past_ref, h0_ref, cw_ref, cb_ref, wa_ref, ba_ref, wi_ref, bi_ref, lam_ref,
                  o_ref, conv_ref, hlast_ref, xp_scr, h_scr, *, tt, nt):
    t = pl.program_id(1)
    pad = 8

    @pl.when(t == 0)
    def _():
        xp_scr[pad - (CONV_W - 1):pad, :] = past_ref[0]
        h_scr[...] = h0_ref[0]

    xp_scr[pad:pad + tt, :] = x_ref[...]
    xc = cb_ref[...] + xp_scr[pad - 3:pad - 3 + tt, :] * cw_ref[0:1, :]
    for jw in range(1, CONV_W):
        xc = xc + xp_scr[pad - 3 + jw:pad - 3 + jw + tt, :] * cw_ref[jw:jw + 1, :]

    r_parts, i_parts = [], []
    for n in range(RG_BLOCKS):
        xb = xc[:, n * RG_BLOCK_W:(n + 1) * RG_BLOCK_W].astype(BF16)
        r_parts.append(_dot(xb, wa_ref[n]))
        i_parts.append(_dot(xb, wi_ref[n]))
    r = _sigmoid(jnp.concatenate(r_parts, axis=-1) + ba_ref[...])
    ig = _sigmoid(jnp.concatenate(i_parts, axis=-1) + bi_ref[...])
    log_a = RG_C * r * _log_sigmoid(lam_ref[...])
    a = jnp.exp(log_a)
    b = jnp.sqrt(-_expm1(2.0 * log_a)) * (ig * xc)

    rowi = lax.broadcasted_iota(I32, (tt, D_RNN), 0)
    for s in _prefix_steps(tt):
        keep = rowi >= s
        a_sh = jnp.where(keep, pltpu.roll(a, s, 0), 1.0)
        b_sh = jnp.where(keep, pltpu.roll(b, s, 0), 0.0)
        b = a * b_sh + b
        a = a * a_sh
    h = a * h_scr[...] + b
    h_scr[...] = h[tt - 1:tt, :]
    o_ref[...] = h * _gelu_tanh(y_ref[...])

    tail = xp_scr[pad + tt - (CONV_W - 1):pad + tt, :]
    xp_scr[pad - (CONV_W - 1):pad, :] = tail

    @pl.when(t == nt - 1)
    def _():
        conv_ref[0] = tail
        hlast_ref[0] = h[tt - 1:tt, :]


def _rglru(xy, row0, nb, nt, tt, conv_past, h0, cw, cb, wa, ba, wi, bi, lam, name):
    blk0 = row0 // tt
    vec = lambda b, t: (0, 0)
    w3 = lambda b, t: (0, 0, 0)
    kern = functools.partial(_rglru_kernel, tt=tt, nt=nt)
    return pl.pallas_call(
        kern,
        grid=(nb, nt),
        in_specs=[
            pl.BlockSpec((tt, D_RNN), lambda b, t: (blk0 + b * nt + t, 0)),
            pl.BlockSpec((tt, D_RNN), lambda b, t: (blk0 + b * nt + t, 1)),
            pl.BlockSpec((1, CONV_W - 1, D_RNN), lambda b, t: (b, 0, 0)),
            pl.BlockSpec((1, 1, D_RNN), lambda b, t: (b, 0, 0)),
            pl.BlockSpec((CONV_W, D_RNN), vec),
            pl.BlockSpec((1, D_RNN), vec),
            pl.BlockSpec((RG_BLOCKS, RG_BLOCK_W, RG_BLOCK_W), w3),
            pl.BlockSpec((1, D_RNN), vec),
            pl.BlockSpec((RG_BLOCKS, RG_BLOCK_W, RG_BLOCK_W), w3),
            pl.BlockSpec((1, D_RNN), vec),
            pl.BlockSpec((1, D_RNN), vec),
        ],
        out_specs=[
            pl.BlockSpec((tt, D_RNN), lambda b, t: (b * nt + t, 0)),
            pl.BlockSpec((1, CONV_W - 1, D_RNN), lambda b, t: (b, 0, 0)),
            pl.BlockSpec((1, 1, D_RNN), lambda b, t: (b, 0, 0)),
        ],
        out_shape=[
            jax.ShapeDtypeStruct((nb * nt * tt, D_RNN), F32),
            jax.ShapeDtypeStruct((nb, CONV_W - 1, D_RNN), F32),
            jax.ShapeDtypeStruct((nb, 1, D_RNN), F32),
        ],
        scratch_shapes=[pltpu.VMEM((tt + 8, D_RNN), F32), pltpu.VMEM((1, D_RNN), F32)],
        compiler_params=_cparams(("arbitrary", "arbitrary")),
        name=name,
    )(xy, xy, conv_past, h0, cw, cb, wa, ba, wi, bi, lam)


def _cumsum_kernel(x_ref, o_ref, *, n):
    x = x_ref[...]
    rowi = lax.broadcasted_iota(I32, x.shape, 0)
    for s in _prefix_steps(n):
        x = x + jnp.where(rowi >= s, pltpu.roll(x, s, 0), 0.0)
    o_ref[...] = x


def _cumsum_rows(x, name):
    return pl.pallas_call(
        functools.partial(_cumsum_kernel, n=x.shape[0]),
        out_shape=jax.ShapeDtypeStruct(x.shape, F32),
        compiler_params=pltpu.CompilerParams(vmem_limit_bytes=VMEM_LIMIT),
        name=name,
    )(x)


FP_T = 512
FP_NT = SEQ // FP_T
FOX_SCALE = FOX_HEAD_DIM ** -0.5


FP_HB = 4
FP_W = FP_HB * FOX_HEAD_DIM
FP_GROUPS = FOX_HEADS // FP_HB


def _fox_prompt_kernel(q_ref, k_ref, vt_ref, cq_ref, ck_ref, o_ref, m_scr, l_scr, acc_scr):
    qi = pl.program_id(1)
    ki = pl.program_id(2)

    @pl.when(ki == 0)
    def _():
        m_scr[...] = jnp.full(m_scr.shape, NEG, F32)
        l_scr[...] = jnp.zeros(l_scr.shape, F32)
        acc_scr[...] = jnp.zeros(acc_scr.shape, F32)

    def update(diagonal):
        for hh in range(FP_HB):
            cols = slice(hh * FOX_HEAD_DIM, (hh + 1) * FOX_HEAD_DIM)
            st = _dot_nt(k_ref[:, cols].astype(BF16), q_ref[:, cols].astype(BF16)) * FOX_SCALE
            st = st + (cq_ref[hh] - ck_ref[hh])
            if diagonal:
                key = lax.broadcasted_iota(I32, (FP_T, FP_T), 0)
                qry = lax.broadcasted_iota(I32, (FP_T, FP_T), 1)
                st = jnp.where(key <= qry, st, NEG)
            m_old = m_scr[hh]
            m_new = jnp.maximum(m_old, jnp.max(st, axis=0, keepdims=True))
            alpha = jnp.exp(m_old - m_new)
            p = jnp.exp(st - m_new)
            l_scr[hh] = alpha * l_scr[hh] + jnp.sum(p, axis=0, keepdims=True)
            acc_scr[hh] = alpha * acc_scr[hh] + _dot(vt_ref[cols, :].astype(BF16), p.astype(BF16))
            m_scr[hh] = m_new

    @pl.when(ki < qi)
    def _():
        update(False)

    @pl.when(ki == qi)
    def _():
        update(True)

    @pl.when(ki == FP_NT - 1)
    def _():
        for hh in range(FP_HB):
            cols = slice(hh * FOX_HEAD_DIM, (hh + 1) * FOX_HEAD_DIM)
            o_ref[:, cols] = (acc_scr[hh] / l_scr[hh]).T


def _fox_prompt(proj_p, v_t, c_col, c_row):
    def rows(g, blk):
        return (g // FP_GROUPS) * FP_NT + blk

    def colblk(g, col0):
        return col0 // FP_W + g % FP_GROUPS

    seen = lambda qi, ki: jnp.minimum(ki, qi)
    return pl.pallas_call(
        _fox_prompt_kernel,
        grid=(BATCH * FP_GROUPS, FP_NT, FP_NT),
        in_specs=[
            pl.BlockSpec((FP_T, FP_W), lambda g, qi, ki: (rows(g, qi), colblk(g, COL_Q))),
            pl.BlockSpec((FP_T, FP_W), lambda g, qi, ki: (rows(g, seen(qi, ki)), colblk(g, COL_K))),
            pl.BlockSpec((FP_W, FP_T), lambda g, qi, ki: (g % FP_GROUPS, rows(g, seen(qi, ki)))),
            pl.BlockSpec((FP_HB, 1, FP_T), lambda g, qi, ki: (g, 0, qi)),
            pl.BlockSpec((FP_HB, FP_T, 1), lambda g, qi, ki: (g, seen(qi, ki), 0)),
        ],
        out_specs=pl.BlockSpec((FP_T, FP_W), lambda g, qi, ki: (rows(g, qi), g % FP_GROUPS)),
        out_shape=jax.ShapeDtypeStruct((N_P, D_FOX), F32),
        scratch_shapes=[pltpu.VMEM((FP_HB, 1, FP_T), F32), pltpu.VMEM((FP_HB, 1, FP_T), F32),
                        pltpu.VMEM((FP_HB, FOX_HEAD_DIM, FP_T), F32)],
        compiler_params=_cparams(("arbitrary", "arbitrary", "arbitrary")),
        name="fox_prompt",
    )(proj_p, proj_p, v_t, c_row, c_col)


FS_TK = 1024
FS_NT = PAST_LEN // FS_TK
FS_ROWS = DEC_SEQ * FOX_HEADS


FS_COLS = FS_TK * FOX_HEADS


def _fox_sample_kernel(q_ref, kc_ref, vc_ref, kn_ref, vn_ref, cq_ref, ckp_ref, ckn_ref, o_ref,
                       m_scr, l_scr, acc_scr):
    kt = pl.program_id(1)

    @pl.when(kt == 0)
    def _():
        m_scr[...] = jnp.full(m_scr.shape, NEG, F32)
        l_scr[...] = jnp.zeros(l_scr.shape, F32)
        acc_scr[...] = jnp.zeros(acc_scr.shape, F32)

    qb = q_ref[0].astype(BF16)

    def step(k2, v2, ck, causal):
        n = k2.shape[0]
        s = _dot_nt(qb, k2.astype(BF16)) * FOX_SCALE + (cq_ref[0] - ck)
        rowi = lax.broadcasted_iota(I32, (FS_ROWS, n), 0)
        coli = lax.broadcasted_iota(I32, (FS_ROWS, n), 1)
        keep = jnp.bitwise_and(coli, FOX_HEADS - 1) == jnp.bitwise_and(rowi, FOX_HEADS - 1)
        s = jnp.where(keep, s, NEG)
        if causal:
            s = jnp.where(jnp.right_shift(coli, 3) <= jnp.right_shift(rowi, 3), s, NEG)
        m_old = m_scr[...]
        m_new = jnp.maximum(m_old, jnp.max(s, axis=-1, keepdims=True))
        alpha = jnp.exp(m_old - m_new)
        p = jnp.exp(s - m_new)
        l_scr[...] = alpha * l_scr[...] + jnp.sum(p, axis=-1, keepdims=True)
        acc_scr[...] = alpha * acc_scr[...] + _dot(p.astype(BF16), v2.astype(BF16))
        m_scr[...] = m_new

    step(kc_ref[0, 0].reshape(FS_COLS, FOX_HEAD_DIM), vc_ref[0, 0].reshape(FS_COLS, FOX_HEAD_DIM),
         ckp_ref[0], False)

    @pl.when(kt == FS_NT - 1)
    def _():
        step(kn_ref[0], vn_ref[0], ckn_ref[0], True)
        o_ref[0] = acc_scr[...] / l_scr[...]


def _fox_sample(q_s, cache_k, cache_v, k_n, v_n, cq, ck):
    per_req = lambda b, kt: (b, 0, 0)
    cache = lambda b, kt: (0, b, kt, 0, 0)
    return pl.pallas_call(
        _fox_sample_kernel,
        grid=(DEC_BATCH, FS_NT),
        in_specs=[
            pl.BlockSpec((1, FS_ROWS, FOX_HEAD_DIM), per_req),
            pl.BlockSpec((1, 1, FS_TK, FOX_HEADS, FOX_HEAD_DIM), cache),
            pl.BlockSpec((1, 1, FS_TK, FOX_HEADS, FOX_HEAD_DIM), cache),
            pl.BlockSpec((1, FS_ROWS, FOX_HEAD_DIM), per_req),
            pl.BlockSpec((1, FS_ROWS, FOX_HEAD_DIM), per_req),
            pl.BlockSpec((1, FS_ROWS, 1), per_req),
            pl.BlockSpec((1, 1, FS_COLS), lambda b, kt: (b, 0, kt)),
            pl.BlockSpec((1, 1, FS_ROWS), lambda b, kt: (b, 0, PAST_LEN * FOX_HEADS // FS_ROWS)),
        ],
        out_specs=pl.BlockSpec((1, FS_ROWS, FOX_HEAD_DIM), per_req),
        out_shape=jax.ShapeDtypeStruct((DEC_BATCH, FS_ROWS, FOX_HEAD_DIM), F32),
        scratch_shapes=[pltpu.VMEM((FS_ROWS, 1), F32), pltpu.VMEM((FS_ROWS, 1), F32),
                        pltpu.VMEM((FS_ROWS, FOX_HEAD_DIM), F32)],
        compiler_params=_cparams(("arbitrary", "arbitrary")),
        name="fox_sample",
    )(q_s, cache_k, cache_v, k_n, v_n, cq, ck, ck)


OP_TN = 512
OP_NJ = D_MODEL // OP_TN


def _out_proj_kernel(orn_ref, ofx_ref, x_ref, g1_ref, g2_ref, w_ref, gm_ref,
                     rwh_ref, rwl_ref, rb_ref, h_ref, xn_ref, lg_ref, mix_scr):
    j = pl.program_id(1)

    @pl.when(j == 0)
    def _():
        mix_scr[:, :D_RNN] = _rms(orn_ref[...], g1_ref[...]).astype(BF16)
        mix_scr[:, D_RNN:] = _rms(ofx_ref[...], g2_ref[...]).astype(BF16)

    r = _dot(mix_scr[...], w_ref[...])
    for jj in range(OP_NJ):
        @pl.when(j == jj)
        def _(jj=jj):
            h_ref[:, jj * OP_TN:(jj + 1) * OP_TN] = x_ref[:, jj * OP_TN:(jj + 1) * OP_TN] + r

    @pl.when(j == OP_NJ - 1)
    def _():
        xn = _rms(h_ref[...], gm_ref[...])
        xn_ref[...] = xn
        xh = xn.astype(BF16)
        xl = (xn - xh.astype(F32)).astype(BF16)
        lg_ref[...] = (_dot(xh, rwh_ref[...]) + _dot(xl, rwh_ref[...]) + _dot(xh, rwl_ref[...])) + rb_ref[...]


def _out_proj(o_rnn, o_fox, x, tm, g1, g2, w_out, g_moe, rw_hi, rw_lo, rb, name):
    n = x.shape[0]
    row = lambda i, j: (i, 0)
    const = lambda i, j: (0, 0)
    return pl.pallas_call(
        _out_proj_kernel,
        grid=(n // tm, OP_NJ),
        in_specs=[
            pl.BlockSpec((tm, D_RNN), row),
            pl.BlockSpec((tm, D_FOX), row),
            pl.BlockSpec((tm, D_MODEL), row),
            pl.BlockSpec((1, D_RNN), const),
            pl.BlockSpec((1, D_FOX), const),
            pl.BlockSpec((D_MODEL, OP_TN), lambda i, j: (0, j)),
            pl.BlockSpec((1, D_MODEL), const),
            pl.BlockSpec((D_MODEL, LANES), const),
            pl.BlockSpec((D_MODEL, LANES), const),
            pl.BlockSpec((1, LANES), const),
        ],
        out_specs=[
            pl.BlockSpec((tm, D_MODEL), row),
            pl.BlockSpec((tm, D_MODEL), row),
            pl.BlockSpec((tm, LANES), row),
        ],
        out_shape=[
            jax.ShapeDtypeStruct((n, D_MODEL), F32),
            jax.ShapeDtypeStruct((n, D_MODEL), F32),
            jax.ShapeDtypeStruct((n, LANES), F32),
        ],
        scratch_shapes=[pltpu.VMEM((tm, D_MODEL), BF16)],
        compiler_params=_cparams(("arbitrary", "arbitrary")),
        name=name,
    )(o_rnn, o_fox, x, g1, g2, w_out, g_moe, rw_hi, rw_lo, rb)


def _route_kernel(lgp_ref, lgs_ref, ei_ref, gate_ref, cnt_ref, carry_scr):
    t = pl.program_id(0)

    @pl.when(t == 0)
    def _():
        carry_scr[...] = jnp.zeros(carry_scr.shape, F32)

    lane = lax.broadcasted_iota(I32, (TM, LANES), 1)
    lane_f = lane.astype(F32)
    work = jnp.where(t < P_TILES, lgp_ref[...], lgs_ref[...])
    tops, idxs, hots = [], [], []
    for _ in range(TOP_K):
        m = jnp.max(work, axis=-1, keepdims=True)
        idx_f = jnp.min(jnp.where(work == m, lane_f, float(LANES)), axis=-1, keepdims=True)
        hot = lane_f == idx_f
        work = jnp.where(hot, -jnp.inf, work)
        tops.append(m)
        idxs.append(idx_f.astype(I32))
        hots.append(hot)

    es = [jnp.exp(tv - tops[0]) for tv in tops]
    denom = es[0] + es[1] + es[2] + es[3]
    gate = jnp.zeros((TM, LANES), F32)
    for k in range(TOP_K):
        gate = jnp.where(lane == k, es[k] / denom, gate)
    gate_ref[...] = gate

    multi = jnp.zeros((TM, LANES), F32)
    for k in range(TOP_K):
        multi = jnp.where(hots[k], 1.0, multi)
    r_i = lax.broadcasted_iota(I32, (TM, TM), 0)
    c_i = lax.broadcasted_iota(I32, (TM, TM), 1)
    strict_lower = jnp.where(c_i < r_i, 1.0, 0.0).astype(BF16)
    before = _dot(strict_lower, multi.astype(BF16)) + carry_scr[...]
    ei = jnp.zeros((TM, LANES), I32)
    for k in range(TOP_K):
        rank = jnp.sum(jnp.where(hots[k], before, 0.0), axis=-1, keepdims=True).astype(I32)
        ei = jnp.where(lane == k, idxs[k], ei)
        ei = jnp.where(lane == TOP_K + k, rank, ei)
    ei_ref[...] = ei
    carry_scr[...] = carry_scr[...] + jnp.sum(multi, axis=0, keepdims=True)
    cnt_ref[...] = carry_scr[...].astype(I32)


def _prompt_tile(t, *_):
    return (jnp.minimum(t, P_TILES - 1), 0)


def _sample_tile(t, *_):
    return (jnp.maximum(t - P_TILES, 0), 0)


def _route(logits_p, logits_s):
    row = lambda t: (t, 0)
    return pl.pallas_call(
        _route_kernel,
        grid=(N_TILES,),
        in_specs=[pl.BlockSpec((TM, LANES), _prompt_tile), pl.BlockSpec((TM, LANES), _sample_tile)],
        out_specs=[pl.BlockSpec((TM, LANES), row), pl.BlockSpec((TM, LANES), row),
                   pl.BlockSpec((1, LANES), lambda t: (0, 0))],
        out_shape=[jax.ShapeDtypeStruct((N_TOK, LANES), I32), jax.ShapeDtypeStruct((N_TOK, LANES), F32),
                   jax.ShapeDtypeStruct((1, LANES), I32)],
        scratch_shapes=[pltpu.VMEM((1, LANES), F32)],
        compiler_params=_cparams(("arbitrary",)),
        name="route",
    )(logits_p, logits_s)


TILE_ROWS = TM * TOP_K
DMA_GROUP = 8
WAIT_GROUP = 32


def _start_token_rows(make_copy):
    toks_per_trip = DMA_GROUP // TOP_K

    def body(ti, c):
        for u in range(toks_per_trip):
            for k in range(TOP_K):
                make_copy(ti * toks_per_trip + u, k).start(priority=k % 2)
        return c
    lax.fori_loop(0, TM // toks_per_trip, body, 0)


def _wait_rows(n, copy):
    def body(gi, c):
        for _ in range(WAIT_GROUP):
            copy.wait()
        return c
    lax.fori_loop(0, n // WAIT_GROUP, body, 0)


def _dispatch_kernel(meta_ref, dest_ref, xp_ref, xs_ref, out_hbm, zero_scr, sem, zsem):
    t = pl.program_id(0)

    def scatter(x_ref):
        def row_copy(tok, k):
            return pltpu.make_async_copy(x_ref.at[pl.ds(tok, 1)], out_hbm.at[pl.ds(dest_ref[tok * TOP_K + k], 1)],
                                         sem)
        _start_token_rows(row_copy)

    @pl.when(t < P_TILES)
    def _():
        scatter(xp_ref)

    @pl.when(t >= P_TILES)
    def _():
        scatter(xs_ref)

    @pl.when(t == 0)
    def _():
        zero_scr[...] = jnp.zeros(zero_scr.shape, F32)

        def fill(act):
            def per_row(r, c):
                act(pltpu.make_async_copy(zero_scr.at[pl.ds(0, 1)], out_hbm.at[pl.ds(r, 1)], zsem))
                return c

            def per_expert(e, c):
                lax.fori_loop(meta_ref[N_EXPERTS + e] + meta_ref[e], meta_ref[2 * N_EXPERTS + e], per_row, 0)
                return c
            lax.fori_loop(0, N_EXPERTS, per_expert, 0)

            def per_block(g, c):
                rows = pl.ds(pl.multiple_of(g * EXP_BLK, EXP_BLK), EXP_BLK)
                act(pltpu.make_async_copy(zero_scr, out_hbm.at[rows], zsem))
                return c
            lax.fori_loop(meta_ref[3 * N_EXPERTS], X_ROWS // EXP_BLK, per_block, 0)

        fill(lambda cp: cp.start())
        fill(lambda cp: cp.wait())

    _wait_rows(TILE_ROWS, pltpu.make_async_copy(xp_ref.at[pl.ds(0, 1)], out_hbm.at[pl.ds(0, 1)], sem))


def _dispatch(meta, dest_flat, xn_p, xn_s):
    return pl.pallas_call(
        _dispatch_kernel,
        grid_spec=pltpu.PrefetchScalarGridSpec(
            num_scalar_prefetch=1,
            grid=(N_TILES,),
            in_specs=[
                pl.BlockSpec((TILE_ROWS,), lambda t, meta: (t,), memory_space=pltpu.SMEM),
                pl.BlockSpec((TM, D_MODEL), _prompt_tile),
                pl.BlockSpec((TM, D_MODEL), _sample_tile),
            ],
            out_specs=pl.BlockSpec(memory_space=pl.ANY),
            scratch_shapes=[pltpu.VMEM((EXP_BLK, D_MODEL), F32), pltpu.SemaphoreType.DMA(()),
                            pltpu.SemaphoreType.DMA(())],
        ),
        out_shape=jax.ShapeDtypeStruct((X_ROWS, D_MODEL), F32),
        compiler_params=_cparams(("arbitrary",)),
        name="dispatch",
    )(meta, dest_flat, xn_p, xn_s)


EX_BLKS = 12
EX_ROWS = EX_BLKS * EXP_BLK
EX_CHUNK = 2 * EXP_BLK
EX_GROUP = 4 * EXP_BLK
EX_TF = 256
EX_NF = D_FF // EX_TF
EX_TN = 512
N_ITEMS = (N_BLOCKS + (EX_BLKS - 1) * N_EXPERTS) // EX_BLKS
EXPERTS_VMEM_LIMIT = 56 * 1024 * 1024
assert EX_ROWS // EX_CHUNK < EX_NF


def _experts_kernel(ie_ref, ib_ref, inb_ref, ni_ref, nu_ref,
                    x_hbm, wg_ref, wu_ref, bg_ref, bu_ref, wd_ref, bd_ref, y_hbm,
                    stage_scr, xb_scr, acc_scr, wgb_scr, wub_scr, wdb_scr, xsem, osem, zsem):
    del ie_ref
    w = pl.program_id(0)
    f = pl.program_id(1)
    n_items = ni_ref[0]
    slot = jnp.bitwise_and(w, 1)

    def rows_at(row0, n):
        return pl.ds(pl.multiple_of(row0, EXP_BLK), n)

    def n_chunks(item):
        return lax.shift_right_logical(inb_ref[item] + 1, 1)

    def x_copy(item, c):
        st = jnp.bitwise_and(c, 1)
        return pltpu.make_async_copy(x_hbm.at[rows_at(ib_ref[item] * EXP_BLK + c * EX_CHUNK, EX_CHUNK)],
                                     stage_scr.at[st], xsem.at[st])

    def stage_to_bf16(half, c):
        xb_scr[half, rows_at(c * EX_CHUNK, EX_CHUNK), :] = stage_scr[jnp.bitwise_and(c, 1)].astype(BF16)

    def out_copy(item, row0, m):
        return pltpu.make_async_copy(acc_scr.at[rows_at(row0, m)],
                                     y_hbm.at[rows_at(ib_ref[item] * EXP_BLK + row0, m)], osem)

    def wait_out_copies(item):
        nb = inb_ref[item]
        n_big = lax.shift_right_logical(nb, 2)

        def per_group(c, carry):
            out_copy(item, c * EX_GROUP, EX_GROUP).wait()
            return carry
        lax.fori_loop(0, n_big, per_group, 0)

        @pl.when(jnp.bitwise_and(nb, 2) == 2)
        def _():
            out_copy(item, n_big * EX_GROUP, EX_CHUNK).wait()

        @pl.when(jnp.bitwise_and(nb, 1) == 1)
        def _():
            out_copy(item, (nb - 1) * EXP_BLK, EXP_BLK).wait()

    @pl.when(jnp.logical_and(w == 0, f == 0))
    def _():
        acc_scr[0:EXP_BLK, :] = jnp.zeros((EXP_BLK, D_MODEL), F32)

        def tail(act):
            def body(g, c):
                act(pltpu.make_async_copy(acc_scr.at[pl.ds(0, EXP_BLK)], y_hbm.at[rows_at(g * EXP_BLK, EXP_BLK)],
                                          zsem))
                return c
            lax.fori_loop(nu_ref[0], N_BLOCKS, body, 0)
        tail(lambda cp: cp.start())
        tail(lambda cp: cp.wait())

        def first_rows(c, carry):
            x_copy(0, c).start()
            x_copy(0, c).wait()
            stage_to_bf16(0, c)
            return carry
        lax.fori_loop(0, n_chunks(0), first_rows, 0)

    @pl.when(w < n_items)
    def _():
        nxt = jnp.minimum(w + 1, n_items - 1)
        stages_next = jnp.where(w + 1 < n_items, n_chunks(nxt), 0)

        @pl.when(jnp.logical_and(f >= 1, f - 1 < stages_next))
        def _():
            x_copy(nxt, f - 1).wait()
            stage_to_bf16(1 - slot, f - 1)

        @pl.when(f < stages_next)
        def _():
            x_copy(nxt, f).start()

        wgb_scr[...] = wg_ref[0, 0].astype(BF16)
        wub_scr[...] = wu_ref[0, 0].astype(BF16)
        wdb_scr[...] = wd_ref[0, 0].astype(BF16)

        @pl.when(jnp.logical_and(f == 0, w > 0))
        def _():
            wait_out_copies(w - 1)

        def ffn_rows(row0, m):
            rows = rows_at(row0, m)
            xb = xb_scr[slot, rows, :]
            gg = jnp.minimum(_dot(xb, wgb_scr[...]) + bg_ref[0], SWIGLU_LIMIT)
            uu = jnp.clip(_dot(xb, wub_scr[...]) + bu_ref[0], -SWIGLU_LIMIT, SWIGLU_LIMIT)
            hb = ((uu + 1.0) * (gg * _sigmoid(SWIGLU_ALPHA * gg))).astype(BF16)
            for c in range(D_MODEL // EX_TN):
                cols = slice(c * EX_TN, (c + 1) * EX_TN)
                part = _dot(hb, wdb_scr[:, cols])
                base = jnp.where(f == 0, jnp.broadcast_to(bd_ref[0][:, cols], (m, EX_TN)), acc_scr[rows, cols])
                acc_scr[rows, cols] = base + part

            @pl.when(f == EX_NF - 1)
            def _():
                out_copy(w, row0, m).start()

        nb = inb_ref[w]
        n_big = lax.shift_right_logical(nb, 2)

        def per_group(c, carry):
            ffn_rows(c * EX_GROUP, EX_GROUP)
            return carry
        lax.fori_loop(0, n_big, per_group, 0)

        @pl.when(jnp.bitwise_and(nb, 2) == 2)
        def _():
            ffn_rows(n_big * EX_GROUP, EX_CHUNK)

        @pl.when(jnp.bitwise_and(nb, 1) == 1)
        def _():
            ffn_rows((nb - 1) * EXP_BLK, EXP_BLK)

    @pl.when(jnp.logical_and(w == N_ITEMS - 1, f == EX_NF - 1))
    def _():
        wait_out_copies(n_items - 1)


def _experts(item_e, item_blk, item_nblk, n_items, n_used, x_sorted, w_gu, b_gu, w_down, b_down):
    def item(w, ni):
        return jnp.minimum(w, ni[0] - 1)

    def fsel(w, f, ni):
        return jnp.where(w < ni[0], f, EX_NF - 1)

    return pl.pallas_call(
        _experts_kernel,
        grid_spec=pltpu.PrefetchScalarGridSpec(
            num_scalar_prefetch=5,
            grid=(N_ITEMS, EX_NF),
            in_specs=[
                pl.BlockSpec(memory_space=pl.ANY),
                pl.BlockSpec((1, 1, D_MODEL, EX_TF),
                             lambda w, f, ie, ib, ins, ni, nu: (0, ie[item(w, ni)], 0, fsel(w, f, ni))),
                pl.BlockSpec((1, 1, D_MODEL, EX_TF),
                             lambda w, f, ie, ib, ins, ni, nu: (0, ie[item(w, ni)], 0, EX_NF + fsel(w, f, ni))),
                pl.BlockSpec((1, 1, EX_TF), lambda w, f, ie, ib, ins, ni, nu: (ie[item(w, ni)], 0, fsel(w, f, ni))),
                pl.BlockSpec((1, 1, EX_TF),
                             lambda w, f, ie, ib, ins, ni, nu: (ie[item(w, ni)], 0, EX_NF + fsel(w, f, ni))),
                pl.BlockSpec((1, 1, EX_TF, D_MODEL),
                             lambda w, f, ie, ib, ins, ni, nu: (0, ie[item(w, ni)], fsel(w, f, ni), 0)),
                pl.BlockSpec((1, 1, D_MODEL), lambda w, f, ie, ib, ins, ni, nu: (ie[item(w, ni)], 0, 0)),
            ],
            out_specs=pl.BlockSpec(memory_space=pl.ANY),
            scratch_shapes=[
                pltpu.VMEM((2, EX_CHUNK, D_MODEL), F32),
                pltpu.VMEM((2, EX_ROWS, D_MODEL), BF16),
                pltpu.VMEM((EX_ROWS, D_MODEL), F32),
                pltpu.VMEM((D_MODEL, EX_TF), BF16),
                pltpu.VMEM((D_MODEL, EX_TF), BF16),
                pltpu.VMEM((EX_TF, D_MODEL), BF16),
                pltpu.SemaphoreType.DMA((2,)),
                pltpu.SemaphoreType.DMA(()),
                pltpu.SemaphoreType.DMA(()),
            ],
        ),
        out_shape=jax.ShapeDtypeStruct((N_PAD, D_MODEL), F32),
        compiler_params=pltpu.CompilerParams(dimension_semantics=("arbitrary", "arbitrary"),
                                             vmem_limit_bytes=EXPERTS_VMEM_LIMIT),
        name="experts",
    )(item_e, item_blk, item_nblk, n_items, n_used, x_sorted, w_gu, w_gu, b_gu, b_gu, w_down, b_down)


PL_TN = 512
PL_NJ = D_MODEL // PL_TN
COMBINE_VMEM_LIMIT = 56 * 1024 * 1024


def _combine_final_kernel(dcur_ref, dnext_ref, y_hbm, hp_ref, hs_ref, gate_ref, pp_ref, ps_ref,
                          wp_ref, gp_ref, gg_ref, wg_ref, gf_ref, yp_ref, ys_ref,
                          buf, sems, h_scr, hn_scr, e_scr):
    t = pl.program_id(0)
    j = pl.program_id(1)
    slot = jnp.bitwise_and(t, 1)

    def row_copy(d_ref, sl):
        def make(tok, k):
            return pltpu.make_async_copy(y_hbm.at[pl.ds(d_ref[tok * TOP_K + k], 1)], buf.at[sl, k, pl.ds(tok, 1)],
                                         sems.at[sl])
        return make

    @pl.when(jnp.logical_and(t == 0, j == 0))
    def _():
        _start_token_rows(row_copy(dcur_ref, 0))

    @pl.when(jnp.logical_and(t + 1 < N_TILES, j == 0))
    def _():
        _start_token_rows(row_copy(dnext_ref, 1 - slot))

    @pl.when(j == 0)
    def _():
        _wait_rows(TILE_ROWS,
                   pltpu.make_async_copy(y_hbm.at[pl.ds(0, 1)], buf.at[slot, 0, pl.ds(0, 1)], sems.at[slot]))
        gate = gate_ref[...]
        h2 = jnp.where(t < P_TILES, hp_ref[...], hs_ref[...])
        for k in range(TOP_K):
            h2 = h2 + gate[:, k:k + 1] * buf[slot, k]
        h_scr[...] = h2
        hn_scr[...] = _rms(h2, gg_ref[...]).astype(BF16)
        p = jnp.where(t < P_TILES, pp_ref[...], ps_ref[...])
        e_scr[...] = _rms(_dot(p.astype(BF16), wp_ref[...]), gp_ref[...])

    ple_gate = _sigmoid(_dot(hn_scr[...], wg_ref[...]))
    for jj in range(PL_NJ):
        @pl.when(j == jj)
        def _(jj=jj):
            sl = slice(jj * PL_TN, (jj + 1) * PL_TN)
            h_scr[:, sl] = h_scr[:, sl] + ple_gate * e_scr[:, sl]

    @pl.when(jnp.logical_and(j == PL_NJ - 1, t < P_TILES))
    def _():
        yp_ref[...] = _rms(h_scr[...], gf_ref[...])

    @pl.when(jnp.logical_and(j == PL_NJ - 1, t >= P_TILES))
    def _():
        ys_ref[...] = _rms(h_scr[...], gf_ref[...])


def _combine_final(dest_flat, y_sorted, h_p, h_s, gate, p_p, p_s, w_proj, g_proj, g_gate, w_gate, g_final):
    const = lambda t, j: (0, 0)
    return pl.pallas_call(
        _combine_final_kernel,
        grid=(N_TILES, PL_NJ),
        in_specs=[
            pl.BlockSpec((TILE_ROWS,), lambda t, j: (t,), memory_space=pltpu.SMEM),
            pl.BlockSpec((TILE_ROWS,), lambda t, j: (jnp.minimum(t + 1, N_TILES - 1),), memory_space=pltpu.SMEM),
            pl.BlockSpec(memory_space=pl.ANY),
            pl.BlockSpec((TM, D_MODEL), _prompt_tile),
            pl.BlockSpec((TM, D_MODEL), _sample_tile),
            pl.BlockSpec((TM, LANES), lambda t, j: (t, 0)),
            pl.BlockSpec((TM, D_PLE), _prompt_tile),
            pl.BlockSpec((TM, D_PLE), _sample_tile),
            pl.BlockSpec((D_PLE, D_MODEL), const),
            pl.BlockSpec((1, D_MODEL), const),
            pl.BlockSpec((1, D_MODEL), const),
            pl.BlockSpec((D_MODEL, PL_TN), lambda t, j: (0, j)),
            pl.BlockSpec((1, D_MODEL), const),
        ],
        out_specs=[pl.BlockSpec((TM, D_MODEL), _prompt_tile), pl.BlockSpec((TM, D_MODEL), _sample_tile)],
        out_shape=[jax.ShapeDtypeStruct((N_P, D_MODEL), F32), jax.ShapeDtypeStruct((N_S, D_MODEL), F32)],
        scratch_shapes=[pltpu.VMEM((2, TOP_K, TM, D_MODEL), F32), pltpu.SemaphoreType.DMA((2,)),
                        pltpu.VMEM((TM, D_MODEL), F32), pltpu.VMEM((TM, D_MODEL), BF16),
                        pltpu.VMEM((TM, D_MODEL), F32)],
        compiler_params=pltpu.CompilerParams(dimension_semantics=("arbitrary", "arbitrary"),
                                             vmem_limit_bytes=COMBINE_VMEM_LIMIT),
        name="combine_final",
    )(dest_flat, dest_flat, y_sorted, h_p, h_s, gate, p_p, p_s, w_proj, g_proj, g_gate, w_gate, g_final)


def kernel(x_prompt, x_sample, cache_k, cache_v, cache_logf, state_conv, state_rglru, p_prompt, p_sample,
           g_mix, w_in, b_fgate, conv_w, conv_b, rg_w_a, rg_b_a, rg_w_i, rg_b_i, rg_lambda,
           g_out_rnn, g_out_fox, w_out, g_moe, router_w, router_b, w_gu, b_gu, w_down, b_down,
           w_ple_proj, g_ple_proj, g_ple_gate, w_ple_gate, g_final):
    assert g_mix.shape[0] == 1, "one layer"
    row = lambda v: v.reshape(1, -1)
    lane_pad = lambda a: jnp.pad(a, ((0, 0), (0, LANES - a.shape[1])))

    xp = x_prompt.reshape(N_P, D_MODEL)
    xs = x_sample.reshape(N_S, D_MODEL)

    w_main = w_in[0][:, :D_MAIN].astype(BF16)
    w_f = lane_pad(w_in[0][:, D_MAIN:]).astype(BF16)
    b_f = lane_pad(row(b_fgate[0]))
    proj_p, lf_p, vt_p = _in_proj(xp, TM_IN, row(g_mix[0]), w_main, w_f, b_f, True, "in_proj_prompt")
    proj_s, lf_s = _in_proj(xs, N_S, row(g_mix[0]), w_main, w_f, b_f, False, "in_proj_sample")
    logf_p = lf_p[:, :FOX_HEADS].reshape(BATCH, SEQ, FOX_HEADS)
    logf_s = lf_s[:, :FOX_HEADS].reshape(DEC_BATCH, DEC_SEQ, FOX_HEADS)
    k_p, v_p = proj_p[:, COL_K:COL_V], proj_p[:, COL_V:]
    q_s, k_s, v_s = proj_s[:, COL_Q:COL_K], proj_s[:, COL_K:COL_V], proj_s[:, COL_V:]

    rg = (conv_w[0], row(conv_b[0]), rg_w_a[0].astype(BF16), row(rg_b_a[0]), rg_w_i[0].astype(BF16),
          row(rg_b_i[0]), row(rg_lambda[0]))
    o_rnn_p, conv_p, hl_p = _rglru(proj_p, 0, BATCH, SEQ // TM, TM,
                                   jnp.zeros((BATCH, CONV_W - 1, D_RNN), F32), jnp.zeros((BATCH, 1, D_RNN), F32),
                                   *rg, name="rglru_prompt")
    o_rnn_s, conv_s, hl_s = _rglru(proj_s, 0, DEC_BATCH, 1, DEC_SEQ,
                                   state_conv[0], state_rglru[0].reshape(DEC_BATCH, 1, D_RNN),
                                   *rg, name="rglru_sample")

    c_p = _cumsum_rows(logf_p.transpose(1, 0, 2).reshape(SEQ, BATCH * FOX_HEADS), "cumsum_prompt")
    c_pT = c_p.T
    o_fox_p = _fox_prompt(proj_p, vt_p, c_pT[:, :, None], c_pT[:, None, :])

    lf_all = jnp.concatenate([cache_logf[0], logf_s], axis=1)
    c_s = _cumsum_rows(lf_all.transpose(1, 0, 2).reshape(PAST_LEN + DEC_SEQ, DEC_BATCH * FOX_HEADS),
                       "cumsum_sample")
    c_bth = c_s.reshape(PAST_LEN + DEC_SEQ, DEC_BATCH, FOX_HEADS).transpose(1, 0, 2)
    ck = c_bth.reshape(DEC_BATCH, 1, (PAST_LEN + DEC_SEQ) * FOX_HEADS)
    cq = c_bth[:, PAST_LEN:].reshape(DEC_BATCH, FS_ROWS, 1)
    per_req = lambda a: a.reshape(DEC_BATCH, FS_ROWS, FOX_HEAD_DIM)
    o_fox_s = _fox_sample(per_req(q_s), cache_k, cache_v, per_req(k_s), per_req(v_s), cq, ck)
    o_fox_s = o_fox_s.reshape(N_S, D_FOX)

    rw = lane_pad(router_w[0])
    rw_hi = rw.astype(BF16)
    rw_lo = (rw - rw_hi.astype(F32)).astype(BF16)
    rb = jnp.concatenate([row(router_b[0]), jnp.full((1, LANES - N_EXPERTS), NEG, F32)], axis=1)
    op_w = (row(g_out_rnn[0]), row(g_out_fox[0]), w_out[0].astype(BF16), row(g_moe[0]), rw_hi, rw_lo, rb)
    h1_p, xn_p, logits_p = _out_proj(o_rnn_p, o_fox_p, xp, TM_OUT, *op_w, name="out_proj_prompt")
    h1_s, xn_s, logits_s = _out_proj(o_rnn_s, o_fox_s, xs, N_S, *op_w, name="out_proj_sample")

    ei, gate, cnt = _route(logits_p, logits_s)
    counts = cnt[0, :N_EXPERTS]
    padded = (counts + EXP_BLK - 1) // EXP_BLK * EXP_BLK
    pad_end = jnp.cumsum(padded)
    pad_start = pad_end - padded
    dest = (pad_start[ei[:, :TOP_K]] + ei[:, TOP_K:2 * TOP_K]).reshape(N_ROWS)
    n_used = (pad_end[-1] // EXP_BLK).reshape(1).astype(I32)
    meta = jnp.concatenate([counts, pad_start, pad_end, n_used]).astype(I32)
    x_sorted = _dispatch(meta, dest, xn_p, xn_s)
    blocks_e = padded // EXP_BLK
    items_e = (blocks_e + EX_BLKS - 1) // EX_BLKS
    item_end = jnp.cumsum(items_e)
    item_first = item_end - items_e
    item_id = jnp.arange(N_ITEMS, dtype=I32)
    item_e = jnp.minimum(jnp.sum((item_end[None, :] <= item_id[:, None]).astype(I32), axis=1), N_EXPERTS - 1)
    item_run = item_id - item_first[item_e]
    item_blk = jnp.clip(pad_start[item_e] // EXP_BLK + EX_BLKS * item_run, 0, N_BLOCKS - 1).astype(I32)
    item_nblk = jnp.clip(blocks_e[item_e] - EX_BLKS * item_run, 1, EX_BLKS).astype(I32)
    n_items = item_end[-1].reshape(1).astype(I32)
    y_sorted = _experts(item_e.astype(I32), item_blk, item_nblk, n_items, n_used, x_sorted,
                        w_gu, b_gu[0][:, None, :], w_down, b_down[0][:, None, :])
    y_p, y_s = _combine_final(dest, y_sorted, h1_p, h1_s, gate,
                              p_prompt[0].reshape(N_P, D_PLE), p_sample[0].reshape(N_S, D_PLE),
                              w_ple_proj[0].astype(BF16), row(g_ple_proj[0]), row(g_ple_gate[0]),
                              w_ple_gate[0].astype(BF16), row(g_final))

    shp = (1, BATCH, SEQ, FOX_HEADS, FOX_HEAD_DIM)
    shs = (1, DEC_BATCH, DEC_SEQ, FOX_HEADS, FOX_HEAD_DIM)
    return (y_p.reshape(BATCH, SEQ, D_MODEL), y_s.reshape(DEC_BATCH, DEC_SEQ, D_MODEL),
            k_p.reshape(shp), v_p.reshape(shp), logf_p[None],
            conv_p[None], hl_p.reshape(1, BATCH, D_RNN),
            k_s.reshape(shs), v_s.reshape(shs), logf_s[None],
            conv_s[None], hl_s.reshape(1, DEC_BATCH, D_RNN))
```

```python
import functools

import jax
import jax.numpy as jnp
from jax import lax
from jax.experimental import pallas as pl
from jax.experimental.pallas import tpu as pltpu

F32 = jnp.float32
BF16 = jnp.bfloat16
I32 = jnp.int32

D_MODEL = 2048
BATCH = 4
SEQ = 2048
DEC_BATCH = 16
DEC_SEQ = 16
PAST_LEN = 4096
D_RNN = 1024
RG_BLOCKS = 8
RG_BLOCK_W = 128
CONV_W = 4
RG_C = 8.0
FOX_HEADS = 8
FOX_HEAD_DIM = 128
D_FOX = 1024
D_MAIN = 2 * D_RNN + 3 * D_FOX
N_EXPERTS = 32
TOP_K = 4
D_FF = 2048
SWIGLU_LIMIT = 7.0
SWIGLU_ALPHA = 1.702
D_PLE = 256
EPS = 1e-6

LANES = 128
N_P = BATCH * SEQ
N_S = DEC_BATCH * DEC_SEQ
N_TOK = N_P + N_S
TM = 256
TM_IN = 1024
TM_OUT = 512
N_TILES = N_TOK // TM
P_TILES = N_P // TM
N_ROWS = N_TOK * TOP_K
EXP_BLK = 128
N_BLOCKS = (N_ROWS + N_EXPERTS * (EXP_BLK - 1) + EXP_BLK - 1) // EXP_BLK
N_PAD = N_BLOCKS * EXP_BLK
X_ROWS = N_PAD + EXP_BLK
NEG = -1e30
VMEM_LIMIT = 48 * 1024 * 1024


def _cparams(sem):
    return pltpu.CompilerParams(dimension_semantics=sem, vmem_limit_bytes=VMEM_LIMIT)


def _rms(x, g):
    return x * lax.rsqrt(jnp.mean(x * x, axis=-1, keepdims=True) + EPS) * g


def _log_sigmoid(z):
    return jnp.minimum(z, 0.0) - jnp.log1p(jnp.exp(-jnp.abs(z)))


def _sigmoid(z):
    return 1.0 / (1.0 + jnp.exp(-z))


def _expm1(x):
    u = jnp.exp(x)
    degenerate = jnp.logical_or(u == 1.0, u == 0.0)
    val = (u - 1.0) * x / jnp.log(jnp.where(degenerate, 0.5, u))
    return jnp.where(u == 1.0, x, jnp.where(u == 0.0, -1.0, val))


def _gelu_tanh(y):
    return 0.5 * y * (1.0 + jnp.tanh(0.7978845608028654 * (y + 0.044715 * (y * y * y))))


def _dot(a, b):
    return jnp.dot(a, b, preferred_element_type=F32)


def _dot_nt(a, b):
    return lax.dot_general(a, b, (((1,), (1,)), ((), ())), preferred_element_type=F32)


IN_TN = 512
IN_NJ = D_MAIN // IN_TN
COL_Q = 2 * D_RNN
COL_K = COL_Q + D_FOX
COL_V = COL_K + D_FOX
J_V = COL_V // IN_TN


def _in_proj_kernel(x_ref, g_ref, w_ref, wf_ref, bf_ref, proj_ref, logf_ref, *rest, with_vt):
    j = pl.program_id(1)
    xn_scr = rest[-1]

    @pl.when(j == 0)
    def _():
        xb = _rms(x_ref[...], g_ref[...]).astype(BF16)
        xn_scr[...] = xb
        logf_ref[...] = _log_sigmoid(_dot(xb, wf_ref[...]) + bf_ref[...])

    r = _dot(xn_scr[...], w_ref[...])
    proj_ref[...] = r
    if with_vt:
        vt_ref = rest[0]

        @pl.when(j >= J_V)
        def _():
            vt_ref[...] = r.T


def _in_proj(x, tm, g_mix, w_main, w_f, b_f, with_vt, name):
    n = x.shape[0]
    row = lambda i, j: (i, 0)
    const = lambda i, j: (0, 0)
    out_specs = [pl.BlockSpec((tm, IN_TN), lambda i, j: (i, j)), pl.BlockSpec((tm, LANES), row)]
    out_shape = [jax.ShapeDtypeStruct((n, D_MAIN), F32), jax.ShapeDtypeStruct((n, LANES), F32)]
    if with_vt:
        out_specs.append(pl.BlockSpec((IN_TN, tm), lambda i, j: (jnp.maximum(j - J_V, 0), i)))
        out_shape.append(jax.ShapeDtypeStruct((D_FOX, n), F32))
    return pl.pallas_call(
        functools.partial(_in_proj_kernel, with_vt=with_vt),
        grid=(n // tm, IN_NJ),
        in_specs=[
            pl.BlockSpec((tm, D_MODEL), row),
            pl.BlockSpec((1, D_MODEL), const),
            pl.BlockSpec((D_MODEL, IN_TN), lambda i, j: (0, j)),
            pl.BlockSpec((D_MODEL, LANES), const),
            pl.BlockSpec((1, LANES), const),
        ],
        out_specs=out_specs,
        out_shape=out_shape,
        scratch_shapes=[pltpu.VMEM((tm, D_MODEL), BF16)],
        compiler_params=_cparams(("arbitrary", "arbitrary")),
        name=name,
    )(x, g_mix, w_main, w_f, b_f)


def _prefix_steps(n):
    s = 1
    while s < n:
        yield s
        s *= 2


def _rglru_kernel(x_ref, y_ref, past_ref, h0_ref, cw_ref, cb_ref, wa_ref, ba_ref, wi_ref, bi_ref, lam_ref,
                  o_ref, conv_ref, hlast_ref, xp_scr, h_scr, *, tt, nt):
    t = pl.program_id(1)
    pad = 8

    @pl.when(t == 0)
    def _():
        xp_scr[pad - (CONV_W - 1):pad, :] = past_ref[0]
        h_scr[...] = h0_ref[0]

    xp_scr[pad:pad + tt, :] = x_ref[...]
    xc = cb_ref[...] + xp_scr[pad - 3:pad - 3 + tt, :] * cw_ref[0:1, :]
    for jw in range(1, CONV_W):
        xc = xc + xp_scr[pad - 3 + jw:pad - 3 + jw + tt, :] * cw_ref[jw:jw + 1, :]

    r_parts, i_parts = [], []
    for n in range(RG_BLOCKS):
        xb = xc[:, n * RG_BLOCK_W:(n + 1) * RG_BLOCK_W].astype(BF16)
        r_parts.append(_dot(xb, wa_ref[n]))
        i_parts.append(_dot(xb, wi_ref[n]))
    r = _sigmoid(jnp.concatenate(r_parts, axis=-1) + ba_ref[...])
    ig = _sigmoid(jnp.concatenate(i_parts, axis=-1) + bi_ref[...])
    log_a = RG_C * r * _log_sigmoid(lam_ref[...])
    a = jnp.exp(log_a)
    b = jnp.sqrt(-_expm1(2.0 * log_a)) * (ig * xc)

    rowi = lax.broadcasted_iota(I32, (tt, D_RNN), 0)
    for s in _prefix_steps(tt):
        keep = rowi >= s
        a_sh = jnp.where(keep, pltpu.roll(a, s, 0), 1.0)
        b_sh = jnp.where(keep, pltpu.roll(b, s, 0), 0.0)
        b = a * b_sh + b
        a = a * a_sh
    h = a * h_scr[...] + b
    h_scr[...] = h[tt - 1:tt, :]
    o_ref[...] = h * _gelu_tanh(y_ref[...])

    tail = xp_scr[pad + tt - (CONV_W - 1):pad + tt, :]
    xp_scr[pad - (CONV_W - 1):pad, :] = tail

    @pl.when(t == nt - 1)
    def _():
        conv_ref[0] = tail
        hlast_ref[0] = h[tt - 1:tt, :]


def _rglru(xy, row0, nb, nt, tt, conv_past, h0, cw, cb, wa, ba, wi, bi, lam, name):
    blk0 = row0 // tt
    vec = lambda b, t: (0, 0)
    w3 = lambda b, t: (0, 0, 0)
    kern = functools.partial(_rglru_kernel, tt=tt, nt=nt)
    return pl.pallas_call(
        kern,
        grid=(nb, nt),
        in_specs=[
            pl.BlockSpec((tt, D_RNN), lambda b, t: (blk0 + b * nt + t, 0)),
            pl.BlockSpec((tt, D_RNN), lambda b, t: (blk0 + b * nt + t, 1)),
            pl.BlockSpec((1, CONV_W - 1, D_RNN), lambda b, t: (b, 0, 0)),
            pl.BlockSpec((1, 1, D_RNN), lambda b, t: (b, 0, 0)),
            pl.BlockSpec((CONV_W, D_RNN), vec),
            pl.BlockSpec((1, D_RNN), vec),
            pl.BlockSpec((RG_BLOCKS, RG_BLOCK_W, RG_BLOCK_W), w3),
            pl.BlockSpec((1, D_RNN), vec),
            pl.BlockSpec((RG_BLOCKS, RG_BLOCK_W, RG_BLOCK_W), w3),
            pl.BlockSpec((1, D_RNN), vec),
            pl.BlockSpec((1, D_RNN), vec),
        ],
        out_specs=[
            pl.BlockSpec((tt, D_RNN), lambda b, t: (b * nt + t, 0)),
            pl.BlockSpec((1, CONV_W - 1, D_RNN), lambda b, t: (b, 0, 0)),
            pl.BlockSpec((1, 1, D_RNN), lambda b, t: (b, 0, 0)),
        ],
        out_shape=[
            jax.ShapeDtypeStruct((nb * nt * tt, D_RNN), F32),
            jax.ShapeDtypeStruct((nb, CONV_W - 1, D_RNN), F32),
            jax.ShapeDtypeStruct((nb, 1, D_RNN), F32),
        ],
        scratch_shapes=[pltpu.VMEM((tt + 8, D_RNN), F32), pltpu.VMEM((1, D_RNN), F32)],
        compiler_params=_cparams(("arbitrary", "arbitrary")),
        name=name,
    )(xy, xy, conv_past, h0, cw, cb, wa, ba, wi, bi, lam)


def _cumsum_kernel(x_ref, o_ref, *, n):
    x = x_ref[...]
    rowi = lax.broadcasted_iota(I32, x.shape, 0)
    for s in _prefix_steps(n):
        x = x + jnp.where(rowi >= s, pltpu.roll(x, s, 0), 0.0)
    o_ref[...] = x


def _cumsum_rows(x, name):
    return pl.pallas_call(
        functools.partial(_cumsum_kernel, n=x.shape[0]),
        out_shape=jax.ShapeDtypeStruct(x.shape, F32),
        compiler_params=pltpu.CompilerParams(vmem_limit_bytes=VMEM_LIMIT),
        name=name,
    )(x)


FP_T = 512
FP_NT = SEQ // FP_T
FOX_SCALE = FOX_HEAD_DIM ** -0.5


FP_HB = 4
FP_W = FP_HB * FOX_HEAD_DIM
FP_GROUPS = FOX_HEADS // FP_HB


def _fox_prompt_kernel(q_ref, k_ref, vt_ref, cq_ref, ck_ref, o_ref, m_scr, l_scr, acc_scr):
    qi = pl.program_id(1)
    ki = pl.program_id(2)

    @pl.when(ki == 0)
    def _():
        m_scr[...] = jnp.full(m_scr.shape, NEG, F32)
        l_scr[...] = jnp.zeros(l_scr.shape, F32)
        acc_scr[...] = jnp.zeros(acc_scr.shape, F32)

    def update(diagonal):
        for hh in range(FP_HB):
            cols = slice(hh * FOX_HEAD_DIM, (hh + 1) * FOX_HEAD_DIM)
            st = _dot_nt(k_ref[:, cols].astype(BF16), q_ref[:, cols].astype(BF16)) * FOX_SCALE
            st = st + (cq_ref[hh] - ck_ref[hh])
            if diagonal:
                key = lax.broadcasted_iota(I32, (FP_T, FP_T), 0)
                qry = lax.broadcasted_iota(I32, (FP_T, FP_T), 1)
                st = jnp.where(key <= qry, st, NEG)
            m_old = m_scr[hh]
            m_new = jnp.maximum(m_old, jnp.max(st, axis=0, keepdims=True))
            alpha = jnp.exp(m_old - m_new)
            p = jnp.exp(st - m_new)
            l_scr[hh] = alpha * l_scr[hh] + jnp.sum(p, axis=0, keepdims=True)
            acc_scr[hh] = alpha * acc_scr[hh] + _dot(vt_ref[cols, :].astype(BF16), p.astype(BF16))
            m_scr[hh] = m_new

    @pl.when(ki < qi)
    def _():
        update(False)

    @pl.when(ki == qi)
    def _():
        update(True)

    @pl.when(ki == FP_NT - 1)
    def _():
        for hh in range(FP_HB):
            cols = slice(hh * FOX_HEAD_DIM, (hh + 1) * FOX_HEAD_DIM)
            o_ref[:, cols] = (acc_scr[hh] / l_scr[hh]).T


def _fox_prompt(proj_p, v_t, c_col, c_row):
    def rows(g, blk):
        return (g // FP_GROUPS) * FP_NT + blk

    def colblk(g, col0):
        return col0 // FP_W + g % FP_GROUPS

    seen = lambda qi, ki: jnp.minimum(ki, qi)
    return pl.pallas_call(
        _fox_prompt_kernel,
        grid=(BATCH * FP_GROUPS, FP_NT, FP_NT),
        in_specs=[
            pl.BlockSpec((FP_T, FP_W), lambda g, qi, ki: (rows(g, qi), colblk(g, COL_Q))),
            pl.BlockSpec((FP_T, FP_W), lambda g, qi, ki: (rows(g, seen(qi, ki)), colblk(g, COL_K))),
            pl.BlockSpec((FP_W, FP_T), lambda g, qi, ki: (g % FP_GROUPS, rows(g, seen(qi, ki)))),
            pl.BlockSpec((FP_HB, 1, FP_T), lambda g, qi, ki: (g, 0, qi)),
            pl.BlockSpec((FP_HB, FP_T, 1), lambda g, qi, ki: (g, seen(qi, ki), 0)),
        ],
        out_specs=pl.BlockSpec((FP_T, FP_W), lambda g, qi, ki: (rows(g, qi), g % FP_GROUPS)),
        out_shape=jax.ShapeDtypeStruct((N_P, D_FOX), F32),
        scratch_shapes=[pltpu.VMEM((FP_HB, 1, FP_T), F32), pltpu.VMEM((FP_HB, 1, FP_T), F32),
                        pltpu.VMEM((FP_HB, FOX_HEAD_DIM, FP_T), F32)],
        compiler_params=_cparams(("arbitrary", "arbitrary", "arbitrary")),
        name="fox_prompt",
    )(proj_p, proj_p, v_t, c_row, c_col)


FS_TK = 1024
FS_NT = PAST_LEN // FS_TK
FS_ROWS = DEC_SEQ * FOX_HEADS


FS_COLS = FS_TK * FOX_HEADS


def _fox_sample_kernel(q_ref, kc_ref, vc_ref, kn_ref, vn_ref, cq_ref, ckp_ref, ckn_ref, o_ref,
                       m_scr, l_scr, acc_scr):
    kt = pl.program_id(1)

    @pl.when(kt == 0)
    def _():
        m_scr[...] = jnp.full(m_scr.shape, NEG, F32)
        l_scr[...] = jnp.zeros(l_scr.shape, F32)
        acc_scr[...] = jnp.zeros(acc_scr.shape, F32)

    qb = q_ref[0].astype(BF16)

    def step(k2, v2, ck, causal):
        n = k2.shape[0]
        s = _dot_nt(qb, k2.astype(BF16)) * FOX_SCALE + (cq_ref[0] - ck)
        rowi = lax.broadcasted_iota(I32, (FS_ROWS, n), 0)
        coli = lax.broadcasted_iota(I32, (FS_ROWS, n), 1)
        keep = jnp.bitwise_and(coli, FOX_HEADS - 1) == jnp.bitwise_and(rowi, FOX_HEADS - 1)
        s = jnp.where(keep, s, NEG)
        if causal:
            s = jnp.where(jnp.right_shift(coli, 3) <= jnp.right_shift(rowi, 3), s, NEG)
        m_old = m_scr[...]
        m_new = jnp.maximum(m_old, jnp.max(s, axis=-1, keepdims=True))
        alpha = jnp.exp(m_old - m_new)
        p = jnp.exp(s - m_new)
        l_scr[...] = alpha * l_scr[...] + jnp.sum(p, axis=-1, keepdims=True)
        acc_scr[...] = alpha * acc_scr[...] + _dot(p.astype(BF16), v2.astype(BF16))
        m_scr[...] = m_new

    step(kc_ref[0, 0].reshape(FS_COLS, FOX_HEAD_DIM), vc_ref[0, 0].reshape(FS_COLS, FOX_HEAD_DIM),
         ckp_ref[0], False)

    @pl.when(kt == FS_NT - 1)
    def _():
        step(kn_ref[0], vn_ref[0], ckn_ref[0], True)
        o_ref[0] = acc_scr[...] / l_scr[...]


def _fox_sample(q_s, cache_k, cache_v, k_n, v_n, cq, ck):
    per_req = lambda b, kt: (b, 0, 0)
    cache = lambda b, kt: (0, b, kt, 0, 0)
    return pl.pallas_call(
        _fox_sample_kernel,
        grid=(DEC_BATCH, FS_NT),
        in_specs=[
            pl.BlockSpec((1, FS_ROWS, FOX_HEAD_DIM), per_req),
            pl.BlockSpec((1, 1, FS_TK, FOX_HEADS, FOX_HEAD_DIM), cache),
            pl.BlockSpec((1, 1, FS_TK, FOX_HEADS, FOX_HEAD_DIM), cache),
            pl.BlockSpec((1, FS_ROWS, FOX_HEAD_DIM), per_req),
            pl.BlockSpec((1, FS_ROWS, FOX_HEAD_DIM), per_req),
            pl.BlockSpec((1, FS_ROWS, 1), per_req),
            pl.BlockSpec((1, 1, FS_COLS), lambda b, kt: (b, 0, kt)),
            pl.BlockSpec((1, 1, FS_ROWS), lambda b, kt: (b, 0, PAST_LEN * FOX_HEADS // FS_ROWS)),
        ],
        out_specs=pl.BlockSpec((1, FS_ROWS, FOX_HEAD_DIM), per_req),
        out_shape=jax.ShapeDtypeStruct((DEC_BATCH, FS_ROWS, FOX_HEAD_DIM), F32),
        scratch_shapes=[pltpu.VMEM((FS_ROWS, 1), F32), pltpu.VMEM((FS_ROWS, 1), F32),
                        pltpu.VMEM((FS_ROWS, FOX_HEAD_DIM), F32)],
        compiler_params=_cparams(("arbitrary", "arbitrary")),
        name="fox_sample",
    )(q_s, cache_k, cache_v, k_n, v_n, cq, ck, ck)


OP_TN = 512
OP_NJ = D_MODEL // OP_TN


def _out_proj_kernel(orn_ref, ofx_ref, x_ref, g1_ref, g2_ref, w_ref, gm_ref,
                     rwh_ref, rwl_ref, rb_ref, h_ref, xn_ref, lg_ref, mix_scr):
    j = pl.program_id(1)

    @pl.when(j == 0)
    def _():
        mix_scr[:, :D_RNN] = _rms(orn_ref[...], g1_ref[...]).astype(BF16)
        mix_scr[:, D_RNN:] = _rms(ofx_ref[...], g2_ref[...]).astype(BF16)

    r = _dot(mix_scr[...], w_ref[...])
    for jj in range(OP_NJ):
        @pl.when(j == jj)
        def _(jj=jj):
            h_ref[:, jj * OP_TN:(jj + 1) * OP_TN] = x_ref[:, jj * OP_TN:(jj + 1) * OP_TN] + r

    @pl.when(j == OP_NJ - 1)
    def _():
        xn = _rms(h_ref[...], gm_ref[...])
        xn_ref[...] = xn
        xh = xn.astype(BF16)
        xl = (xn - xh.astype(F32)).astype(BF16)
        lg_ref[...] = (_dot(xh, rwh_ref[...]) + _dot(xl, rwh_ref[...]) + _dot(xh, rwl_ref[...])) + rb_ref[...]


def _out_proj(o_rnn, o_fox, x, tm, g1, g2, w_out, g_moe, rw_hi, rw_lo, rb, name):
    n = x.shape[0]
    row = lambda i, j: (i, 0)
    const = lambda i, j: (0, 0)
    return pl.pallas_call(
        _out_proj_kernel,
        grid=(n // tm, OP_NJ),
        in_specs=[
            pl.BlockSpec((tm, D_RNN), row),
            pl.BlockSpec((tm, D_FOX), row),
            pl.BlockSpec((tm, D_MODEL), row),
            pl.BlockSpec((1, D_RNN), const),
            pl.BlockSpec((1, D_FOX), const),
            pl.BlockSpec((D_MODEL, OP_TN), lambda i, j: (0, j)),
            pl.BlockSpec((1, D_MODEL), const),
            pl.BlockSpec((D_MODEL, LANES), const),
            pl.BlockSpec((D_MODEL, LANES), const),
            pl.BlockSpec((1, LANES), const),
        ],
        out_specs=[
            pl.BlockSpec((tm, D_MODEL), row),
            pl.BlockSpec((tm, D_MODEL), row),
            pl.BlockSpec((tm, LANES), row),
        ],
        out_shape=[
            jax.ShapeDtypeStruct((n, D_MODEL), F32),
            jax.ShapeDtypeStruct((n, D_MODEL), F32),
            jax.ShapeDtypeStruct((n, LANES), F32),
        ],
        scratch_shapes=[pltpu.VMEM((tm, D_MODEL), BF16)],
        compiler_params=_cparams(("arbitrary", "arbitrary")),
        name=name,
    )(o_rnn, o_fox, x, g1, g2, w_out, g_moe, rw_hi, rw_lo, rb)


def _route_kernel(lgp_ref, lgs_ref, ei_ref, gate_ref, cnt_ref, carry_scr):
    t = pl.program_id(0)

    @pl.when(t == 0)
    def _():
        carry_scr[...] = jnp.zeros(carry_scr.shape, F32)

    lane = lax.broadcasted_iota(I32, (TM, LANES), 1)
    lane_f = lane.astype(F32)
    work = jnp.where(t < P_TILES, lgp_ref[...], lgs_ref[...])
    tops, idxs, hots = [], [], []
    for _ in range(TOP_K):
        m = jnp.max(work, axis=-1, keepdims=True)
        idx_f = jnp.min(jnp.where(work == m, lane_f, float(LANES)), axis=-1, keepdims=True)
        hot = lane_f == idx_f
        work = jnp.where(hot, -jnp.inf, work)
        tops.append(m)
        idxs.append(idx_f.astype(I32))
        hots.append(hot)

    es = [jnp.exp(tv - tops[0]) for tv in tops]
    denom = es[0] + es[1] + es[2] + es[3]
    gate = jnp.zeros((TM, LANES), F32)
    for k in range(TOP_K):
        gate = jnp.where(lane == k, es[k] / denom, gate)
    gate_ref[...] = gate

    multi = jnp.zeros((TM, LANES), F32)
    for k in range(TOP_K):
        multi = jnp.where(hots[k], 1.0, multi)
    r_i = lax.broadcasted_iota(I32, (TM, TM), 0)
    c_i = lax.broadcasted_iota(I32, (TM, TM), 1)
    strict_lower = jnp.where(c_i < r_i, 1.0, 0.0).astype(BF16)
    before = _dot(strict_lower, multi.astype(BF16)) + carry_scr[...]
    ei = jnp.zeros((TM, LANES), I32)
    for k in range(TOP_K):
        rank = jnp.sum(jnp.where(hots[k], before, 0.0), axis=-1, keepdims=True).astype(I32)
        ei = jnp.where(lane == k, idxs[k], ei)
        ei = jnp.where(lane == TOP_K + k, rank, ei)
    ei_ref[...] = ei
    carry_scr[...] = carry_scr[...] + jnp.sum(multi, axis=0, keepdims=True)
    cnt_ref[...] = carry_scr[...].astype(I32)


def _prompt_tile(t, *_):
    return (jnp.minimum(t, P_TILES - 1), 0)


def _sample_tile(t, *_):
    return (jnp.maximum(t - P_TILES, 0), 0)


def _route(logits_p, logits_s):
    row = lambda t: (t, 0)
    return pl.pallas_call(
        _route_kernel,
        grid=(N_TILES,),
        in_specs=[pl.BlockSpec((TM, LANES), _prompt_tile), pl.BlockSpec((TM, LANES), _sample_tile)],
        out_specs=[pl.BlockSpec((TM, LANES), row), pl.BlockSpec((TM, LANES), row),
                   pl.BlockSpec((1, LANES), lambda t: (0, 0))],
        out_shape=[jax.ShapeDtypeStruct((N_TOK, LANES), I32), jax.ShapeDtypeStruct((N_TOK, LANES), F32),
                   jax.ShapeDtypeStruct((1, LANES), I32)],
        scratch_shapes=[pltpu.VMEM((1, LANES), F32)],
        compiler_params=_cparams(("arbitrary",)),
        name="route",
    )(logits_p, logits_s)


TILE_ROWS = TM * TOP_K
DMA_GROUP = 8
WAIT_GROUP = 32


def _start_token_rows(make_copy):
    toks_per_trip = DMA_GROUP // TOP_K

    def body(ti, c):
        for u in range(toks_per_trip):
            for k in range(TOP_K):
                make_copy(ti * toks_per_trip + u, k).start(priority=k % 2)
        return c
    lax.fori_loop(0, TM // toks_per_trip, body, 0)


def _wait_rows(n, copy):
    def body(gi, c):
        for _ in range(WAIT_GROUP):
            copy.wait()
        return c
    lax.fori_loop(0, n // WAIT_GROUP, body, 0)


def _dispatch_kernel(meta_ref, dest_ref, xp_ref, xs_ref, out_hbm, zero_scr, sem, zsem):
    t = pl.program_id(0)

    def scatter(x_ref):
        def row_copy(tok, k):
            return pltpu.make_async_copy(x_ref.at[pl.ds(tok, 1)], out_hbm.at[pl.ds(dest_ref[tok * TOP_K + k], 1)],
                                         sem)
        _start_token_rows(row_copy)

    @pl.when(t < P_TILES)
    def _():
        scatter(xp_ref)

    @pl.when(t >= P_TILES)
    def _():
        scatter(xs_ref)

    @pl.when(t == 0)
    def _():
        zero_scr[...] = jnp.zeros(zero_scr.shape, F32)

        def fill(act):
            def per_row(r, c):
                act(pltpu.make_async_copy(zero_scr.at[pl.ds(0, 1)], out_hbm.at[pl.ds(r, 1)], zsem))
                return c

            def per_expert(e, c):
                lax.fori_loop(meta_ref[N_EXPERTS + e] + meta_ref[e], meta_ref[2 * N_EXPERTS + e], per_row, 0)
                return c
            lax.fori_loop(0, N_EXPERTS, per_expert, 0)

            def per_block(g, c):
                rows = pl.ds(pl.multiple_of(g * EXP_BLK, EXP_BLK), EXP_BLK)
                act(pltpu.make_async_copy(zero_scr, out_hbm.at[rows], zsem))
                return c
            lax.fori_loop(meta_ref[3 * N_EXPERTS], X_ROWS // EXP_BLK, per_block, 0)

        fill(lambda cp: cp.start())
        fill(lambda cp: cp.wait())

    _wait_rows(TILE_ROWS, pltpu.make_async_copy(xp_ref.at[pl.ds(0, 1)], out_hbm.at[pl.ds(0, 1)], sem))


def _dispatch(meta, dest_flat, xn_p, xn_s):
    return pl.pallas_call(
        _dispatch_kernel,
        grid_spec=pltpu.PrefetchScalarGridSpec(
            num_scalar_prefetch=1,
            grid=(N_TILES,),
            in_specs=[
                pl.BlockSpec((TILE_ROWS,), lambda t, meta: (t,), memory_space=pltpu.SMEM),
                pl.BlockSpec((TM, D_MODEL), _prompt_tile),
                pl.BlockSpec((TM, D_MODEL), _sample_tile),
            ],
            out_specs=pl.BlockSpec(memory_space=pl.ANY),
            scratch_shapes=[pltpu.VMEM((EXP_BLK, D_MODEL), F32), pltpu.SemaphoreType.DMA(()),
                            pltpu.SemaphoreType.DMA(())],
        ),
        out_shape=jax.ShapeDtypeStruct((X_ROWS, D_MODEL), F32),
        compiler_params=_cparams(("arbitrary",)),
        name="dispatch",
    )(meta, dest_flat, xn_p, xn_s)


EX_BLKS = 12
EX_ROWS = EX_BLKS * EXP_BLK
EX_CHUNK = 2 * EXP_BLK
EX_GROUP = 4 * EXP_BLK
EX_TF = 256
EX_NF = D_FF // EX_TF
EX_TN = 512
N_ITEMS = (N_BLOCKS + (EX_BLKS - 1) * N_EXPERTS) // EX_BLKS
EXPERTS_VMEM_LIMIT = 56 * 1024 * 1024
assert EX_ROWS // EX_CHUNK < EX_NF


def _experts_kernel(ie_ref, ib_ref, inb_ref, ni_ref, nu_ref,
                    x_hbm, wg_ref, wu_ref, bg_ref, bu_ref, wd_ref, bd_ref, y_hbm,
                    stage_scr, xb_scr, acc_scr, wgb_scr, wub_scr, wdb_scr, xsem, osem, zsem):
    del ie_ref
    w = pl.program_id(0)
    f = pl.program_id(1)
    n_items = ni_ref[0]
    slot = jnp.bitwise_and(w, 1)

    def rows_at(row0, n):
        return pl.ds(pl.multiple_of(row0, EXP_BLK), n)

    def n_chunks(item):
        return lax.shift_right_logical(inb_ref[item] + 1, 1)

    def x_copy(item, c):
        st = jnp.bitwise_and(c, 1)
        return pltpu.make_async_copy(x_hbm.at[rows_at(ib_ref[item] * EXP_BLK + c * EX_CHUNK, EX_CHUNK)],
                                     stage_scr.at[st], xsem.at[st])

    def stage_to_bf16(half, c):
        xb_scr[half, rows_at(c * EX_CHUNK, EX_CHUNK), :] = stage_scr[jnp.bitwise_and(c, 1)].astype(BF16)

    def out_copy(item, row0, m):
        return pltpu.make_async_copy(acc_scr.at[rows_at(row0, m)],
                                     y_hbm.at[rows_at(ib_ref[item] * EXP_BLK + row0, m)], osem)

    def for_row_groups(nb, act):
        n_big = lax.shift_right_logical(nb, 2)
        rem = jnp.bitwise_and(nb, 3)
        n_plain = jnp.maximum(n_big - 1, 0)

        def per_group(c, carry):
            act(c * EX_GROUP, EX_GROUP)
            return carry
        lax.fori_loop(0, n_plain, per_group, 0)

        for r in range(4):
            @pl.when(jnp.logical_and(n_big >= 1, rem == r))
            def _(r=r):
                act(n_plain * EX_GROUP, EX_GROUP + r * EXP_BLK)

        @pl.when(jnp.logical_and(n_big == 0, rem >= 2))
        def _():
            act(n_plain * EX_GROUP, EX_CHUNK)

        @pl.when(jnp.logical_and(n_big == 0, jnp.bitwise_and(rem, 1) == 1))
        def _():
            act((nb - 1) * EXP_BLK, EXP_BLK)

    def wait_out_copies(item):
        for_row_groups(inb_ref[item], lambda row0, m: out_copy(item, row0, m).wait())

    @pl.when(jnp.logical_and(w == 0, f == 0))
    def _():
        acc_scr[0:EXP_BLK, :] = jnp.zeros((EXP_BLK, D_MODEL), F32)

        def tail(act):
            def body(g, c):
                act(pltpu.make_async_copy(acc_scr.at[pl.ds(0, EXP_BLK)], y_hbm.at[rows_at(g * EXP_BLK, EXP_BLK)],
                                          zsem))
                return c
            lax.fori_loop(nu_ref[0], N_BLOCKS, body, 0)
        tail(lambda cp: cp.start())
        tail(lambda cp: cp.wait())

        def first_rows(c, carry):
            x_copy(0, c).start()
            x_copy(0, c).wait()
            stage_to_bf16(0, c)
            return carry
        lax.fori_loop(0, n_chunks(0), first_rows, 0)

    @pl.when(w < n_items)
    def _():
        nxt = jnp.minimum(w + 1, n_items - 1)
        stages_next = jnp.where(w + 1 < n_items, n_chunks(nxt), 0)

        @pl.when(jnp.logical_and(f >= 1, f - 1 < stages_next))
        def _():
            x_copy(nxt, f - 1).wait()
            stage_to_bf16(1 - slot, f - 1)

        @pl.when(f < stages_next)
        def _():
            x_copy(nxt, f).start()

        wgb_scr[...] = wg_ref[0, 0].astype(BF16)
        wub_scr[...] = wu_ref[0, 0].astype(BF16)
        wdb_scr[...] = wd_ref[0, 0].astype(BF16)

        @pl.when(jnp.logical_and(f == 0, w > 0))
        def _():
            wait_out_copies(w - 1)

        def ffn_rows(row0, m):
            rows = rows_at(row0, m)
            xb = xb_scr[slot, rows, :]
            gg = jnp.minimum(_dot(xb, wgb_scr[...]) + bg_ref[0], SWIGLU_LIMIT)
            uu = jnp.clip(_dot(xb, wub_scr[...]) + bu_ref[0], -SWIGLU_LIMIT, SWIGLU_LIMIT)
            hb = ((uu + 1.0) * (gg * _sigmoid(SWIGLU_ALPHA * gg))).astype(BF16)
            for c in range(D_MODEL // EX_TN):
                cols = slice(c * EX_TN, (c + 1) * EX_TN)
                part = _dot(hb, wdb_scr[:, cols])
                base = jnp.where(f == 0, jnp.broadcast_to(bd_ref[0][:, cols], (m, EX_TN)), acc_scr[rows, cols])
                acc_scr[rows, cols] = base + part

            @pl.when(f == EX_NF - 1)
            def _():
                out_copy(w, row0, m).start()

        for_row_groups(inb_ref[w], ffn_rows)

    @pl.when(jnp.logical_and(w == N_ITEMS - 1, f == EX_NF - 1))
    def _():
        wait_out_copies(n_items - 1)


def _experts(item_e, item_blk, item_nblk, n_items, n_used, x_sorted, w_gu, b_gu, w_down, b_down):
    def item(w, ni):
        return jnp.minimum(w, ni[0] - 1)

    def fsel(w, f, ni):
        return jnp.where(w < ni[0], f, EX_NF - 1)

    return pl.pallas_call(
        _experts_kernel,
        grid_spec=pltpu.PrefetchScalarGridSpec(
            num_scalar_prefetch=5,
            grid=(N_ITEMS, EX_NF),
            in_specs=[
                pl.BlockSpec(memory_space=pl.ANY),
                pl.BlockSpec((1, 1, D_MODEL, EX_TF),
                             lambda w, f, ie, ib, ins, ni, nu: (0, ie[item(w, ni)], 0, fsel(w, f, ni))),
                pl.BlockSpec((1, 1, D_MODEL, EX_TF),
                             lambda w, f, ie, ib, ins, ni, nu: (0, ie[item(w, ni)], 0, EX_NF + fsel(w, f, ni))),
                pl.BlockSpec((1, 1, EX_TF), lambda w, f, ie, ib, ins, ni, nu: (ie[item(w, ni)], 0, fsel(w, f, ni))),
                pl.BlockSpec((1, 1, EX_TF),
                             lambda w, f, ie, ib, ins, ni, nu: (ie[item(w, ni)], 0, EX_NF + fsel(w, f, ni))),
                pl.BlockSpec((1, 1, EX_TF, D_MODEL),
                             lambda w, f, ie, ib, ins, ni, nu: (0, ie[item(w, ni)], fsel(w, f, ni), 0)),
                pl.BlockSpec((1, 1, D_MODEL), lambda w, f, ie, ib, ins, ni, nu: (ie[item(w, ni)], 0, 0)),
            ],
            out_specs=pl.BlockSpec(memory_space=pl.ANY),
            scratch_shapes=[
                pltpu.VMEM((2, EX_CHUNK, D_MODEL), F32),
                pltpu.VMEM((2, EX_ROWS, D_MODEL), BF16),
                pltpu.VMEM((EX_ROWS, D_MODEL), F32),
                pltpu.VMEM((D_MODEL, EX_TF), BF16),
                pltpu.VMEM((D_MODEL, EX_TF), BF16),
                pltpu.VMEM((EX_TF, D_MODEL), BF16),
                pltpu.SemaphoreType.DMA((2,)),
                pltpu.SemaphoreType.DMA(()),
                pltpu.SemaphoreType.DMA(()),
            ],
        ),
        out_shape=jax.ShapeDtypeStruct((N_PAD, D_MODEL), F32),
        compiler_params=pltpu.CompilerParams(dimension_semantics=("arbitrary", "arbitrary"),
                                             vmem_limit_bytes=EXPERTS_VMEM_LIMIT),
        name="experts",
    )(item_e, item_blk, item_nblk, n_items, n_used, x_sorted, w_gu, w_gu, b_gu, b_gu, w_down, b_down)


PL_TN = 512
PL_NJ = D_MODEL // PL_TN
COMBINE_VMEM_LIMIT = 56 * 1024 * 1024


def _combine_final_kernel(dcur_ref, dnext_ref, y_hbm, hp_ref, hs_ref, gate_ref, pp_ref, ps_ref,
                          wp_ref, gp_ref, gg_ref, wg_ref, gf_ref, yp_ref, ys_ref,
                          buf, sems, h_scr, hn_scr, e_scr):
    t = pl.program_id(0)
    j = pl.program_id(1)
    slot = jnp.bitwise_and(t, 1)

    def row_copy(d_ref, sl):
        def make(tok, k):
            return pltpu.make_async_copy(y_hbm.at[pl.ds(d_ref[tok * TOP_K + k], 1)], buf.at[sl, k, pl.ds(tok, 1)],
                                         sems.at[sl])
        return make

    @pl.when(jnp.logical_and(t == 0, j == 0))
    def _():
        _start_token_rows(row_copy(dcur_ref, 0))

    @pl.when(jnp.logical_and(t + 1 < N_TILES, j == 0))
    def _():
        _start_token_rows(row_copy(dnext_ref, 1 - slot))

    @pl.when(j == 0)
    def _():
        _wait_rows(TILE_ROWS,
                   pltpu.make_async_copy(y_hbm.at[pl.ds(0, 1)], buf.at[slot, 0, pl.ds(0, 1)], sems.at[slot]))
        gate = gate_ref[...]
        h2 = jnp.where(t < P_TILES, hp_ref[...], hs_ref[...])
        for k in range(TOP_K):
            h2 = h2 + gate[:, k:k + 1] * buf[slot, k]
        h_scr[...] = h2
        hn_scr[...] = _rms(h2, gg_ref[...]).astype(BF16)
        p = jnp.where(t < P_TILES, pp_ref[...], ps_ref[...])
        e_scr[...] = _rms(_dot(p.astype(BF16), wp_ref[...]), gp_ref[...])

    ple_gate = _sigmoid(_dot(hn_scr[...], wg_ref[...]))
    for jj in range(PL_NJ):
        @pl.when(j == jj)
        def _(jj=jj):
            sl = slice(jj * PL_TN, (jj + 1) * PL_TN)
            h_scr[:, sl] = h_scr[:, sl] + ple_gate * e_scr[:, sl]

    @pl.when(jnp.logical_and(j == PL_NJ - 1, t < P_TILES))
    def _():
        yp_ref[...] = _rms(h_scr[...], gf_ref[...])

    @pl.when(jnp.logical_and(j == PL_NJ - 1, t >= P_TILES))
    def _():
        ys_ref[...] = _rms(h_scr[...], gf_ref[...])


def _combine_final(dest_flat, y_sorted, h_p, h_s, gate, p_p, p_s, w_proj, g_proj, g_gate, w_gate, g_final):
    const = lambda t, j: (0, 0)
    return pl.pallas_call(
        _combine_final_kernel,
        grid=(N_TILES, PL_NJ),
        in_specs=[
            pl.BlockSpec((TILE_ROWS,), lambda t, j: (t,), memory_space=pltpu.SMEM),
            pl.BlockSpec((TILE_ROWS,), lambda t, j: (jnp.minimum(t + 1, N_TILES - 1),), memory_space=pltpu.SMEM),
            pl.BlockSpec(memory_space=pl.ANY),
            pl.BlockSpec((TM, D_MODEL), _prompt_tile),
            pl.BlockSpec((TM, D_MODEL), _sample_tile),
            pl.BlockSpec((TM, LANES), lambda t, j: (t, 0)),
            pl.BlockSpec((TM, D_PLE), _prompt_tile),
            pl.BlockSpec((TM, D_PLE), _sample_tile),
            pl.BlockSpec((D_PLE, D_MODEL), const),
            pl.BlockSpec((1, D_MODEL), const),
            pl.BlockSpec((1, D_MODEL), const),
            pl.BlockSpec((D_MODEL, PL_TN), lambda t, j: (0, j)),
            pl.BlockSpec((1, D_MODEL), const),
        ],
        out_specs=[pl.BlockSpec((TM, D_MODEL), _prompt_tile), pl.BlockSpec((TM, D_MODEL), _sample_tile)],
        out_shape=[jax.ShapeDtypeStruct((N_P, D_MODEL), F32), jax.ShapeDtypeStruct((N_S, D_MODEL), F32)],
        scratch_shapes=[pltpu.VMEM((2, TOP_K, TM, D_MODEL), F32), pltpu.SemaphoreType.DMA((2,)),
                        pltpu.VMEM((TM, D_MODEL), F32), pltpu.VMEM((TM, D_MODEL), BF16),
                        pltpu.VMEM((TM, D_MODEL), F32)],
        compiler_params=pltpu.CompilerParams(dimension_semantics=("arbitrary", "arbitrary"),
                                             vmem_limit_bytes=COMBINE_VMEM_LIMIT),
        name="combine_final",
    )(dest_flat, dest_flat, y_sorted, h_p, h_s, gate, p_p, p_s, w_proj, g_proj, g_gate, w_gate, g_final)


def kernel(x_prompt, x_sample, cache_k, cache_v, cache_logf, state_conv, state_rglru, p_prompt, p_sample,
           g_mix, w_in, b_fgate, conv_w, conv_b, rg_w_a, rg_b_a, rg_w_i, rg_b_i, rg_lambda,
           g_out_rnn, g_out_fox, w_out, g_moe, router_w, router_b, w_gu, b_gu, w_down, b_down,
           w_ple_proj, g_ple_proj, g_ple_gate, w_ple_gate, g_final):
    assert g_mix.shape[0] == 1, "one layer"
    row = lambda v: v.reshape(1, -1)
    lane_pad = lambda a: jnp.pad(a, ((0, 0), (0, LANES - a.shape[1])))

    xp = x_prompt.reshape(N_P, D_MODEL)
    xs = x_sample.reshape(N_S, D_MODEL)

    w_main = w_in[0][:, :D_MAIN].astype(BF16)
    w_f = lane_pad(w_in[0][:, D_MAIN:]).astype(BF16)
    b_f = lane_pad(row(b_fgate[0]))
    proj_p, lf_p, vt_p = _in_proj(xp, TM_IN, row(g_mix[0]), w_main, w_f, b_f, True, "in_proj_prompt")
    proj_s, lf_s = _in_proj(xs, N_S, row(g_mix[0]), w_main, w_f, b_f, False, "in_proj_sample")
    logf_p = lf_p[:, :FOX_HEADS].reshape(BATCH, SEQ, FOX_HEADS)
    logf_s = lf_s[:, :FOX_HEADS].reshape(DEC_BATCH, DEC_SEQ, FOX_HEADS)
    k_p, v_p = proj_p[:, COL_K:COL_V], proj_p[:, COL_V:]
    q_s, k_s, v_s = proj_s[:, COL_Q:COL_K], proj_s[:, COL_K:COL_V], proj_s[:, COL_V:]

    rg = (conv_w[0], row(conv_b[0]), rg_w_a[0].astype(BF16), row(rg_b_a[0]), rg_w_i[0].astype(BF16),
          row(rg_b_i[0]), row(rg_lambda[0]))
    o_rnn_p, conv_p, hl_p = _rglru(proj_p, 0, BATCH, SEQ // TM, TM,
                                   jnp.zeros((BATCH, CONV_W - 1, D_RNN), F32), jnp.zeros((BATCH, 1, D_RNN), F32),
                                   *rg, name="rglru_prompt")
    o_rnn_s, conv_s, hl_s = _rglru(proj_s, 0, DEC_BATCH, 1, DEC_SEQ,
                                   state_conv[0], state_rglru[0].reshape(DEC_BATCH, 1, D_RNN),
                                   *rg, name="rglru_sample")

    c_p = _cumsum_rows(logf_p.transpose(1, 0, 2).reshape(SEQ, BATCH * FOX_HEADS), "cumsum_prompt")
    c_pT = c_p.T
    o_fox_p = _fox_prompt(proj_p, vt_p, c_pT[:, :, None], c_pT[:, None, :])

    lf_all = jnp.concatenate([cache_logf[0], logf_s], axis=1)
    c_s = _cumsum_rows(lf_all.transpose(1, 0, 2).reshape(PAST_LEN + DEC_SEQ, DEC_BATCH * FOX_HEADS),
                       "cumsum_sample")
    c_bth = c_s.reshape(PAST_LEN + DEC_SEQ, DEC_BATCH, FOX_HEADS).transpose(1, 0, 2)
    ck = c_bth.reshape(DEC_BATCH, 1, (PAST_LEN + DEC_SEQ) * FOX_HEADS)
    cq = c_bth[:, PAST_LEN:].reshape(DEC_BATCH, FS_ROWS, 1)
    per_req = lambda a: a.reshape(DEC_BATCH, FS_ROWS, FOX_HEAD_DIM)
    o_fox_s = _fox_sample(per_req(q_s), cache_k, cache_v, per_req(k_s), per_req(v_s), cq, ck)
    o_fox_s = o_fox_s.reshape(N_S, D_FOX)

    rw = lane_pad(router_w[0])
    rw_hi = rw.astype(BF16)
    rw_lo = (rw - rw_hi.astype(F32)).astype(BF16)
    rb = jnp.concatenate([row(router_b[0]), jnp.full((1, LANES - N_EXPERTS), NEG, F32)], axis=1)
    op_w = (row(g_out_rnn[0]), row(g_out_fox[0]), w_out[0].astype(BF16), row(g_moe[0]), rw_hi, rw_lo, rb)
    h1_p, xn_p, logits_p = _out_proj(o_rnn_p, o_fox_p, xp, TM_OUT, *op_w, name="out_proj_prompt")
    h1_s, xn_s, logits_s = _out_proj(o_rnn_s, o_fox_s, xs, N_S, *op_w, name="out_proj_sample")

    ei, gate, cnt = _route(logits_p, logits_s)
    counts = cnt[0, :N_EXPERTS]
    padded = (counts + EXP_BLK - 1) // EXP_BLK * EXP_BLK
    pad_end = jnp.cumsum(padded)
    pad_start = pad_end - padded
    dest = (pad_start[ei[:, :TOP_K]] + ei[:, TOP_K:2 * TOP_K]).reshape(N_ROWS)
    n_used = (pad_end[-1] // EXP_BLK).reshape(1).astype(I32)
    meta = jnp.concatenate([counts, pad_start, pad_end, n_used]).astype(I32)
    x_sorted = _dispatch(meta, dest, xn_p, xn_s)
    blocks_e = padded // EXP_BLK
    items_e = (blocks_e + EX_BLKS - 1) // EX_BLKS
    item_end = jnp.cumsum(items_e)
    item_first = item_end - items_e
    item_id = jnp.arange(N_ITEMS, dtype=I32)
    item_e = jnp.minimum(jnp.sum((item_end[None, :] <= item_id[:, None]).astype(I32), axis=1), N_EXPERTS - 1)
    item_run = item_id - item_first[item_e]
    item_blk = jnp.clip(pad_start[item_e] // EXP_BLK + EX_BLKS * item_run, 0, N_BLOCKS - 1).astype(I32)
    item_nblk = jnp.clip(blocks_e[item_e] - EX_BLKS * item_run, 1, EX_BLKS).astype(I32)
    n_items = item_end[-1].reshape(1).astype(I32)
    y_sorted = _experts(item_e.astype(I32), item_blk, item_nblk, n_items, n_used, x_sorted,
                        w_gu, b_gu[0][:, None, :], w_down, b_down[0][:, None, :])
    y_p, y_s = _combine_final(dest, y_sorted, h1_p, h1_s, gate,
                              p_prompt[0].reshape(N_P, D_PLE), p_sample[0].reshape(N_S, D_PLE),
                              w_ple_proj[0].astype(BF16), row(g_ple_proj[0]), row(g_ple_gate[0]),
                              w_ple_gate[0].astype(BF16), row(g_final))

    shp = (1, BATCH, SEQ, FOX_HEADS, FOX_HEAD_DIM)
    shs = (1, DEC_BATCH, DEC_SEQ, FOX_HEADS, FOX_HEAD_DIM)
    return (y_p.reshape(BATCH, SEQ, D_MODEL), y_s.reshape(DEC_BATCH, DEC_SEQ, D_MODEL),
            k_p.reshape(shp), v_p.reshape(shp), logf_p[None],
            conv_p[None], hl_p.reshape(1, BATCH, D_RNN),
            k_s.reshape(shs), v_s.reshape(shs), logf_s[None],
            conv_s[None], hl_s.reshape(1, DEC_BATCH, D_RNN))
```

```python
import functools

import jax
import jax.numpy as jnp
from jax import lax
from jax.experimental import pallas as pl
from jax.experimental.pallas import tpu as pltpu

F32 = jnp.float32
BF16 = jnp.bfloat16
I32 = jnp.int32

D_MODEL = 2048
BATCH = 4
SEQ = 2048
DEC_BATCH = 16
DEC_SEQ = 16
PAST_LEN = 4096
D_RNN = 1024
RG_BLOCKS = 8
RG_BLOCK_W = 128
CONV_W = 4
RG_C = 8.0
FOX_HEADS = 8
FOX_HEAD_DIM = 128
D_FOX = 1024
D_MAIN = 2 * D_RNN + 3 * D_FOX
N_EXPERTS = 32
TOP_K = 4
D_FF = 2048
SWIGLU_LIMIT = 7.0
SWIGLU_ALPHA = 1.702
D_PLE = 256
EPS = 1e-6

LANES = 128
N_P = BATCH * SEQ
N_S = DEC_BATCH * DEC_SEQ
N_TOK = N_P + N_S
TM = 256
TM_IN = 1024
TM_OUT = 512
N_TILES = N_TOK // TM
P_TILES = N_P // TM
N_ROWS = N_TOK * TOP_K
EXP_BLK = 128
N_BLOCKS = (N_ROWS + N_EXPERTS * (EXP_BLK - 1) + EXP_BLK - 1) // EXP_BLK
N_PAD = N_BLOCKS * EXP_BLK
X_ROWS = N_PAD + EXP_BLK
NEG = -1e30
VMEM_LIMIT = 48 * 1024 * 1024


def _cparams(sem):
    return pltpu.CompilerParams(dimension_semantics=sem, vmem_limit_bytes=VMEM_LIMIT)


def _rms(x, g):
    return x * lax.rsqrt(jnp.mean(x * x, axis=-1, keepdims=True) + EPS) * g


def _log_sigmoid(z):
    return jnp.minimum(z, 0.0) - jnp.log1p(jnp.exp(-jnp.abs(z)))


def _sigmoid(z):
    return 1.0 / (1.0 + jnp.exp(-z))


def _expm1(x):
    u = jnp.exp(x)
    degenerate = jnp.logical_or(u == 1.0, u == 0.0)
    val = (u - 1.0) * x / jnp.log(jnp.where(degenerate, 0.5, u))
    return jnp.where(u == 1.0, x, jnp.where(u == 0.0, -1.0, val))


def _gelu_tanh(y):
    return 0.5 * y * (1.0 + jnp.tanh(0.7978845608028654 * (y + 0.044715 * (y * y * y))))


def _dot(a, b):
    return jnp.dot(a, b, preferred_element_type=F32)


def _dot_nt(a, b):
    return lax.dot_general(a, b, (((1,), (1,)), ((), ())), preferred_element_type=F32)


IN_TN = 512
IN_NJ = D_MAIN // IN_TN
COL_Q = 2 * D_RNN
COL_K = COL_Q + D_FOX
COL_V = COL_K + D_FOX
J_V = COL_V // IN_TN


def _in_proj_kernel(x_ref, g_ref, w_ref, wf_ref, bf_ref, proj_ref, logf_ref, *rest, with_vt):
    j = pl.program_id(1)
    xn_scr = rest[-1]

    @pl.when(j == 0)
    def _():
        xb = _rms(x_ref[...], g_ref[...]).astype(BF16)
        xn_scr[...] = xb
        logf_ref[...] = _log_sigmoid(_dot(xb, wf_ref[...]) + bf_ref[...])

    r = _dot(xn_scr[...], w_ref[...])
    proj_ref[...] = r
    if with_vt:
        vt_ref = rest[0]

        @pl.when(j >= J_V)
        def _():
            vt_ref[...] = r.T


def _in_proj(x, tm, g_mix, w_main, w_f, b_f, with_vt, name):
    n = x.shape[0]
    row = lambda i, j: (i, 0)
    const = lambda i, j: (0, 0)
    out_specs = [pl.BlockSpec((tm, IN_TN), lambda i, j: (i, j)), pl.BlockSpec((tm, LANES), row)]
    out_shape = [jax.ShapeDtypeStruct((n, D_MAIN), F32), jax.ShapeDtypeStruct((n, LANES), F32)]
    if with_vt:
        out_specs.append(pl.BlockSpec((IN_TN, tm), lambda i, j: (jnp.maximum(j - J_V, 0), i)))
        out_shape.append(jax.ShapeDtypeStruct((D_FOX, n), F32))
    return pl.pallas_call(
        functools.partial(_in_proj_kernel, with_vt=with_vt),
        grid=(n // tm, IN_NJ),
        in_specs=[
            pl.BlockSpec((tm, D_MODEL), row),
            pl.BlockSpec((1, D_MODEL), const),
            pl.BlockSpec((D_MODEL, IN_TN), lambda i, j: (0, j)),
            pl.BlockSpec((D_MODEL, LANES), const),
            pl.BlockSpec((1, LANES), const),
        ],
        out_specs=out_specs,
        out_shape=out_shape,
        scratch_shapes=[pltpu.VMEM((tm, D_MODEL), BF16)],
        compiler_params=_cparams(("arbitrary", "arbitrary")),
        name=name,
    )(x, g_mix, w_main, w_f, b_f)


def _prefix_steps(n):
    s = 1
    while s < n:
        yield s
        s *= 2


def _rglru_kernel(x_ref, y_ref, past_ref, h0_ref, cw_ref, cb_ref, wa_ref, ba_ref, wi_ref, bi_ref, lam_ref,
                  o_ref, conv_ref, hlast_ref, xp_scr, h_scr, *, tt, nt):
    t = pl.program_id(1)
    pad = 8

    @pl.when(t == 0)
    def _():
        xp_scr[pad - (CONV_W - 1):pad, :] = past_ref[0]
        h_scr[...] = h0_ref[0]

    xp_scr[pad:pad + tt, :] = x_ref[...]
    xc = cb_ref[...] + xp_scr[pad - 3:pad - 3 + tt, :] * cw_ref[0:1, :]
    for jw in range(1, CONV_W):
        xc = xc + xp_scr[pad - 3 + jw:pad - 3 + jw + tt, :] * cw_ref[jw:jw + 1, :]

    r_parts, i_parts = [], []
    for n in range(RG_BLOCKS):
        xb = xc[:, n * RG_BLOCK_W:(n + 1) * RG_BLOCK_W].astype(BF16)
        r_parts.append(_dot(xb, wa_ref[n]))
        i_parts.append(_dot(xb, wi_ref[n]))
    r = _sigmoid(jnp.concatenate(r_parts, axis=-1) + ba_ref[...])
    ig = _sigmoid(jnp.concatenate(i_parts, axis=-1) + bi_ref[...])
    log_a = RG_C * r * _log_sigmoid(lam_ref[...])
    a = jnp.exp(log_a)
    b = jnp.sqrt(-_expm1(2.0 * log_a)) * (ig * xc)

    rowi = lax.broadcasted_iota(I32, (tt, D_RNN), 0)
    for s in _prefix_steps(tt):
        keep = rowi >= s
        a_sh = jnp.where(keep, pltpu.roll(a, s, 0), 1.0)
        b_sh = jnp.where(keep, pltpu.roll(b, s, 0), 0.0)
        b = a * b_sh + b
        a = a * a_sh
    h = a * h_scr[...] + b
    h_scr[...] = h[tt - 1:tt, :]
    o_ref[...] = h * _gelu_tanh(y_ref[...])

    tail = xp_scr[pad + tt - (CONV_W - 1):pad + tt, :]
    xp_scr[pad - (CONV_W - 1):pad, :] = tail

    @pl.when(t == nt - 1)
    def _():
        conv_ref[0] = tail
        hlast_ref[0] = h[tt - 1:tt, :]


def _rglru(xy, row0, nb, nt, tt, conv_past, h0, cw, cb, wa, ba, wi, bi, lam, name):
    blk0 = row0 // tt
    vec = lambda b, t: (0, 0)
    w3 = lambda b, t: (0, 0, 0)
    kern = functools.partial(_rglru_kernel, tt=tt, nt=nt)
    return pl.pallas_call(
        kern,
        grid=(nb, nt),
        in_specs=[
            pl.BlockSpec((tt, D_RNN), lambda b, t: (blk0 + b * nt + t, 0)),
            pl.BlockSpec((tt, D_RNN), lambda b, t: (blk0 + b * nt + t, 1)),
            pl.BlockSpec((1, CONV_W - 1, D_RNN), lambda b, t: (b, 0, 0)),
            pl.BlockSpec((1, 1, D_RNN), lambda b, t: (b, 0, 0)),
            pl.BlockSpec((CONV_W, D_RNN), vec),
            pl.BlockSpec((1, D_RNN), vec),
            pl.BlockSpec((RG_BLOCKS, RG_BLOCK_W, RG_BLOCK_W), w3),
            pl.BlockSpec((1, D_RNN), vec),
            pl.BlockSpec((RG_BLOCKS, RG_BLOCK_W, RG_BLOCK_W), w3),
            pl.BlockSpec((1, D_RNN), vec),
            pl.BlockSpec((1, D_RNN), vec),
        ],
        out_specs=[
            pl.BlockSpec((tt, D_RNN), lambda b, t: (b * nt + t, 0)),
            pl.BlockSpec((1, CONV_W - 1, D_RNN), lambda b, t: (b, 0, 0)),
            pl.BlockSpec((1, 1, D_RNN), lambda b, t: (b, 0, 0)),
        ],
        out_shape=[
            jax.ShapeDtypeStruct((nb * nt * tt, D_RNN), F32),
            jax.ShapeDtypeStruct((nb, CONV_W - 1, D_RNN), F32),
            jax.ShapeDtypeStruct((nb, 1, D_RNN), F32),
        ],
        scratch_shapes=[pltpu.VMEM((tt + 8, D_RNN), F32), pltpu.VMEM((1, D_RNN), F32)],
        compiler_params=_cparams(("arbitrary", "arbitrary")),
        name=name,
    )(xy, xy, conv_past, h0, cw, cb, wa, ba, wi, bi, lam)


def _cumsum_kernel(x_ref, o_ref, *, n):
    x = x_ref[...]
    rowi = lax.broadcasted_iota(I32, x.shape, 0)
    for s in _prefix_steps(n):
        x = x + jnp.where(rowi >= s, pltpu.roll(x, s, 0), 0.0)
    o_ref[...] = x


def _cumsum_rows(x, name):
    return pl.pallas_call(
        functools.partial(_cumsum_kernel, n=x.shape[0]),
        out_shape=jax.ShapeDtypeStruct(x.shape, F32),
        compiler_params=pltpu.CompilerParams(vmem_limit_bytes=VMEM_LIMIT),
        name=name,
    )(x)


FP_T = 512
FP_NT = SEQ // FP_T
FOX_SCALE = FOX_HEAD_DIM ** -0.5


FP_HB = 4
FP_W = FP_HB * FOX_HEAD_DIM
FP_GROUPS = FOX_HEADS // FP_HB


def _fox_prompt_kernel(q_ref, k_ref, vt_ref, cq_ref, ck_ref, o_ref, m_scr, l_scr, acc_scr):
    qi = pl.program_id(1)
    ki = pl.program_id(2)

    @pl.when(ki == 0)
    def _():
        m_scr[...] = jnp.full(m_scr.shape, NEG, F32)
        l_scr[...] = jnp.zeros(l_scr.shape, F32)
        acc_scr[...] = jnp.zeros(acc_scr.shape, F32)

    def update(diagonal):
        for hh in range(FP_HB):
            cols = slice(hh * FOX_HEAD_DIM, (hh + 1) * FOX_HEAD_DIM)
            st = _dot_nt(k_ref[:, cols].astype(BF16), q_ref[:, cols].astype(BF16)) * FOX_SCALE
            st = st + (cq_ref[hh] - ck_ref[hh])
            if diagonal:
                key = lax.broadcasted_iota(I32, (FP_T, FP_T), 0)
                qry = lax.broadcasted_iota(I32, (FP_T, FP_T), 1)
                st = jnp.where(key <= qry, st, NEG)
            m_old = m_scr[hh]
            m_new = jnp.maximum(m_old, jnp.max(st, axis=0, keepdims=True))
            alpha = jnp.exp(m_old - m_new)
            p = jnp.exp(st - m_new)
            l_scr[hh] = alpha * l_scr[hh] + jnp.sum(p, axis=0, keepdims=True)
            acc_scr[hh] = alpha * acc_scr[hh] + _dot(vt_ref[cols, :].astype(BF16), p.astype(BF16))
            m_scr[hh] = m_new

    @pl.when(ki < qi)
    def _():
        update(False)

    @pl.when(ki == qi)
    def _():
        update(True)

    @pl.when(ki == FP_NT - 1)
    def _():
        for hh in range(FP_HB):
            cols = slice(hh * FOX_HEAD_DIM, (hh + 1) * FOX_HEAD_DIM)
            o_ref[:, cols] = (acc_scr[hh] / l_scr[hh]).T


def _fox_prompt(proj_p, v_t, c_col, c_row):
    def rows(g, blk):
        return (g // FP_GROUPS) * FP_NT + blk

    def colblk(g, col0):
        return col0 // FP_W + g % FP_GROUPS

    seen = lambda qi, ki: jnp.minimum(ki, qi)
    return pl.pallas_call(
        _fox_prompt_kernel,
        grid=(BATCH * FP_GROUPS, FP_NT, FP_NT),
        in_specs=[
            pl.BlockSpec((FP_T, FP_W), lambda g, qi, ki: (rows(g, qi), colblk(g, COL_Q))),
            pl.BlockSpec((FP_T, FP_W), lambda g, qi, ki: (rows(g, seen(qi, ki)), colblk(g, COL_K))),
            pl.BlockSpec((FP_W, FP_T), lambda g, qi, ki: (g % FP_GROUPS, rows(g, seen(qi, ki)))),
            pl.BlockSpec((FP_HB, 1, FP_T), lambda g, qi, ki: (g, 0, qi)),
            pl.BlockSpec((FP_HB, FP_T, 1), lambda g, qi, ki: (g, seen(qi, ki), 0)),
        ],
        out_specs=pl.BlockSpec((FP_T, FP_W), lambda g, qi, ki: (rows(g, qi), g % FP_GROUPS)),
        out_shape=jax.ShapeDtypeStruct((N_P, D_FOX), F32),
        scratch_shapes=[pltpu.VMEM((FP_HB, 1, FP_T), F32), pltpu.VMEM((FP_HB, 1, FP_T), F32),
                        pltpu.VMEM((FP_HB, FOX_HEAD_DIM, FP_T), F32)],
        compiler_params=_cparams(("arbitrary", "arbitrary", "arbitrary")),
        name="fox_prompt",
    )(proj_p, proj_p, v_t, c_row, c_col)


FS_TK = 1024
FS_NT = PAST_LEN // FS_TK
FS_ROWS = DEC_SEQ * FOX_HEADS


FS_COLS = FS_TK * FOX_HEADS


def _fox_sample_kernel(q_ref, kc_ref, vc_ref, kn_ref, vn_ref, cq_ref, ckp_ref, ckn_ref, o_ref,
                       m_scr, l_scr, acc_scr, own_scr):
    kt = pl.program_id(1)

    def own_head(n):
        rowi = lax.broadcasted_iota(I32, (FS_ROWS, n), 0)
        coli = lax.broadcasted_iota(I32, (FS_ROWS, n), 1)
        return rowi, coli, jnp.bitwise_and(coli, FOX_HEADS - 1) == jnp.bitwise_and(rowi, FOX_HEADS - 1)

    @pl.when(jnp.logical_and(pl.program_id(0) == 0, kt == 0))
    def _():
        own_scr[...] = jnp.where(own_head(FS_COLS)[2], 0.0, NEG)

    @pl.when(kt == 0)
    def _():
        m_scr[...] = jnp.full(m_scr.shape, NEG, F32)
        l_scr[...] = jnp.zeros(l_scr.shape, F32)
        acc_scr[...] = jnp.zeros(acc_scr.shape, F32)

    qb = q_ref[0].astype(BF16)

    def step(k2, v2, ck, new_keys):
        n = k2.shape[0]
        s = _dot_nt(qb, k2.astype(BF16)) * FOX_SCALE + (cq_ref[0] - ck)
        if new_keys:
            rowi, coli, keep = own_head(n)
            s = jnp.where(keep, s, NEG)
            s = jnp.where(jnp.right_shift(coli, 3) <= jnp.right_shift(rowi, 3), s, NEG)
        else:
            s = s + own_scr[...]
        m_old = m_scr[...]
        m_new = jnp.maximum(m_old, jnp.max(s, axis=-1, keepdims=True))
        alpha = jnp.exp(m_old - m_new)
        p = jnp.exp(s - m_new)
        l_scr[...] = alpha * l_scr[...] + jnp.sum(p, axis=-1, keepdims=True)
        acc_scr[...] = alpha * acc_scr[...] + _dot(p.astype(BF16), v2.astype(BF16))
        m_scr[...] = m_new

    step(kc_ref[0, 0].reshape(FS_COLS, FOX_HEAD_DIM), vc_ref[0, 0].reshape(FS_COLS, FOX_HEAD_DIM),
         ckp_ref[0], False)

    @pl.when(kt == FS_NT - 1)
    def _():
        step(kn_ref[0], vn_ref[0], ckn_ref[0], True)
        o_ref[0] = acc_scr[...] / l_scr[...]


def _fox_sample(q_s, cache_k, cache_v, k_n, v_n, cq, ck):
    per_req = lambda b, kt: (b, 0, 0)
    cache = lambda b, kt: (0, b, kt, 0, 0)
    return pl.pallas_call(
        _fox_sample_kernel,
        grid=(DEC_BATCH, FS_NT),
        in_specs=[
            pl.BlockSpec((1, FS_ROWS, FOX_HEAD_DIM), per_req),
            pl.BlockSpec((1, 1, FS_TK, FOX_HEADS, FOX_HEAD_DIM), cache),
            pl.BlockSpec((1, 1, FS_TK, FOX_HEADS, FOX_HEAD_DIM), cache),
            pl.BlockSpec((1, FS_ROWS, FOX_HEAD_DIM), per_req),
            pl.BlockSpec((1, FS_ROWS, FOX_HEAD_DIM), per_req),
            pl.BlockSpec((1, FS_ROWS, 1), per_req),
            pl.BlockSpec((1, 1, FS_COLS), lambda b, kt: (b, 0, kt)),
            pl.BlockSpec((1, 1, FS_ROWS), lambda b, kt: (b, 0, PAST_LEN * FOX_HEADS // FS_ROWS)),
        ],
        out_specs=pl.BlockSpec((1, FS_ROWS, FOX_HEAD_DIM), per_req),
        out_shape=jax.ShapeDtypeStruct((DEC_BATCH, FS_ROWS, FOX_HEAD_DIM), F32),
        scratch_shapes=[pltpu.VMEM((FS_ROWS, 1), F32), pltpu.VMEM((FS_ROWS, 1), F32),
                        pltpu.VMEM((FS_ROWS, FOX_HEAD_DIM), F32), pltpu.VMEM((FS_ROWS, FS_COLS), F32)],
        compiler_params=_cparams(("arbitrary", "arbitrary")),
        name="fox_sample",
    )(q_s, cache_k, cache_v, k_n, v_n, cq, ck, ck)


OP_TN = 512
OP_NJ = D_MODEL // OP_TN


def _out_proj_kernel(orn_ref, ofx_ref, x_ref, g1_ref, g2_ref, w_ref, gm_ref,
                     rwh_ref, rwl_ref, rb_ref, h_ref, xn_ref, lg_ref, mix_scr):
    j = pl.program_id(1)

    @pl.when(j == 0)
    def _():
        mix_scr[:, :D_RNN] = _rms(orn_ref[...], g1_ref[...]).astype(BF16)
        mix_scr[:, D_RNN:] = _rms(ofx_ref[...], g2_ref[...]).astype(BF16)

    r = _dot(mix_scr[...], w_ref[...])
    for jj in range(OP_NJ):
        @pl.when(j == jj)
        def _(jj=jj):
            h_ref[:, jj * OP_TN:(jj + 1) * OP_TN] = x_ref[:, jj * OP_TN:(jj + 1) * OP_TN] + r

    @pl.when(j == OP_NJ - 1)
    def _():
        xn = _rms(h_ref[...], gm_ref[...])
        xn_ref[...] = xn
        xh = xn.astype(BF16)
        xl = (xn - xh.astype(F32)).astype(BF16)
        lg_ref[...] = (_dot(xh, rwh_ref[...]) + _dot(xl, rwh_ref[...]) + _dot(xh, rwl_ref[...])) + rb_ref[...]


def _out_proj(o_rnn, o_fox, x, tm, g1, g2, w_out, g_moe, rw_hi, rw_lo, rb, name):
    n = x.shape[0]
    row = lambda i, j: (i, 0)
    const = lambda i, j: (0, 0)
    return pl.pallas_call(
        _out_proj_kernel,
        grid=(n // tm, OP_NJ),
        in_specs=[
            pl.BlockSpec((tm, D_RNN), row),
            pl.BlockSpec((tm, D_FOX), row),
            pl.BlockSpec((tm, D_MODEL), row),
            pl.BlockSpec((1, D_RNN), const),
            pl.BlockSpec((1, D_FOX), const),
            pl.BlockSpec((D_MODEL, OP_TN), lambda i, j: (0, j)),
            pl.BlockSpec((1, D_MODEL), const),
            pl.BlockSpec((D_MODEL, LANES), const),
            pl.BlockSpec((D_MODEL, LANES), const),
            pl.BlockSpec((1, LANES), const),
        ],
        out_specs=[
            pl.BlockSpec((tm, D_MODEL), row),
            pl.BlockSpec((tm, D_MODEL), row),
            pl.BlockSpec((tm, LANES), row),
        ],
        out_shape=[
            jax.ShapeDtypeStruct((n, D_MODEL), F32),
            jax.ShapeDtypeStruct((n, D_MODEL), F32),
            jax.ShapeDtypeStruct((n, LANES), F32),
        ],
        scratch_shapes=[pltpu.VMEM((tm, D_MODEL), BF16)],
        compiler_params=_cparams(("arbitrary", "arbitrary")),
        name=name,
    )(o_rnn, o_fox, x, g1, g2, w_out, g_moe, rw_hi, rw_lo, rb)


def _route_kernel(lgp_ref, lgs_ref, ei_ref, gate_ref, cnt_ref, carry_scr):
    t = pl.program_id(0)

    @pl.when(t == 0)
    def _():
        carry_scr[...] = jnp.zeros(carry_scr.shape, F32)

    lane = lax.broadcasted_iota(I32, (TM, LANES), 1)
    lane_f = lane.astype(F32)
    work = jnp.where(t < P_TILES, lgp_ref[...], lgs_ref[...])
    tops, idxs, hots = [], [], []
    for _ in range(TOP_K):
        m = jnp.max(work, axis=-1, keepdims=True)
        idx_f = jnp.min(jnp.where(work == m, lane_f, float(LANES)), axis=-1, keepdims=True)
        hot = lane_f == idx_f
        work = jnp.where(hot, -jnp.inf, work)
        tops.append(m)
        idxs.append(idx_f.astype(I32))
        hots.append(hot)

    es = [jnp.exp(tv - tops[0]) for tv in tops]
    denom = es[0] + es[1] + es[2] + es[3]
    gate = jnp.zeros((TM, LANES), F32)
    for k in range(TOP_K):
        gate = jnp.where(lane == k, es[k] / denom, gate)
    gate_ref[...] = gate

    multi = jnp.zeros((TM, LANES), F32)
    for k in range(TOP_K):
        multi = jnp.where(hots[k], 1.0, multi)
    r_i = lax.broadcasted_iota(I32, (TM, TM), 0)
    c_i = lax.broadcasted_iota(I32, (TM, TM), 1)
    strict_lower = jnp.where(c_i < r_i, 1.0, 0.0).astype(BF16)
    before = _dot(strict_lower, multi.astype(BF16)) + carry_scr[...]
    ei = jnp.zeros((TM, LANES), I32)
    for k in range(TOP_K):
        rank = jnp.sum(jnp.where(hots[k], before, 0.0), axis=-1, keepdims=True).astype(I32)
        ei = jnp.where(lane == k, idxs[k], ei)
        ei = jnp.where(lane == TOP_K + k, rank, ei)
    ei_ref[...] = ei
    carry_scr[...] = carry_scr[...] + jnp.sum(multi, axis=0, keepdims=True)
    cnt_ref[...] = carry_scr[...].astype(I32)


def _prompt_tile(t, *_):
    return (jnp.minimum(t, P_TILES - 1), 0)


def _sample_tile(t, *_):
    return (jnp.maximum(t - P_TILES, 0), 0)


def _route(logits_p, logits_s):
    row = lambda t: (t, 0)
    return pl.pallas_call(
        _route_kernel,
        grid=(N_TILES,),
        in_specs=[pl.BlockSpec((TM, LANES), _prompt_tile), pl.BlockSpec((TM, LANES), _sample_tile)],
        out_specs=[pl.BlockSpec((TM, LANES), row), pl.BlockSpec((TM, LANES), row),
                   pl.BlockSpec((1, LANES), lambda t: (0, 0))],
        out_shape=[jax.ShapeDtypeStruct((N_TOK, LANES), I32), jax.ShapeDtypeStruct((N_TOK, LANES), F32),
                   jax.ShapeDtypeStruct((1, LANES), I32)],
        scratch_shapes=[pltpu.VMEM((1, LANES), F32)],
        compiler_params=_cparams(("arbitrary",)),
        name="route",
    )(logits_p, logits_s)


TILE_ROWS = TM * TOP_K
DMA_GROUP = 8
WAIT_GROUP = 32


def _start_token_rows(make_copy):
    toks_per_trip = DMA_GROUP // TOP_K

    def body(ti, c):
        for u in range(toks_per_trip):
            for k in range(TOP_K):
                make_copy(ti * toks_per_trip + u, k).start(priority=k % 2)
        return c
    lax.fori_loop(0, TM // toks_per_trip, body, 0)


def _wait_rows(n, copy):
    def body(gi, c):
        for _ in range(WAIT_GROUP):
            copy.wait()
        return c
    lax.fori_loop(0, n // WAIT_GROUP, body, 0)


def _dispatch_kernel(meta_ref, dest_ref, xp_ref, xs_ref, out_hbm, zero_scr, sem, zsem):
    t = pl.program_id(0)

    def scatter(x_ref):
        def row_copy(tok, k):
            return pltpu.make_async_copy(x_ref.at[pl.ds(tok, 1)], out_hbm.at[pl.ds(dest_ref[tok * TOP_K + k], 1)],
                                         sem)
        _start_token_rows(row_copy)

    @pl.when(t < P_TILES)
    def _():
        scatter(xp_ref)

    @pl.when(t >= P_TILES)
    def _():
        scatter(xs_ref)

    @pl.when(t == 0)
    def _():
        zero_scr[...] = jnp.zeros(zero_scr.shape, F32)

        def fill(act):
            def per_row(r, c):
                act(pltpu.make_async_copy(zero_scr.at[pl.ds(0, 1)], out_hbm.at[pl.ds(r, 1)], zsem))
                return c

            def per_expert(e, c):
                lax.fori_loop(meta_ref[N_EXPERTS + e] + meta_ref[e], meta_ref[2 * N_EXPERTS + e], per_row, 0)
                return c
            lax.fori_loop(0, N_EXPERTS, per_expert, 0)

            def per_block(g, c):
                rows = pl.ds(pl.multiple_of(g * EXP_BLK, EXP_BLK), EXP_BLK)
                act(pltpu.make_async_copy(zero_scr, out_hbm.at[rows], zsem))
                return c
            lax.fori_loop(meta_ref[3 * N_EXPERTS], X_ROWS // EXP_BLK, per_block, 0)

        fill(lambda cp: cp.start())
        fill(lambda cp: cp.wait())

    _wait_rows(TILE_ROWS, pltpu.make_async_copy(xp_ref.at[pl.ds(0, 1)], out_hbm.at[pl.ds(0, 1)], sem))


def _dispatch(meta, dest_flat, xn_p, xn_s):
    return pl.pallas_call(
        _dispatch_kernel,
        grid_spec=pltpu.PrefetchScalarGridSpec(
            num_scalar_prefetch=1,
            grid=(N_TILES,),
            in_specs=[
                pl.BlockSpec((TILE_ROWS,), lambda t, meta: (t,), memory_space=pltpu.SMEM),
                pl.BlockSpec((TM, D_MODEL), _prompt_tile),
                pl.BlockSpec((TM, D_MODEL), _sample_tile),
            ],
            out_specs=pl.BlockSpec(memory_space=pl.ANY),
            scratch_shapes=[pltpu.VMEM((EXP_BLK, D_MODEL), F32), pltpu.SemaphoreType.DMA(()),
                            pltpu.SemaphoreType.DMA(())],
        ),
        out_shape=jax.ShapeDtypeStruct((X_ROWS, D_MODEL), F32),
        compiler_params=_cparams(("arbitrary",)),
        name="dispatch",
    )(meta, dest_flat, xn_p, xn_s)


EX_BLKS = 12
EX_ROWS = EX_BLKS * EXP_BLK
EX_CHUNK = 2 * EXP_BLK
EX_GROUP = 4 * EXP_BLK
EX_TF = 256
EX_NF = D_FF // EX_TF
EX_TN = 512
N_ITEMS = (N_BLOCKS + (EX_BLKS - 1) * N_EXPERTS) // EX_BLKS
EXPERTS_VMEM_LIMIT = 56 * 1024 * 1024
assert EX_ROWS // EX_CHUNK < EX_NF


def _experts_kernel(ie_ref, ib_ref, inb_ref, ni_ref, nu_ref,
                    x_hbm, wg_ref, wu_ref, bg_ref, bu_ref, wd_ref, bd_ref, y_hbm,
                    stage_scr, xb_scr, acc_scr, wgb_scr, wub_scr, wdb_scr, xsem, osem, zsem):
    del ie_ref
    w = pl.program_id(0)
    f = pl.program_id(1)
    n_items = ni_ref[0]
    slot = jnp.bitwise_and(w, 1)

    def rows_at(row0, n):
        return pl.ds(pl.multiple_of(row0, EXP_BLK), n)

    def n_chunks(item):
        return lax.shift_right_logical(inb_ref[item] + 1, 1)

    def x_copy(item, c):
        st = jnp.bitwise_and(c, 1)
        return pltpu.make_async_copy(x_hbm.at[rows_at(ib_ref[item] * EXP_BLK + c * EX_CHUNK, EX_CHUNK)],
                                     stage_scr.at[st], xsem.at[st])

    def stage_to_bf16(half, c):
        xb_scr[half, rows_at(c * EX_CHUNK, EX_CHUNK), :] = stage_scr[jnp.bitwise_and(c, 1)].astype(BF16)

    def out_copy(item, row0, m):
        return pltpu.make_async_copy(acc_scr.at[rows_at(row0, m)],
                                     y_hbm.at[rows_at(ib_ref[item] * EXP_BLK + row0, m)], osem)

    def for_row_groups(nb, act):
        n_big = lax.shift_right_logical(nb, 2)
        rem = jnp.bitwise_and(nb, 3)
        n_plain = jnp.maximum(n_big - 1, 0)

        def per_group(c, carry):
            act(c * EX_GROUP, EX_GROUP)
            return carry
        lax.fori_loop(0, n_plain, per_group, 0)

        for r in range(4):
            @pl.when(jnp.logical_and(n_big >= 1, rem == r))
            def _(r=r):
                act(n_plain * EX_GROUP, EX_GROUP + r * EXP_BLK)

        @pl.when(jnp.logical_and(n_big == 0, rem >= 2))
        def _():
            act(n_plain * EX_GROUP, EX_CHUNK)

        @pl.when(jnp.logical_and(n_big == 0, jnp.bitwise_and(rem, 1) == 1))
        def _():
            act((nb - 1) * EXP_BLK, EXP_BLK)

    def wait_out_copies(item):
        for_row_groups(inb_ref[item], lambda row0, m: out_copy(item, row0, m).wait())

    @pl.when(jnp.logical_and(w == 0, f == 0))
    def _():
        acc_scr[0:EXP_BLK, :] = jnp.zeros((EXP_BLK, D_MODEL), F32)

        def tail(act):
            def body(g, c):
                act(pltpu.make_async_copy(acc_scr.at[pl.ds(0, EXP_BLK)], y_hbm.at[rows_at(g * EXP_BLK, EXP_BLK)],
                                          zsem))
                return c
            lax.fori_loop(nu_ref[0], N_BLOCKS, body, 0)
        tail(lambda cp: cp.start())
        tail(lambda cp: cp.wait())

        def first_rows(c, carry):
            x_copy(0, c).start()
            x_copy(0, c).wait()
            stage_to_bf16(0, c)
            return carry
        lax.fori_loop(0, n_chunks(0), first_rows, 0)

    @pl.when(w < n_items)
    def _():
        nxt = jnp.minimum(w + 1, n_items - 1)
        stages_next = jnp.where(w + 1 < n_items, n_chunks(nxt), 0)

        @pl.when(jnp.logical_and(f >= 1, f - 1 < stages_next))
        def _():
            x_copy(nxt, f - 1).wait()
            stage_to_bf16(1 - slot, f - 1)

        @pl.when(f < stages_next)
        def _():
            x_copy(nxt, f).start()

        wgb_scr[...] = wg_ref[0, 0].astype(BF16)
        wub_scr[...] = wu_ref[0, 0].astype(BF16)
        wdb_scr[...] = wd_ref[0, 0].astype(BF16)

        @pl.when(jnp.logical_and(f == 0, w > 0))
        def _():
            wait_out_copies(w - 1)

        def ffn_rows(row0, m):
            rows = rows_at(row0, m)
            xb = xb_scr[slot, rows, :]
            gg = jnp.minimum(_dot(xb, wgb_scr[...]) + bg_ref[0], SWIGLU_LIMIT)
            uu = jnp.clip(_dot(xb, wub_scr[...]) + bu_ref[0], -SWIGLU_LIMIT, SWIGLU_LIMIT)
            hb = ((uu + 1.0) * (gg * _sigmoid(SWIGLU_ALPHA * gg))).astype(BF16)
            for c in range(D_MODEL // EX_TN):
                cols = slice(c * EX_TN, (c + 1) * EX_TN)
                part = _dot(hb, wdb_scr[:, cols])
                base = jnp.where(f == 0, jnp.broadcast_to(bd_ref[0][:, cols], (m, EX_TN)), acc_scr[rows, cols])
                acc_scr[rows, cols] = base + part

            @pl.when(f == EX_NF - 1)
            def _():
                out_copy(w, row0, m).start()

        for_row_groups(inb_ref[w], ffn_rows)

    @pl.when(jnp.logical_and(w == N_ITEMS - 1, f == EX_NF - 1))
    def _():
        wait_out_copies(n_items - 1)


def _experts(item_e, item_blk, item_nblk, n_items, n_used, x_sorted, w_gu, b_gu, w_down, b_down):
    def item(w, ni):
        return jnp.minimum(w, ni[0] - 1)

    def fsel(w, f, ni):
        return jnp.where(w < ni[0], f, EX_NF - 1)

    return pl.pallas_call(
        _experts_kernel,
        grid_spec=pltpu.PrefetchScalarGridSpec(
            num_scalar_prefetch=5,
            grid=(N_ITEMS, EX_NF),
            in_specs=[
                pl.BlockSpec(memory_space=pl.ANY),
                pl.BlockSpec((1, 1, D_MODEL, EX_TF),
                             lambda w, f, ie, ib, ins, ni, nu: (0, ie[item(w, ni)], 0, fsel(w, f, ni))),
                pl.BlockSpec((1, 1, D_MODEL, EX_TF),
                             lambda w, f, ie, ib, ins, ni, nu: (0, ie[item(w, ni)], 0, EX_NF + fsel(w, f, ni))),
                pl.BlockSpec((1, 1, EX_TF), lambda w, f, ie, ib, ins, ni, nu: (ie[item(w, ni)], 0, fsel(w, f, ni))),
                pl.BlockSpec((1, 1, EX_TF),
                             lambda w, f, ie, ib, ins, ni, nu: (ie[item(w, ni)], 0, EX_NF + fsel(w, f, ni))),
                pl.BlockSpec((1, 1, EX_TF, D_MODEL),
                             lambda w, f, ie, ib, ins, ni, nu: (0, ie[item(w, ni)], fsel(w, f, ni), 0)),
                pl.BlockSpec((1, 1, D_MODEL), lambda w, f, ie, ib, ins, ni, nu: (ie[item(w, ni)], 0, 0)),
            ],
            out_specs=pl.BlockSpec(memory_space=pl.ANY),
            scratch_shapes=[
                pltpu.VMEM((2, EX_CHUNK, D_MODEL), F32),
                pltpu.VMEM((2, EX_ROWS, D_MODEL), BF16),
                pltpu.VMEM((EX_ROWS, D_MODEL), F32),
                pltpu.VMEM((D_MODEL, EX_TF), BF16),
                pltpu.VMEM((D_MODEL, EX_TF), BF16),
                pltpu.VMEM((EX_TF, D_MODEL), BF16),
                pltpu.SemaphoreType.DMA((2,)),
                pltpu.SemaphoreType.DMA(()),
                pltpu.SemaphoreType.DMA(()),
            ],
        ),
        out_shape=jax.ShapeDtypeStruct((N_PAD, D_MODEL), F32),
        compiler_params=pltpu.CompilerParams(dimension_semantics=("arbitrary", "arbitrary"),
                                             vmem_limit_bytes=EXPERTS_VMEM_LIMIT),
        name="experts",
    )(item_e, item_blk, item_nblk, n_items, n_used, x_sorted, w_gu, w_gu, b_gu, b_gu, w_down, b_down)


PL_TN = 1024
PL_NJ = D_MODEL // PL_TN
COMBINE_VMEM_LIMIT = 56 * 1024 * 1024


def _combine_final_kernel(dcur_ref, dnext_ref, y_hbm, hp_ref, hs_ref, gate_ref, pp_ref, ps_ref,
                          wp_ref, gp_ref, gg_ref, wg_ref, gf_ref, yp_ref, ys_ref,
                          buf, sems, h_scr, hn_scr, e_scr):
    t = pl.program_id(0)
    j = pl.program_id(1)
    slot = jnp.bitwise_and(t, 1)

    def row_copy(d_ref, sl):
        def make(tok, k):
            return pltpu.make_async_copy(y_hbm.at[pl.ds(d_ref[tok * TOP_K + k], 1)], buf.at[sl, k, pl.ds(tok, 1)],
                                         sems.at[sl])
        return make

    @pl.when(jnp.logical_and(t == 0, j == 0))
    def _():
        _start_token_rows(row_copy(dcur_ref, 0))

    @pl.when(jnp.logical_and(t + 1 < N_TILES, j == 0))
    def _():
        _start_token_rows(row_copy(dnext_ref, 1 - slot))

    @pl.when(j == 0)
    def _():
        _wait_rows(TILE_ROWS,
                   pltpu.make_async_copy(y_hbm.at[pl.ds(0, 1)], buf.at[slot, 0, pl.ds(0, 1)], sems.at[slot]))
        gate = gate_ref[...]
        h2 = jnp.where(t < P_TILES, hp_ref[...], hs_ref[...])
        for k in range(TOP_K):
            h2 = h2 + gate[:, k:k + 1] * buf[slot, k]
        h_scr[...] = h2
        hn_scr[...] = _rms(h2, gg_ref[...]).astype(BF16)
        p = jnp.where(t < P_TILES, pp_ref[...], ps_ref[...])
        e_scr[...] = _rms(_dot(p.astype(BF16), wp_ref[...]), gp_ref[...])

    ple_gate = _sigmoid(_dot(hn_scr[...], wg_ref[...]))
    for jj in range(PL_NJ):
        @pl.when(j == jj)
        def _(jj=jj):
            sl = slice(jj * PL_TN, (jj + 1) * PL_TN)
            h_scr[:, sl] = h_scr[:, sl] + ple_gate * e_scr[:, sl]

    @pl.when(jnp.logical_and(j == PL_NJ - 1, t < P_TILES))
    def _():
        yp_ref[...] = _rms(h_scr[...], gf_ref[...])

    @pl.when(jnp.logical_and(j == PL_NJ - 1, t >= P_TILES))
    def _():
        ys_ref[...] = _rms(h_scr[...], gf_ref[...])


def _combine_final(dest_flat, y_sorted, h_p, h_s, gate, p_p, p_s, w_proj, g_proj, g_gate, w_gate, g_final):
    const = lambda t, j: (0, 0)
    return pl.pallas_call(
        _combine_final_kernel,
        grid=(N_TILES, PL_NJ),
        in_specs=[
            pl.BlockSpec((TILE_ROWS,), lambda t, j: (t,), memory_space=pltpu.SMEM),
            pl.BlockSpec((TILE_ROWS,), lambda t, j: (jnp.minimum(t + 1, N_TILES - 1),), memory_space=pltpu.SMEM),
            pl.BlockSpec(memory_space=pl.ANY),
            pl.BlockSpec((TM, D_MODEL), _prompt_tile),
            pl.BlockSpec((TM, D_MODEL), _sample_tile),
            pl.BlockSpec((TM, LANES), lambda t, j: (t, 0)),
            pl.BlockSpec((TM, D_PLE), _prompt_tile),
            pl.BlockSpec((TM, D_PLE), _sample_tile),
            pl.BlockSpec((D_PLE, D_MODEL), const),
            pl.BlockSpec((1, D_MODEL), const),
            pl.BlockSpec((1, D_MODEL), const),
            pl.BlockSpec((D_MODEL, PL_TN), lambda t, j: (0, j)),
            pl.BlockSpec((1, D_MODEL), const),
        ],
        out_specs=[pl.BlockSpec((TM, D_MODEL), _prompt_tile), pl.BlockSpec((TM, D_MODEL), _sample_tile)],
        out_shape=[jax.ShapeDtypeStruct((N_P, D_MODEL), F32), jax.ShapeDtypeStruct((N_S, D_MODEL), F32)],
        scratch_shapes=[pltpu.VMEM((2, TOP_K, TM, D_MODEL), F32), pltpu.SemaphoreType.DMA((2,)),
                        pltpu.VMEM((TM, D_MODEL), F32), pltpu.VMEM((TM, D_MODEL), BF16),
                        pltpu.VMEM((TM, D_MODEL), F32)],
        compiler_params=pltpu.CompilerParams(dimension_semantics=("arbitrary", "arbitrary"),
                                             vmem_limit_bytes=COMBINE_VMEM_LIMIT),
        name="combine_final",
    )(dest_flat, dest_flat, y_sorted, h_p, h_s, gate, p_p, p_s, w_proj, g_proj, g_gate, w_gate, g_final)


def kernel(x_prompt, x_sample, cache_k, cache_v, cache_logf, state_conv, state_rglru, p_prompt, p_sample,
           g_mix, w_in, b_fgate, conv_w, conv_b, rg_w_a, rg_b_a, rg_w_i, rg_b_i, rg_lambda,
           g_out_rnn, g_out_fox, w_out, g_moe, router_w, router_b, w_gu, b_gu, w_down, b_down,
           w_ple_proj, g_ple_proj, g_ple_gate, w_ple_gate, g_final):
    assert g_mix.shape[0] == 1, "one layer"
    row = lambda v: v.reshape(1, -1)
    lane_pad = lambda a: jnp.pad(a, ((0, 0), (0, LANES - a.shape[1])))

    xp = x_prompt.reshape(N_P, D_MODEL)
    xs = x_sample.reshape(N_S, D_MODEL)

    w_main = w_in[0][:, :D_MAIN].astype(BF16)
    w_f = lane_pad(w_in[0][:, D_MAIN:]).astype(BF16)
    b_f = lane_pad(row(b_fgate[0]))
    proj_p, lf_p, vt_p = _in_proj(xp, TM_IN, row(g_mix[0]), w_main, w_f, b_f, True, "in_proj_prompt")
    proj_s, lf_s = _in_proj(xs, N_S, row(g_mix[0]), w_main, w_f, b_f, False, "in_proj_sample")
    logf_p = lf_p[:, :FOX_HEADS].reshape(BATCH, SEQ, FOX_HEADS)
    logf_s = lf_s[:, :FOX_HEADS].reshape(DEC_BATCH, DEC_SEQ, FOX_HEADS)
    k_p, v_p = proj_p[:, COL_K:COL_V], proj_p[:, COL_V:]
    q_s, k_s, v_s = proj_s[:, COL_Q:COL_K], proj_s[:, COL_K:COL_V], proj_s[:, COL_V:]

    rg = (conv_w[0], row(conv_b[0]), rg_w_a[0].astype(BF16), row(rg_b_a[0]), rg_w_i[0].astype(BF16),
          row(rg_b_i[0]), row(rg_lambda[0]))
    o_rnn_p, conv_p, hl_p = _rglru(proj_p, 0, BATCH, SEQ // TM, TM,
                                   jnp.zeros((BATCH, CONV_W - 1, D_RNN), F32), jnp.zeros((BATCH, 1, D_RNN), F32),
                                   *rg, name="rglru_prompt")
    o_rnn_s, conv_s, hl_s = _rglru(proj_s, 0, DEC_BATCH, 1, DEC_SEQ,
                                   state_conv[0], state_rglru[0].reshape(DEC_BATCH, 1, D_RNN),
                                   *rg, name="rglru_sample")

    c_p = _cumsum_rows(logf_p.transpose(1, 0, 2).reshape(SEQ, BATCH * FOX_HEADS), "cumsum_prompt")
    c_pT = c_p.T
    o_fox_p = _fox_prompt(proj_p, vt_p, c_pT[:, :, None], c_pT[:, None, :])

    lf_all = jnp.concatenate([cache_logf[0], logf_s], axis=1)
    c_s = _cumsum_rows(lf_all.transpose(1, 0, 2).reshape(PAST_LEN + DEC_SEQ, DEC_BATCH * FOX_HEADS),
                       "cumsum_sample")
    c_bth = c_s.reshape(PAST_LEN + DEC_SEQ, DEC_BATCH, FOX_HEADS).transpose(1, 0, 2)
    ck = c_bth.reshape(DEC_BATCH, 1, (PAST_LEN + DEC_SEQ) * FOX_HEADS)
    cq = c_bth[:, PAST_LEN:].reshape(DEC_BATCH, FS_ROWS, 1)
    per_req = lambda a: a.reshape(DEC_BATCH, FS_ROWS, FOX_HEAD_DIM)
    o_fox_s = _fox_sample(per_req(q_s), cache_k, cache_v, per_req(k_s), per_req(v_s), cq, ck)
    o_fox_s = o_fox_s.reshape(N_S, D_FOX)

    rw = lane_pad(router_w[0])
    rw_hi = rw.astype(BF16)
    rw_lo = (rw - rw_hi.astype(F32)).astype(BF16)
    rb = jnp.concatenate([row(router_b[0]), jnp.full((1, LANES - N_EXPERTS), NEG, F32)], axis=1)
    op_w = (row(g_out_rnn[0]), row(g_out_fox[0]), w_out[0].astype(BF16), row(g_moe[0]), rw_hi, rw_lo, rb)
    h1_p, xn_p, logits_p = _out_proj(o_rnn_p, o_fox_p, xp, TM_OUT, *op_w, name="out_proj_prompt")
    h1_s, xn_s, logits_s = _out_proj(o_rnn_s, o_fox_s, xs, N_S, *op_w, name="out_proj_sample")

    ei, gate, cnt = _route(logits_p, logits_s)
    counts = cnt[0, :N_EXPERTS]
    padded = (counts + EXP_BLK - 1) // EXP_BLK * EXP_BLK
    pad_end = jnp.cumsum(padded)
    pad_start = pad_end - padded
    dest = (pad_start[ei[:, :TOP_K]] + ei[:, TOP_K:2 * TOP_K]).reshape(N_ROWS)
    n_used = (pad_end[-1] // EXP_BLK).reshape(1).astype(I32)
    meta = jnp.concatenate([counts, pad_start, pad_end, n_used]).astype(I32)
    x_sorted = _dispatch(meta, dest, xn_p, xn_s)
    blocks_e = padded // EXP_BLK
    items_e = (blocks_e + EX_BLKS - 1) // EX_BLKS
    item_end = jnp.cumsum(items_e)
    item_first = item_end - items_e
    item_id = jnp.arange(N_ITEMS, dtype=I32)
    item_e = jnp.minimum(jnp.sum((item_end[None, :] <= item_id[:, None]).astype(I32), axis=1), N_EXPERTS - 1)
    item_run = item_id - item_first[item_e]
    item_blk = jnp.clip(pad_start[item_e] // EXP_BLK + EX_BLKS * item_run, 0, N_BLOCKS - 1).astype(I32)
    item_nblk = jnp.clip(blocks_e[item_e] - EX_BLKS * item_run, 1, EX_BLKS).astype(I32)
    n_items = item_end[-1].reshape(1).astype(I32)
    y_sorted = _experts(item_e.astype(I32), item_blk, item_nblk, n_items, n_used, x_sorted,
                        w_gu, b_gu[0][:, None, :], w_down, b_down[0][:, None, :])
    y_p, y_s = _combine_final(dest, y_sorted, h1_p, h1_s, gate,
                              p_prompt[0].reshape(N_P, D_PLE), p_sample[0].reshape(N_S, D_PLE),
                              w_ple_proj[0].astype(BF16), row(g_ple_proj[0]), row(g_ple_gate[0]),
                              w_ple_gate[0].astype(BF16), row(g_final))

    shp = (1, BATCH, SEQ, FOX_HEADS, FOX_HEAD_DIM)
    shs = (1, DEC_BATCH, DEC_SEQ, FOX_HEADS, FOX_HEAD_DIM)
    return (y_p.reshape(BATCH, SEQ, D_MODEL), y_s.reshape(DEC_BATCH, DEC_SEQ, D_MODEL),
            k_p.reshape(shp), v_p.reshape(shp), logf_p[None],
            conv_p[None], hl_p.reshape(1, BATCH, D_RNN),
            k_s.reshape(shs), v_s.reshape(shs), logf_s[None],
            conv_s[None], hl_s.reshape(1, DEC_BATCH, D_RNN))
```

```python
import functools

import jax
import jax.numpy as jnp
from jax import lax
from jax.experimental import pallas as pl
from jax.experimental.pallas import tpu as pltpu

F32 = jnp.float32
BF16 = jnp.bfloat16
I32 = jnp.int32

D_MODEL = 2048
BATCH = 4
SEQ = 2048
DEC_BATCH = 16
DEC_SEQ = 16
PAST_LEN = 4096
D_RNN = 1024
RG_BLOCKS = 8
RG_BLOCK_W = 128
CONV_W = 4
RG_C = 8.0
FOX_HEADS = 8
FOX_HEAD_DIM = 128
D_FOX = 1024
D_MAIN = 2 * D_RNN + 3 * D_FOX
N_EXPERTS = 32
TOP_K = 4
D_FF = 2048
SWIGLU_LIMIT = 7.0
SWIGLU_ALPHA = 1.702
D_PLE = 256
EPS = 1e-6

LANES = 128
N_P = BATCH * SEQ
N_S = DEC_BATCH * DEC_SEQ
N_TOK = N_P + N_S
TM = 256
TM_IN = 1024
TM_OUT = 512
N_TILES = N_TOK // TM
P_TILES = N_P // TM
N_ROWS = N_TOK * TOP_K
EXP_BLK = 128
N_BLOCKS = (N_ROWS + N_EXPERTS * (EXP_BLK - 1) + EXP_BLK - 1) // EXP_BLK
N_PAD = N_BLOCKS * EXP_BLK
X_ROWS = N_PAD + EXP_BLK
NEG = -1e30
VMEM_LIMIT = 48 * 1024 * 1024


def _cparams(sem):
    return pltpu.CompilerParams(dimension_semantics=sem, vmem_limit_bytes=VMEM_LIMIT)


def _rms(x, g):
    return x * lax.rsqrt(jnp.mean(x * x, axis=-1, keepdims=True) + EPS) * g


def _log_sigmoid(z):
    return jnp.minimum(z, 0.0) - jnp.log1p(jnp.exp(-jnp.abs(z)))


def _sigmoid(z):
    return 1.0 / (1.0 + jnp.exp(-z))


def _expm1(x):
    u = jnp.exp(x)
    degenerate = jnp.logical_or(u == 1.0, u == 0.0)
    val = (u - 1.0) * x / jnp.log(jnp.where(degenerate, 0.5, u))
    return jnp.where(u == 1.0, x, jnp.where(u == 0.0, -1.0, val))


def _gelu_tanh(y):
    return 0.5 * y * (1.0 + jnp.tanh(0.7978845608028654 * (y + 0.044715 * (y * y * y))))


def _dot(a, b):
    return jnp.dot(a, b, preferred_element_type=F32)


def _dot_nt(a, b):
    return lax.dot_general(a, b, (((1,), (1,)), ((), ())), preferred_element_type=F32)


IN_TN = 512
IN_NJ = D_MAIN // IN_TN
COL_Q = 2 * D_RNN
COL_K = COL_Q + D_FOX
COL_V = COL_K + D_FOX
J_V = COL_V // IN_TN


def _in_proj_kernel(x_ref, g_ref, w_ref, wf_ref, bf_ref, proj_ref, logf_ref, *rest, with_vt):
    j = pl.program_id(1)
    xn_scr = rest[-1]

    @pl.when(j == 0)
    def _():
        xb = _rms(x_ref[...], g_ref[...]).astype(BF16)
        xn_scr[...] = xb
        logf_ref[...] = _log_sigmoid(_dot(xb, wf_ref[...]) + bf_ref[...])

    r = _dot(xn_scr[...], w_ref[...])
    proj_ref[...] = r
    if with_vt:
        vt_ref = rest[0]

        @pl.when(j >= J_V)
        def _():
            vt_ref[...] = r.T


def _in_proj(x, tm, g_mix, w_main, w_f, b_f, with_vt, name):
    n = x.shape[0]
    row = lambda i, j: (i, 0)
    const = lambda i, j: (0, 0)
    out_specs = [pl.BlockSpec((tm, IN_TN), lambda i, j: (i, j)), pl.BlockSpec((tm, LANES), row)]
    out_shape = [jax.ShapeDtypeStruct((n, D_MAIN), F32), jax.ShapeDtypeStruct((n, LANES), F32)]
    if with_vt:
        out_specs.append(pl.BlockSpec((IN_TN, tm), lambda i, j: (jnp.maximum(j - J_V, 0), i)))
        out_shape.append(jax.ShapeDtypeStruct((D_FOX, n), F32))
    return pl.pallas_call(
        functools.partial(_in_proj_kernel, with_vt=with_vt),
        grid=(n // tm, IN_NJ),
        in_specs=[
            pl.BlockSpec((tm, D_MODEL), row),
            pl.BlockSpec((1, D_MODEL), const),
            pl.BlockSpec((D_MODEL, IN_TN), lambda i, j: (0, j)),
            pl.BlockSpec((D_MODEL, LANES), const),
            pl.BlockSpec((1, LANES), const),
        ],
        out_specs=out_specs,
        out_shape=out_shape,
        scratch_shapes=[pltpu.VMEM((tm, D_MODEL), BF16)],
        compiler_params=_cparams(("arbitrary", "arbitrary")),
        name=name,
    )(x, g_mix, w_main, w_f, b_f)


def _prefix_steps(n):
    s = 1
    while s < n:
        yield s
        s *= 2


def _rglru_kernel(x_ref, y_ref, past_ref, h0_ref, cw_ref, cb_ref, wa_ref, ba_ref, wi_ref, bi_ref, lam_ref,
                  o_ref, conv_ref, hlast_ref, xp_scr, h_scr, *, tt, nt):
    t = pl.program_id(1)
    pad = 8

    @pl.when(t == 0)
    def _():
        xp_scr[pad - (CONV_W - 1):pad, :] = past_ref[0]
        h_scr[...] = h0_ref[0]

    xp_scr[pad:pad + tt, :] = x_ref[...]
    xc = cb_ref[...] + xp_scr[pad - 3:pad - 3 + tt, :] * cw_ref[0:1, :]
    for jw in range(1, CONV_W):
        xc = xc + xp_scr[pad - 3 + jw:pad - 3 + jw + tt, :] * cw_ref[jw:jw + 1, :]

    r_parts, i_parts = [], []
    for n in range(RG_BLOCKS):
        xb = xc[:, n * RG_BLOCK_W:(n + 1) * RG_BLOCK_W].astype(BF16)
        r_parts.append(_dot(xb, wa_ref[n]))
        i_parts.append(_dot(xb, wi_ref[n]))
    r = _sigmoid(jnp.concatenate(r_parts, axis=-1) + ba_ref[...])
    ig = _sigmoid(jnp.concatenate(i_parts, axis=-1) + bi_ref[...])
    log_a = RG_C * r * _log_sigmoid(lam_ref[...])
    a = jnp.exp(log_a)
    b = jnp.sqrt(-_expm1(2.0 * log_a)) * (ig * xc)

    rowi = lax.broadcasted_iota(I32, (tt, D_RNN), 0)
    for s in _prefix_steps(tt):
        keep = rowi >= s
        a_sh = jnp.where(keep, pltpu.roll(a, s, 0), 1.0)
        b_sh = jnp.where(keep, pltpu.roll(b, s, 0), 0.0)
        b = a * b_sh + b
        a = a * a_sh
    h = a * h_scr[...] + b
    h_scr[...] = h[tt - 1:tt, :]
    o_ref[...] = h * _gelu_tanh(y_ref[...])

    tail = xp_scr[pad + tt - (CONV_W - 1):pad + tt, :]
    xp_scr[pad - (CONV_W - 1):pad, :] = tail

    @pl.when(t == nt - 1)
    def _():
        conv_ref[0] = tail
        hlast_ref[0] = h[tt - 1:tt, :]


def _rglru(xy, row0, nb, nt, tt, conv_past, h0, cw, cb, wa, ba, wi, bi, lam, name):
    blk0 = row0 // tt
    vec = lambda b, t: (0, 0)
    w3 = lambda b, t: (0, 0, 0)
    kern = functools.partial(_rglru_kernel, tt=tt, nt=nt)
    return pl.pallas_call(
        kern,
        grid=(nb, nt),
        in_specs=[
            pl.BlockSpec((tt, D_RNN), lambda b, t: (blk0 + b * nt + t, 0)),
            pl.BlockSpec((tt, D_RNN), lambda b, t: (blk0 + b * nt + t, 1)),
            pl.BlockSpec((1, CONV_W - 1, D_RNN), lambda b, t: (b, 0, 0)),
            pl.BlockSpec((1, 1, D_RNN), lambda b, t: (b, 0, 0)),
            pl.BlockSpec((CONV_W, D_RNN), vec),
            pl.BlockSpec((1, D_RNN), vec),
            pl.BlockSpec((RG_BLOCKS, RG_BLOCK_W, RG_BLOCK_W), w3),
            pl.BlockSpec((1, D_RNN), vec),
            pl.BlockSpec((RG_BLOCKS, RG_BLOCK_W, RG_BLOCK_W), w3),
            pl.BlockSpec((1, D_RNN), vec),
            pl.BlockSpec((1, D_RNN), vec),
        ],
        out_specs=[
            pl.BlockSpec((tt, D_RNN), lambda b, t: (b * nt + t, 0)),
            pl.BlockSpec((1, CONV_W - 1, D_RNN), lambda b, t: (b, 0, 0)),
            pl.BlockSpec((1, 1, D_RNN), lambda b, t: (b, 0, 0)),
        ],
        out_shape=[
            jax.ShapeDtypeStruct((nb * nt * tt, D_RNN), F32),
            jax.ShapeDtypeStruct((nb, CONV_W - 1, D_RNN), F32),
            jax.ShapeDtypeStruct((nb, 1, D_RNN), F32),
        ],
        scratch_shapes=[pltpu.VMEM((tt + 8, D_RNN), F32), pltpu.VMEM((1, D_RNN), F32)],
        compiler_params=_cparams(("arbitrary", "arbitrary")),
        name=name,
    )(xy, xy, conv_past, h0, cw, cb, wa, ba, wi, bi, lam)


def _cumsum_kernel(x_ref, o_ref, *, n):
    x = x_ref[...]
    rowi = lax.broadcasted_iota(I32, x.shape, 0)
    for s in _prefix_steps(n):
        x = x + jnp.where(rowi >= s, pltpu.roll(x, s, 0), 0.0)
    o_ref[...] = x


def _cumsum_rows(x, name):
    return pl.pallas_call(
        functools.partial(_cumsum_kernel, n=x.shape[0]),
        out_shape=jax.ShapeDtypeStruct(x.shape, F32),
        compiler_params=pltpu.CompilerParams(vmem_limit_bytes=VMEM_LIMIT),
        name=name,
    )(x)


FP_T = 512
FP_NT = SEQ // FP_T
FOX_SCALE = FOX_HEAD_DIM ** -0.5


FP_HB = 4
FP_W = FP_HB * FOX_HEAD_DIM
FP_GROUPS = FOX_HEADS // FP_HB


def _fox_prompt_kernel(q_ref, k_ref, vt_ref, cq_ref, ck_ref, o_ref, m_scr, l_scr, acc_scr):
    qi = pl.program_id(1)
    ki = pl.program_id(2)

    @pl.when(ki == 0)
    def _():
        m_scr[...] = jnp.full(m_scr.shape, NEG, F32)
        l_scr[...] = jnp.zeros(l_scr.shape, F32)
        acc_scr[...] = jnp.zeros(acc_scr.shape, F32)

    def update(diagonal):
        for hh in range(FP_HB):
            cols = slice(hh * FOX_HEAD_DIM, (hh + 1) * FOX_HEAD_DIM)
            st = _dot_nt(k_ref[:, cols].astype(BF16), q_ref[:, cols].astype(BF16)) * FOX_SCALE
            st = st + (cq_ref[hh] - ck_ref[hh])
            if diagonal:
                key = lax.broadcasted_iota(I32, (FP_T, FP_T), 0)
                qry = lax.broadcasted_iota(I32, (FP_T, FP_T), 1)
                st = jnp.where(key <= qry, st, NEG)
            m_old = m_scr[hh]
            m_new = jnp.maximum(m_old, jnp.max(st, axis=0, keepdims=True))
            alpha = jnp.exp(m_old - m_new)
            p = jnp.exp(st - m_new)
            l_scr[hh] = alpha * l_scr[hh] + jnp.sum(p, axis=0, keepdims=True)
            acc_scr[hh] = alpha * acc_scr[hh] + _dot(vt_ref[cols, :].astype(BF16), p.astype(BF16))
            m_scr[hh] = m_new

    @pl.when(ki < qi)
    def _():
        update(False)

    @pl.when(ki == qi)
    def _():
        update(True)

    @pl.when(ki == FP_NT - 1)
    def _():
        for hh in range(FP_HB):
            cols = slice(hh * FOX_HEAD_DIM, (hh + 1) * FOX_HEAD_DIM)
            o_ref[:, cols] = (acc_scr[hh] / l_scr[hh]).T


def _fox_prompt(proj_p, v_t, c_col, c_row):
    def rows(g, blk):
        return (g // FP_GROUPS) * FP_NT + blk

    def colblk(g, col0):
        return col0 // FP_W + g % FP_GROUPS

    seen = lambda qi, ki: jnp.minimum(ki, qi)
    return pl.pallas_call(
        _fox_prompt_kernel,
        grid=(BATCH * FP_GROUPS, FP_NT, FP_NT),
        in_specs=[
            pl.BlockSpec((FP_T, FP_W), lambda g, qi, ki: (rows(g, qi), colblk(g, COL_Q))),
            pl.BlockSpec((FP_T, FP_W), lambda g, qi, ki: (rows(g, seen(qi, ki)), colblk(g, COL_K))),
            pl.BlockSpec((FP_W, FP_T), lambda g, qi, ki: (g % FP_GROUPS, rows(g, seen(qi, ki)))),
            pl.BlockSpec((FP_HB, 1, FP_T), lambda g, qi, ki: (g, 0, qi)),
            pl.BlockSpec((FP_HB, FP_T, 1), lambda g, qi, ki: (g, seen(qi, ki), 0)),
        ],
        out_specs=pl.BlockSpec((FP_T, FP_W), lambda g, qi, ki: (rows(g, qi), g % FP_GROUPS)),
        out_shape=jax.ShapeDtypeStruct((N_P, D_FOX), F32),
        scratch_shapes=[pltpu.VMEM((FP_HB, 1, FP_T), F32), pltpu.VMEM((FP_HB, 1, FP_T), F32),
                        pltpu.VMEM((FP_HB, FOX_HEAD_DIM, FP_T), F32)],
        compiler_params=_cparams(("arbitrary", "arbitrary", "arbitrary")),
        name="fox_prompt",
    )(proj_p, proj_p, v_t, c_row, c_col)


FS_TK = 1024
FS_NT = PAST_LEN // FS_TK
FS_ROWS = DEC_SEQ * FOX_HEADS


FS_COLS = FS_TK * FOX_HEADS


def _fox_sample_kernel(q_ref, kc_ref, vc_ref, kn_ref, vn_ref, cq_ref, ckp_ref, ckn_ref, o_ref,
                       m_scr, l_scr, acc_scr, own_scr):
    kt = pl.program_id(1)

    def own_head(n):
        rowi = lax.broadcasted_iota(I32, (FS_ROWS, n), 0)
        coli = lax.broadcasted_iota(I32, (FS_ROWS, n), 1)
        return rowi, coli, jnp.bitwise_and(coli, FOX_HEADS - 1) == jnp.bitwise_and(rowi, FOX_HEADS - 1)

    @pl.when(jnp.logical_and(pl.program_id(0) == 0, kt == 0))
    def _():
        own_scr[...] = jnp.where(own_head(FS_COLS)[2], 0.0, NEG)

    @pl.when(kt == 0)
    def _():
        m_scr[...] = jnp.full(m_scr.shape, NEG, F32)
        l_scr[...] = jnp.zeros(l_scr.shape, F32)
        acc_scr[...] = jnp.zeros(acc_scr.shape, F32)

    qb = q_ref[0].astype(BF16)

    def step(k2, v2, ck, new_keys):
        n = k2.shape[0]
        s = _dot_nt(qb, k2.astype(BF16)) * FOX_SCALE + (cq_ref[0] - ck)
        if new_keys:
            rowi, coli, keep = own_head(n)
            s = jnp.where(keep, s, NEG)
            s = jnp.where(jnp.right_shift(coli, 3) <= jnp.right_shift(rowi, 3), s, NEG)
        else:
            s = s + own_scr[...]
        m_old = m_scr[...]
        m_new = jnp.maximum(m_old, jnp.max(s, axis=-1, keepdims=True))
        alpha = jnp.exp(m_old - m_new)
        p = jnp.exp(s - m_new)
        l_scr[...] = alpha * l_scr[...] + jnp.sum(p, axis=-1, keepdims=True)
        acc_scr[...] = alpha * acc_scr[...] + _dot(p.astype(BF16), v2.astype(BF16))
        m_scr[...] = m_new

    step(kc_ref[0, 0].reshape(FS_COLS, FOX_HEAD_DIM), vc_ref[0, 0].reshape(FS_COLS, FOX_HEAD_DIM),
         ckp_ref[0], False)

    @pl.when(kt == FS_NT - 1)
    def _():
        step(kn_ref[0], vn_ref[0], ckn_ref[0], True)
        o_ref[0] = acc_scr[...] / l_scr[...]


def _fox_sample(q_s, cache_k, cache_v, k_n, v_n, cq, ck):
    per_req = lambda b, kt: (b, 0, 0)
    cache = lambda b, kt: (0, b, kt, 0, 0)
    return pl.pallas_call(
        _fox_sample_kernel,
        grid=(DEC_BATCH, FS_NT),
        in_specs=[
            pl.BlockSpec((1, FS_ROWS, FOX_HEAD_DIM), per_req),
            pl.BlockSpec((1, 1, FS_TK, FOX_HEADS, FOX_HEAD_DIM), cache),
            pl.BlockSpec((1, 1, FS_TK, FOX_HEADS, FOX_HEAD_DIM), cache),
            pl.BlockSpec((1, FS_ROWS, FOX_HEAD_DIM), per_req),
            pl.BlockSpec((1, FS_ROWS, FOX_HEAD_DIM), per_req),
            pl.BlockSpec((1, FS_ROWS, 1), per_req),
            pl.BlockSpec((1, 1, FS_COLS), lambda b, kt: (b, 0, kt)),
            pl.BlockSpec((1, 1, FS_ROWS), lambda b, kt: (b, 0, PAST_LEN * FOX_HEADS // FS_ROWS)),
        ],
        out_specs=pl.BlockSpec((1, FS_ROWS, FOX_HEAD_DIM), per_req),
        out_shape=jax.ShapeDtypeStruct((DEC_BATCH, FS_ROWS, FOX_HEAD_DIM), F32),
        scratch_shapes=[pltpu.VMEM((FS_ROWS, 1), F32), pltpu.VMEM((FS_ROWS, 1), F32),
                        pltpu.VMEM((FS_ROWS, FOX_HEAD_DIM), F32), pltpu.VMEM((FS_ROWS, FS_COLS), F32)],
        compiler_params=_cparams(("arbitrary", "arbitrary")),
        name="fox_sample",
    )(q_s, cache_k, cache_v, k_n, v_n, cq, ck, ck)


OP_TN = 1024
OP_NJ = D_MODEL // OP_TN


def _out_proj_kernel(orn_ref, ofx_ref, x_ref, g1_ref, g2_ref, w_ref, gm_ref,
                     rwh_ref, rwl_ref, rb_ref, h_ref, xn_ref, lg_ref, mix_scr):
    j = pl.program_id(1)

    @pl.when(j == 0)
    def _():
        mix_scr[:, :D_RNN] = _rms(orn_ref[...], g1_ref[...]).astype(BF16)
        mix_scr[:, D_RNN:] = _rms(ofx_ref[...], g2_ref[...]).astype(BF16)

    r = _dot(mix_scr[...], w_ref[...])
    for jj in range(OP_NJ):
        @pl.when(j == jj)
        def _(jj=jj):
            h_ref[:, jj * OP_TN:(jj + 1) * OP_TN] = x_ref[:, jj * OP_TN:(jj + 1) * OP_TN] + r

    @pl.when(j == OP_NJ - 1)
    def _():
        xn = _rms(h_ref[...], gm_ref[...])
        xn_ref[...] = xn
        xh = xn.astype(BF16)
        xl = (xn - xh.astype(F32)).astype(BF16)
        lg_ref[...] = (_dot(xh, rwh_ref[...]) + _dot(xl, rwh_ref[...]) + _dot(xh, rwl_ref[...])) + rb_ref[...]


def _out_proj(o_rnn, o_fox, x, tm, g1, g2, w_out, g_moe, rw_hi, rw_lo, rb, name):
    n = x.shape[0]
    row = lambda i, j: (i, 0)
    const = lambda i, j: (0, 0)
    return pl.pallas_call(
        _out_proj_kernel,
        grid=(n // tm, OP_NJ),
        in_specs=[
            pl.BlockSpec((tm, D_RNN), row),
            pl.BlockSpec((tm, D_FOX), row),
            pl.BlockSpec((tm, D_MODEL), row),
            pl.BlockSpec((1, D_RNN), const),
            pl.BlockSpec((1, D_FOX), const),
            pl.BlockSpec((D_MODEL, OP_TN), lambda i, j: (0, j)),
            pl.BlockSpec((1, D_MODEL), const),
            pl.BlockSpec((D_MODEL, LANES), const),
            pl.BlockSpec((D_MODEL, LANES), const),
            pl.BlockSpec((1, LANES), const),
        ],
        out_specs=[
            pl.BlockSpec((tm, D_MODEL), row),
            pl.BlockSpec((tm, D_MODEL), row),
            pl.BlockSpec((tm, LANES), row),
        ],
        out_shape=[
            jax.ShapeDtypeStruct((n, D_MODEL), F32),
            jax.ShapeDtypeStruct((n, D_MODEL), F32),
            jax.ShapeDtypeStruct((n, LANES), F32),
        ],
        scratch_shapes=[pltpu.VMEM((tm, D_MODEL), BF16)],
        compiler_params=_cparams(("arbitrary", "arbitrary")),
        name=name,
    )(o_rnn, o_fox, x, g1, g2, w_out, g_moe, rw_hi, rw_lo, rb)


def _route_kernel(lgp_ref, lgs_ref, ei_ref, gate_ref, cnt_ref, carry_scr):
    t = pl.program_id(0)

    @pl.when(t == 0)
    def _():
        carry_scr[...] = jnp.zeros(carry_scr.shape, F32)

    lane = lax.broadcasted_iota(I32, (TM, LANES), 1)
    lane_f = lane.astype(F32)
    work = jnp.where(t < P_TILES, lgp_ref[...], lgs_ref[...])
    tops, idxs, hots = [], [], []
    for _ in range(TOP_K):
        m = jnp.max(work, axis=-1, keepdims=True)
        idx_f = jnp.min(jnp.where(work == m, lane_f, float(LANES)), axis=-1, keepdims=True)
        hot = lane_f == idx_f
        work = jnp.where(hot, -jnp.inf, work)
        tops.append(m)
        idxs.append(idx_f.astype(I32))
        hots.append(hot)

    es = [jnp.exp(tv - tops[0]) for tv in tops]
    denom = es[0] + es[1] + es[2] + es[3]
    gate = jnp.zeros((TM, LANES), F32)
    for k in range(TOP_K):
        gate = jnp.where(lane == k, es[k] / denom, gate)
    gate_ref[...] = gate

    multi = jnp.zeros((TM, LANES), F32)
    for k in range(TOP_K):
        multi = jnp.where(hots[k], 1.0, multi)
    r_i = lax.broadcasted_iota(I32, (TM, TM), 0)
    c_i = lax.broadcasted_iota(I32, (TM, TM), 1)
    strict_lower = jnp.where(c_i < r_i, 1.0, 0.0).astype(BF16)
    before = _dot(strict_lower, multi.astype(BF16)) + carry_scr[...]
    ei = jnp.zeros((TM, LANES), I32)
    for k in range(TOP_K):
        rank = jnp.sum(jnp.where(hots[k], before, 0.0), axis=-1, keepdims=True).astype(I32)
        ei = jnp.where(lane == k, idxs[k], ei)
        ei = jnp.where(lane == TOP_K + k, rank, ei)
    ei_ref[...] = ei
    carry_scr[...] = carry_scr[...] + jnp.sum(multi, axis=0, keepdims=True)
    cnt_ref[...] = carry_scr[...].astype(I32)


def _prompt_tile(t, *_):
    return (jnp.minimum(t, P_TILES - 1), 0)


def _sample_tile(t, *_):
    return (jnp.maximum(t - P_TILES, 0), 0)


def _route(logits_p, logits_s):
    row = lambda t: (t, 0)
    return pl.pallas_call(
        _route_kernel,
        grid=(N_TILES,),
        in_specs=[pl.BlockSpec((TM, LANES), _prompt_tile), pl.BlockSpec((TM, LANES), _sample_tile)],
        out_specs=[pl.BlockSpec((TM, LANES), row), pl.BlockSpec((TM, LANES), row),
                   pl.BlockSpec((1, LANES), lambda t: (0, 0))],
        out_shape=[jax.ShapeDtypeStruct((N_TOK, LANES), I32), jax.ShapeDtypeStruct((N_TOK, LANES), F32),
                   jax.ShapeDtypeStruct((1, LANES), I32)],
        scratch_shapes=[pltpu.VMEM((1, LANES), F32)],
        compiler_params=_cparams(("arbitrary",)),
        name="route",
    )(logits_p, logits_s)


TILE_ROWS = TM * TOP_K
DMA_GROUP = 8
WAIT_GROUP = 32


def _start_token_rows(make_copy):
    toks_per_trip = DMA_GROUP // TOP_K

    def body(ti, c):
        for u in range(toks_per_trip):
            for k in range(TOP_K):
                make_copy(ti * toks_per_trip + u, k).start(priority=k % 2)
        return c
    lax.fori_loop(0, TM // toks_per_trip, body, 0)


def _wait_rows(n, copy):
    def body(gi, c):
        for _ in range(WAIT_GROUP):
            copy.wait()
        return c
    lax.fori_loop(0, n // WAIT_GROUP, body, 0)


def _dispatch_kernel(meta_ref, dest_ref, xp_ref, xs_ref, out_hbm, zero_scr, sem, zsem):
    t = pl.program_id(0)

    def scatter(x_ref):
        def row_copy(tok, k):
            return pltpu.make_async_copy(x_ref.at[pl.ds(tok, 1)], out_hbm.at[pl.ds(dest_ref[tok * TOP_K + k], 1)],
                                         sem)
        _start_token_rows(row_copy)

    @pl.when(t < P_TILES)
    def _():
        scatter(xp_ref)

    @pl.when(t >= P_TILES)
    def _():
        scatter(xs_ref)

    @pl.when(t == 0)
    def _():
        zero_scr[...] = jnp.zeros(zero_scr.shape, F32)

        def fill(act):
            def per_row(r, c):
                act(pltpu.make_async_copy(zero_scr.at[pl.ds(0, 1)], out_hbm.at[pl.ds(r, 1)], zsem))
                return c

            def per_expert(e, c):
                lax.fori_loop(meta_ref[N_EXPERTS + e] + meta_ref[e], meta_ref[2 * N_EXPERTS + e], per_row, 0)
                return c
            lax.fori_loop(0, N_EXPERTS, per_expert, 0)

            def per_block(g, c):
                rows = pl.ds(pl.multiple_of(g * EXP_BLK, EXP_BLK), EXP_BLK)
                act(pltpu.make_async_copy(zero_scr, out_hbm.at[rows], zsem))
                return c
            lax.fori_loop(meta_ref[3 * N_EXPERTS], X_ROWS // EXP_BLK, per_block, 0)

        fill(lambda cp: cp.start())
        fill(lambda cp: cp.wait())

    _wait_rows(TILE_ROWS, pltpu.make_async_copy(xp_ref.at[pl.ds(0, 1)], out_hbm.at[pl.ds(0, 1)], sem))


def _dispatch(meta, dest_flat, xn_p, xn_s):
    return pl.pallas_call(
        _dispatch_kernel,
        grid_spec=pltpu.PrefetchScalarGridSpec(
            num_scalar_prefetch=1,
            grid=(N_TILES,),
            in_specs=[
                pl.BlockSpec((TILE_ROWS,), lambda t, meta: (t,), memory_space=pltpu.SMEM),
                pl.BlockSpec((TM, D_MODEL), _prompt_tile),
                pl.BlockSpec((TM, D_MODEL), _sample_tile),
            ],
            out_specs=pl.BlockSpec(memory_space=pl.ANY),
            scratch_shapes=[pltpu.VMEM((EXP_BLK, D_MODEL), F32), pltpu.SemaphoreType.DMA(()),
                            pltpu.SemaphoreType.DMA(())],
        ),
        out_shape=jax.ShapeDtypeStruct((X_ROWS, D_MODEL), F32),
        compiler_params=_cparams(("arbitrary",)),
        name="dispatch",
    )(meta, dest_flat, xn_p, xn_s)


EX_BLKS = 12
EX_ROWS = EX_BLKS * EXP_BLK
EX_CHUNK = 2 * EXP_BLK
EX_GROUP = 4 * EXP_BLK
EX_TF = 256
EX_NF = D_FF // EX_TF
EX_TN = 512
N_ITEMS = (N_BLOCKS + (EX_BLKS - 1) * N_EXPERTS) // EX_BLKS
EXPERTS_VMEM_LIMIT = 56 * 1024 * 1024
assert EX_ROWS // EX_CHUNK < EX_NF


def _experts_kernel(ie_ref, ib_ref, inb_ref, ni_ref, nu_ref,
                    x_hbm, wg_ref, wu_ref, bg_ref, bu_ref, wd_ref, bd_ref, y_hbm,
                    stage_scr, xb_scr, acc_scr, wgb_scr, wub_scr, wdb_scr, xsem, osem, zsem):
    del ie_ref
    w = pl.program_id(0)
    f = pl.program_id(1)
    n_items = ni_ref[0]
    slot = jnp.bitwise_and(w, 1)

    def rows_at(row0, n):
        return pl.ds(pl.multiple_of(row0, EXP_BLK), n)

    def n_chunks(item):
        return lax.shift_right_logical(inb_ref[item] + 1, 1)

    def x_copy(item, c):
        st = jnp.bitwise_and(c, 1)
        return pltpu.make_async_copy(x_hbm.at[rows_at(ib_ref[item] * EXP_BLK + c * EX_CHUNK, EX_CHUNK)],
                                     stage_scr.at[st], xsem.at[st])

    def stage_to_bf16(half, c):
        xb_scr[half, rows_at(c * EX_CHUNK, EX_CHUNK), :] = stage_scr[jnp.bitwise_and(c, 1)].astype(BF16)

    def out_copy(item, row0, m):
        return pltpu.make_async_copy(acc_scr.at[rows_at(row0, m)],
                                     y_hbm.at[rows_at(ib_ref[item] * EXP_BLK + row0, m)], osem)

    def for_row_groups(nb, act):
        n_big = lax.shift_right_logical(nb, 2)
        rem = jnp.bitwise_and(nb, 3)
        n_plain = jnp.maximum(n_big - 1, 0)

        def per_group(c, carry):
            act(c * EX_GROUP, EX_GROUP)
            return carry
        lax.fori_loop(0, n_plain, per_group, 0)

        for r in range(4):
            @pl.when(jnp.logical_and(n_big >= 1, rem == r))
            def _(r=r):
                act(n_plain * EX_GROUP, EX_GROUP + r * EXP_BLK)

        @pl.when(jnp.logical_and(n_big == 0, rem >= 2))
        def _():
            act(n_plain * EX_GROUP, EX_CHUNK)

        @pl.when(jnp.logical_and(n_big == 0, jnp.bitwise_and(rem, 1) == 1))
        def _():
            act((nb - 1) * EXP_BLK, EXP_BLK)

    def wait_out_copies(item):
        for_row_groups(inb_ref[item], lambda row0, m: out_copy(item, row0, m).wait())

    @pl.when(jnp.logical_and(w == 0, f == 0))
    def _():
        acc_scr[0:EXP_BLK, :] = jnp.zeros((EXP_BLK, D_MODEL), F32)

        def tail(act):
            def body(g, c):
                act(pltpu.make_async_copy(acc_scr.at[pl.ds(0, EXP_BLK)], y_hbm.at[rows_at(g * EXP_BLK, EXP_BLK)],
                                          zsem))
                return c
            lax.fori_loop(nu_ref[0], N_BLOCKS, body, 0)
        tail(lambda cp: cp.start())
        tail(lambda cp: cp.wait())

        def first_rows(c, carry):
            x_copy(0, c).start()
            x_copy(0, c).wait()
            stage_to_bf16(0, c)
            return carry
        lax.fori_loop(0, n_chunks(0), first_rows, 0)

    @pl.when(w < n_items)
    def _():
        nxt = jnp.minimum(w + 1, n_items - 1)
        stages_next = jnp.where(w + 1 < n_items, n_chunks(nxt), 0)

        @pl.when(jnp.logical_and(f >= 1, f - 1 < stages_next))
        def _():
            x_copy(nxt, f - 1).wait()
            stage_to_bf16(1 - slot, f - 1)

        @pl.when(f < stages_next)
        def _():
            x_copy(nxt, f).start()

        wgb_scr[...] = wg_ref[0, 0].astype(BF16)
        wub_scr[...] = wu_ref[0, 0].astype(BF16)
        wdb_scr[...] = wd_ref[0, 0].astype(BF16)

        @pl.when(jnp.logical_and(f == 0, w > 0))
        def _():
            wait_out_copies(w - 1)

        def ffn_rows(row0, m):
            rows = rows_at(row0, m)
            xb = xb_scr[slot, rows, :]
            gg = jnp.minimum(_dot(xb, wgb_scr[...]) + bg_ref[0], SWIGLU_LIMIT)
            uu = jnp.clip(_dot(xb, wub_scr[...]) + bu_ref[0], -SWIGLU_LIMIT, SWIGLU_LIMIT)
            hb = ((uu + 1.0) * (gg * _sigmoid(SWIGLU_ALPHA * gg))).astype(BF16)
            for c in range(D_MODEL // EX_TN):
                cols = slice(c * EX_TN, (c + 1) * EX_TN)
                part = _dot(hb, wdb_scr[:, cols])
                base = jnp.where(f == 0, jnp.broadcast_to(bd_ref[0][:, cols], (m, EX_TN)), acc_scr[rows, cols])
                acc_scr[rows, cols] = base + part

            @pl.when(f == EX_NF - 1)
            def _():
                out_copy(w, row0, m).start()

        for_row_groups(inb_ref[w], ffn_rows)

    @pl.when(jnp.logical_and(w == N_ITEMS - 1, f == EX_NF - 1))
    def _():
        wait_out_copies(n_items - 1)


def _experts(item_e, item_blk, item_nblk, n_items, n_used, x_sorted, w_gu, b_gu, w_down, b_down):
    def item(w, ni):
        return jnp.minimum(w, ni[0] - 1)

    def fsel(w, f, ni):
        return jnp.where(w < ni[0], f, EX_NF - 1)

    return pl.pallas_call(
        _experts_kernel,
        grid_spec=pltpu.PrefetchScalarGridSpec(
            num_scalar_prefetch=5,
            grid=(N_ITEMS, EX_NF),
            in_specs=[
                pl.BlockSpec(memory_space=pl.ANY),
                pl.BlockSpec((1, 1, D_MODEL, EX_TF),
                             lambda w, f, ie, ib, ins, ni, nu: (0, ie[item(w, ni)], 0, fsel(w, f, ni))),
                pl.BlockSpec((1, 1, D_MODEL, EX_TF),
                             lambda w, f, ie, ib, ins, ni, nu: (0, ie[item(w, ni)], 0, EX_NF + fsel(w, f, ni))),
                pl.BlockSpec((1, 1, EX_TF), lambda w, f, ie, ib, ins, ni, nu: (ie[item(w, ni)], 0, fsel(w, f, ni))),
                pl.BlockSpec((1, 1, EX_TF),
                             lambda w, f, ie, ib, ins, ni, nu: (ie[item(w, ni)], 0, EX_NF + fsel(w, f, ni))),
                pl.BlockSpec((1, 1, EX_TF, D_MODEL),
                             lambda w, f, ie, ib, ins, ni, nu: (0, ie[item(w, ni)], fsel(w, f, ni), 0)),
                pl.BlockSpec((1, 1, D_MODEL), lambda w, f, ie, ib, ins, ni, nu: (ie[item(w, ni)], 0, 0)),
            ],
            out_specs=pl.BlockSpec(memory_space=pl.ANY),
            scratch_shapes=[
                pltpu.VMEM((2, EX_CHUNK, D_MODEL), F32),
                pltpu.VMEM((2, EX_ROWS, D_MODEL), BF16),
                pltpu.VMEM((EX_ROWS, D_MODEL), F32),
                pltpu.VMEM((D_MODEL, EX_TF), BF16),
                pltpu.VMEM((D_MODEL, EX_TF), BF16),
                pltpu.VMEM((EX_TF, D_MODEL), BF16),
                pltpu.SemaphoreType.DMA((2,)),
                pltpu.SemaphoreType.DMA(()),
                pltpu.SemaphoreType.DMA(()),
            ],
        ),
        out_shape=jax.ShapeDtypeStruct((N_PAD, D_MODEL), F32),
        compiler_params=pltpu.CompilerParams(dimension_semantics=("arbitrary", "arbitrary"),
                                             vmem_limit_bytes=EXPERTS_VMEM_LIMIT),
        name="experts",
    )(item_e, item_blk, item_nblk, n_items, n_used, x_sorted, w_gu, w_gu, b_gu, b_gu, w_down, b_down)


PL_TN = 1024
PL_NJ = D_MODEL // PL_TN
COMBINE_VMEM_LIMIT = 56 * 1024 * 1024


def _combine_final_kernel(dcur_ref, dnext_ref, y_hbm, hp_ref, hs_ref, gate_ref, pp_ref, ps_ref,
                          wp_ref, gp_ref, gg_ref, wg_ref, gf_ref, yp_ref, ys_ref,
                          buf, sems, h_scr, hn_scr, e_scr):
    t = pl.program_id(0)
    j = pl.program_id(1)
    slot = jnp.bitwise_and(t, 1)

    def row_copy(d_ref, sl):
        def make(tok, k):
            return pltpu.make_async_copy(y_hbm.at[pl.ds(d_ref[tok * TOP_K + k], 1)], buf.at[sl, k, pl.ds(tok, 1)],
                                         sems.at[sl])
        return make

    @pl.when(jnp.logical_and(t == 0, j == 0))
    def _():
        _start_token_rows(row_copy(dcur_ref, 0))

    @pl.when(jnp.logical_and(t + 1 < N_TILES, j == 0))
    def _():
        _start_token_rows(row_copy(dnext_ref, 1 - slot))

    @pl.when(j == 0)
    def _():
        _wait_rows(TILE_ROWS,
                   pltpu.make_async_copy(y_hbm.at[pl.ds(0, 1)], buf.at[slot, 0, pl.ds(0, 1)], sems.at[slot]))
        gate = gate_ref[...]
        h2 = jnp.where(t < P_TILES, hp_ref[...], hs_ref[...])
        for k in range(TOP_K):
            h2 = h2 + gate[:, k:k + 1] * buf[slot, k]
        h_scr[...] = h2
        hn_scr[...] = _rms(h2, gg_ref[...]).astype(BF16)
        p = jnp.where(t < P_TILES, pp_ref[...], ps_ref[...])
        e_scr[...] = _rms(_dot(p.astype(BF16), wp_ref[...]), gp_ref[...])

    ple_gate = _sigmoid(_dot(hn_scr[...], wg_ref[...]))
    for jj in range(PL_NJ):
        @pl.when(j == jj)
        def _(jj=jj):
            sl = slice(jj * PL_TN, (jj + 1) * PL_TN)
            h_scr[:, sl] = h_scr[:, sl] + ple_gate * e_scr[:, sl]

    @pl.when(jnp.logical_and(j == PL_NJ - 1, t < P_TILES))
    def _():
        yp_ref[...] = _rms(h_scr[...], gf_ref[...])

    @pl.when(jnp.logical_and(j == PL_NJ - 1, t >= P_TILES))
    def _():
        ys_ref[...] = _rms(h_scr[...], gf_ref[...])


def _combine_final(dest_flat, y_sorted, h_p, h_s, gate, p_p, p_s, w_proj, g_proj, g_gate, w_gate, g_final):
    const = lambda t, j: (0, 0)
    return pl.pallas_call(
        _combine_final_kernel,
        grid=(N_TILES, PL_NJ),
        in_specs=[
            pl.BlockSpec((TILE_ROWS,), lambda t, j: (t,), memory_space=pltpu.SMEM),
            pl.BlockSpec((TILE_ROWS,), lambda t, j: (jnp.minimum(t + 1, N_TILES - 1),), memory_space=pltpu.SMEM),
            pl.BlockSpec(memory_space=pl.ANY),
            pl.BlockSpec((TM, D_MODEL), _prompt_tile),
            pl.BlockSpec((TM, D_MODEL), _sample_tile),
            pl.BlockSpec((TM, LANES), lambda t, j: (t, 0)),
            pl.BlockSpec((TM, D_PLE), _prompt_tile),
            pl.BlockSpec((TM, D_PLE), _sample_tile),
            pl.BlockSpec((D_PLE, D_MODEL), const),
            pl.BlockSpec((1, D_MODEL), const),
            pl.BlockSpec((1, D_MODEL), const),
            pl.BlockSpec((D_MODEL, PL_TN), lambda t, j: (0, j)),
            pl.BlockSpec((1, D_MODEL), const),
        ],
        out_specs=[pl.BlockSpec((TM, D_MODEL), _prompt_tile), pl.BlockSpec((TM, D_MODEL), _sample_tile)],
        out_shape=[jax.ShapeDtypeStruct((N_P, D_MODEL), F32), jax.ShapeDtypeStruct((N_S, D_MODEL), F32)],
        scratch_shapes=[pltpu.VMEM((2, TOP_K, TM, D_MODEL), F32), pltpu.SemaphoreType.DMA((2,)),
                        pltpu.VMEM((TM, D_MODEL), F32), pltpu.VMEM((TM, D_MODEL), BF16),
                        pltpu.VMEM((TM, D_MODEL), F32)],
        compiler_params=pltpu.CompilerParams(dimension_semantics=("arbitrary", "arbitrary"),
                                             vmem_limit_bytes=COMBINE_VMEM_LIMIT),
        name="combine_final",
    )(dest_flat, dest_flat, y_sorted, h_p, h_s, gate, p_p, p_s, w_proj, g_proj, g_gate, w_gate, g_final)


def kernel(x_prompt, x_sample, cache_k, cache_v, cache_logf, state_conv, state_rglru, p_prompt, p_sample,
           g_mix, w_in, b_fgate, conv_w, conv_b, rg_w_a, rg_b_a, rg_w_i, rg_b_i, rg_lambda,
           g_out_rnn, g_out_fox, w_out, g_moe, router_w, router_b, w_gu, b_gu, w_down, b_down,
           w_ple_proj, g_ple_proj, g_ple_gate, w_ple_gate, g_final):
    assert g_mix.shape[0] == 1, "one layer"
    row = lambda v: v.reshape(1, -1)
    lane_pad = lambda a: jnp.pad(a, ((0, 0), (0, LANES - a.shape[1])))

    xp = x_prompt.reshape(N_P, D_MODEL)
    xs = x_sample.reshape(N_S, D_MODEL)

    w_main = w_in[0][:, :D_MAIN].astype(BF16)
    w_f = lane_pad(w_in[0][:, D_MAIN:]).astype(BF16)
    b_f = lane_pad(row(b_fgate[0]))
    proj_p, lf_p, vt_p = _in_proj(xp, TM_IN, row(g_mix[0]), w_main, w_f, b_f, True, "in_proj_prompt")
    proj_s, lf_s = _in_proj(xs, N_S, row(g_mix[0]), w_main, w_f, b_f, False, "in_proj_sample")
    logf_p = lf_p[:, :FOX_HEADS].reshape(BATCH, SEQ, FOX_HEADS)
    logf_s = lf_s[:, :FOX_HEADS].reshape(DEC_BATCH, DEC_SEQ, FOX_HEADS)
    k_p, v_p = proj_p[:, COL_K:COL_V], proj_p[:, COL_V:]
    q_s, k_s, v_s = proj_s[:, COL_Q:COL_K], proj_s[:, COL_K:COL_V], proj_s[:, COL_V:]

    rg = (conv_w[0], row(conv_b[0]), rg_w_a[0].astype(BF16), row(rg_b_a[0]), rg_w_i[0].astype(BF16),
          row(rg_b_i[0]), row(rg_lambda[0]))
    o_rnn_p, conv_p, hl_p = _rglru(proj_p, 0, BATCH, SEQ // TM, TM,
                                   jnp.zeros((BATCH, CONV_W - 1, D_RNN), F32), jnp.zeros((BATCH, 1, D_RNN), F32),
                                   *rg, name="rglru_prompt")
    o_rnn_s, conv_s, hl_s = _rglru(proj_s, 0, DEC_BATCH, 1, DEC_SEQ,
                                   state_conv[0], state_rglru[0].reshape(DEC_BATCH, 1, D_RNN),
                                   *rg, name="rglru_sample")

    c_p = _cumsum_rows(logf_p.transpose(1, 0, 2).reshape(SEQ, BATCH * FOX_HEADS), "cumsum_prompt")
    c_pT = c_p.T
    o_fox_p = _fox_prompt(proj_p, vt_p, c_pT[:, :, None], c_pT[:, None, :])

    lf_all = jnp.concatenate([cache_logf[0], logf_s], axis=1)
    c_s = _cumsum_rows(lf_all.transpose(1, 0, 2).reshape(PAST_LEN + DEC_SEQ, DEC_BATCH * FOX_HEADS),
                       "cumsum_sample")
    c_bth = c_s.reshape(PAST_LEN + DEC_SEQ, DEC_BATCH, FOX_HEADS).transpose(1, 0, 2)
    ck = c_bth.reshape(DEC_BATCH, 1, (PAST_LEN + DEC_SEQ) * FOX_HEADS)
    cq = c_bth[:, PAST_LEN:].reshape(DEC_BATCH, FS_ROWS, 1)
    per_req = lambda a: a.reshape(DEC_BATCH, FS_ROWS, FOX_HEAD_DIM)
    o_fox_s = _fox_sample(per_req(q_s), cache_k, cache_v, per_req(k_s), per_req(v_s), cq, ck)
    o_fox_s = o_fox_s.reshape(N_S, D_FOX)

    rw = lane_pad(router_w[0])
    rw_hi = rw.astype(BF16)
    rw_lo = (rw - rw_hi.astype(F32)).astype(BF16)
    rb = jnp.concatenate([row(router_b[0]), jnp.full((1, LANES - N_EXPERTS), NEG, F32)], axis=1)
    op_w = (row(g_out_rnn[0]), row(g_out_fox[0]), w_out[0].astype(BF16), row(g_moe[0]), rw_hi, rw_lo, rb)
    h1_p, xn_p, logits_p = _out_proj(o_rnn_p, o_fox_p, xp, TM_OUT, *op_w, name="out_proj_prompt")
    h1_s, xn_s, logits_s = _out_proj(o_rnn_s, o_fox_s, xs, N_S, *op_w, name="out_proj_sample")

    ei, gate, cnt = _route(logits_p, logits_s)
    counts = cnt[0, :N_EXPERTS]
    padded = (counts + EXP_BLK - 1) // EXP_BLK * EXP_BLK
    pad_end = jnp.cumsum(padded)
    pad_start = pad_end - padded
    dest = (pad_start[ei[:, :TOP_K]] + ei[:, TOP_K:2 * TOP_K]).reshape(N_ROWS)
    n_used = (pad_end[-1] // EXP_BLK).reshape(1).astype(I32)
    meta = jnp.concatenate([counts, pad_start, pad_end, n_used]).astype(I32)
    x_sorted = _dispatch(meta, dest, xn_p, xn_s)
    blocks_e = padded // EXP_BLK
    items_e = (blocks_e + EX_BLKS - 1) // EX_BLKS
    item_end = jnp.cumsum(items_e)
    item_first = item_end - items_e
    item_id = jnp.arange(N_ITEMS, dtype=I32)
    item_e = jnp.minimum(jnp.sum((item_end[None, :] <= item_id[:, None]).astype(I32), axis=1), N_EXPERTS - 1)
    item_run = item_id - item_first[item_e]
    item_blk = jnp.clip(pad_start[item_e] // EXP_BLK + EX_BLKS * item_run, 0, N_BLOCKS - 1).astype(I32)
    item_nblk = jnp.clip(blocks_e[item_e] - EX_BLKS * item_run, 1, EX_BLKS).astype(I32)
    n_items = item_end[-1].reshape(1).astype(I32)
    y_sorted = _experts(item_e.astype(I32), item_blk, item_nblk, n_items, n_used, x_sorted,
                        w_gu, b_gu[0][:, None, :], w_down, b_down[0][:, None, :])
    y_p, y_s = _combine_final(dest, y_sorted, h1_p, h1_s, gate,
                              p_prompt[0].reshape(N_P, D_PLE), p_sample[0].reshape(N_S, D_PLE),
                              w_ple_proj[0].astype(BF16), row(g_ple_proj[0]), row(g_ple_gate[0]),
                              w_ple_gate[0].astype(BF16), row(g_final))

    shp = (1, BATCH, SEQ, FOX_HEADS, FOX_HEAD_DIM)
    shs = (1, DEC_BATCH, DEC_SEQ, FOX_HEADS, FOX_HEAD_DIM)
    return (y_p.reshape(BATCH, SEQ, D_MODEL), y_s.reshape(DEC_BATCH, DEC_SEQ, D_MODEL),
            k_p.reshape(shp), v_p.reshape(shp), logf_p[None],
            conv_p[None], hl_p.reshape(1, BATCH, D_RNN),
            k_s.reshape(shs), v_s.reshape(shs), logf_s[None],
            conv_s[None], hl_s.reshape(1, DEC_BATCH, D_RNN))
```
